```python
import math
import jax, jax.numpy as jnp
from jax import lax
import numpy as np

D_MODEL = 2048
BATCH = 2
SEQ = 4096
DEPTH = 2
DEC_BATCH = 8
DEC_SEQ = 4
PAST_LEN = 16384
PAGE_SIZE = 128

CONV_WIDTH = 3
WINDOWS = (128, 512, 2048)
DILATIONS = (1, 4, 16)
N_GROUPS = len(WINDOWS)
HEAD_DIM = 128
N_HEADS = D_MODEL // HEAD_DIM
ATTN_BLOCK = 128
ATTN_SCALE = HEAD_DIM ** -0.5
ROPE_THETA = 10000.0
NORM_EPS = 1e-6
PEER_HEADS = 8
PEER_N_KEYS = 128
PEER_N_EXPERTS = PEER_N_KEYS ** 2
PEER_TOPK = 16
PEER_D_KEY = 256
PEER_D_HALF = PEER_D_KEY // 2
PEER_CHUNK = 128

kernel_name = 'hybrid_shortconv_dilated_swa_peer_step'


def rms_norm(x, g):
    x32 = x.astype(jnp.float32)
    y = x32 * lax.rsqrt(jnp.mean(x32 * x32, axis=-1, keepdims=True) + NORM_EPS)
    return (y * g.astype(jnp.float32)).astype(x.dtype)


def rope(x, pos):
    half = HEAD_DIM // 2
    inv_freq = ROPE_THETA ** (-jnp.arange(half, dtype=jnp.float32) / half)
    ang = pos.astype(jnp.float32)[..., None] * inv_freq
    cos = jnp.cos(ang)[..., None, :]
    sin = jnp.sin(ang)[..., None, :]
    x32 = x.astype(jnp.float32)
    x1, x2 = x32[..., :half], x32[..., half:]
    return jnp.concatenate([x1 * cos - x2 * sin, x2 * cos + x1 * sin], axis=-1).astype(x.dtype)


def short_conv_mixer(x, state, w_in, w_conv, w_out):
    b_gate, c_gate, h = jnp.split(x @ w_in, 3, axis=-1)
    u = c_gate * h
    u_ext = jnp.concatenate([state.astype(u.dtype), u], axis=1)
    t = x.shape[1]
    conv = sum(w_conv[k] * u_ext[:, k:k + t] for k in range(CONV_WIDTH))
    y = (b_gate * conv) @ w_out
    return y, u_ext[:, -(CONV_WIDTH - 1):]


def attn_qkv(xn, pos, w_qkv, q_gain, k_gain):
    b, t, _ = xn.shape
    qkv = (xn @ w_qkv).reshape(b, t, N_GROUPS, 3, N_HEADS, HEAD_DIM)
    q = rms_norm(qkv[:, :, :, 0], q_gain[:, None, :])
    k = rms_norm(qkv[:, :, :, 1], k_gain[:, None, :])
    v = qkv[:, :, :, 2]
    gh = N_GROUPS * N_HEADS
    q = rope(q.reshape(b, t, gh, HEAD_DIM), pos).reshape(b, t, N_GROUPS, N_HEADS, HEAD_DIM)
    k = rope(k.reshape(b, t, gh, HEAD_DIM), pos).reshape(b, t, N_GROUPS, N_HEADS, HEAD_DIM)
    return q, k, v


def dilated_prompt(q, k, v, window, dilation):
    b, s, h, c = q.shape
    n = s // dilation
    ws = window // dilation
    qb_len = ATTN_BLOCK
    nb = -(-n // qb_len)
    pad = nb * qb_len - n

    def to_blocks(a):
        a = a.astype(jnp.float32).reshape(b, n, dilation, h, c).transpose(0, 2, 1, 3, 4)
        a = jnp.pad(a, ((0, 0), (0, 0), (0, pad), (0, 0), (0, 0)))
        return a.reshape(b, dilation, nb, qb_len, h, c)

    def with_prev(a):
        prev = jnp.pad(a, ((0, 0), (0, 0), (1, 0), (0, 0), (0, 0), (0, 0)))[:, :, :-1]
        return jnp.concatenate([prev, a], axis=3)

    qb = to_blocks(q)
    kc = with_prev(to_blocks(k))
    vc = with_prev(to_blocks(v))
    i = jnp.arange(qb_len)[:, None]
    j = jnp.arange(2 * qb_len)[None, :]
    band = (j <= i + qb_len) & (j >= i + qb_len - ws)
    first = (jnp.arange(nb)[:, None, None] > 0) | (j[None] >= qb_len)
    mask = band[None] & first
    sc = jnp.einsum('bdnqhc,bdnkhc->bdnhqk', qb, kc) * ATTN_SCALE
    sc = jnp.where(mask[None, None, :, None], sc, -jnp.inf)
    lse = jax.nn.logsumexp(sc, axis=-1)
    p = jnp.exp(sc - lse[..., None])
    o = jnp.einsum('bdnhqk,bdnkhc->bdnqhc', p, vc)

    def from_blocks(a):
        a = a.reshape((b, dilation, nb * qb_len) + a.shape[4:])[:, :, :n]
        a = jnp.swapaxes(a, 1, 2)
        return a.reshape((b, s) + a.shape[3:])

    return from_blocks(o), from_blocks(jnp.swapaxes(lse, 3, 4))


def dilated_sample(q, k_all, v_all, window, dilation, start):
    t = q.shape[1]
    pos_q = PAST_LEN + jnp.arange(t)
    steps = jnp.arange(window // dilation + 1) * dilation
    key_pos = pos_q[:, None] - steps[None, :]
    valid = key_pos >= 0
    idx = jnp.clip(key_pos - start, 0, k_all.shape[1] - 1)
    kg = k_all[:, idx].astype(jnp.float32)
    vg = v_all[:, idx].astype(jnp.float32)
    sc = jnp.einsum('bthc,btjhc->bthj', q.astype(jnp.float32), kg) * ATTN_SCALE
    sc = jnp.where(valid[None, :, None, :], sc, -jnp.inf)
    lse = jax.nn.logsumexp(sc, axis=-1)
    p = jnp.exp(sc - lse[..., None])
    return jnp.einsum('bthj,btjhc->bthc', p, vg), lse


def combine_groups(outs, lses):
    w = jax.nn.softmax(jnp.stack(lses, axis=0), axis=0)
    return jnp.einsum('gbth,gbthc->bthc', w, jnp.stack(outs, axis=0))


def attn_prompt(xn, w_qkv, q_gain, k_gain, w_o):
    b, s, _ = xn.shape
    q, k, v = attn_qkv(xn, jnp.arange(s), w_qkv, q_gain, k_gain)
    outs, lses, states = [], [], []
    for g, (win, dil) in enumerate(zip(WINDOWS, DILATIONS)):
        o, l = dilated_prompt(q[:, :, g], k[:, :, g], v[:, :, g], win, dil)
        outs.append(o)
        lses.append(l)
        keep = min(win, s)
        states.append(jnp.stack([k[:, -keep:, g], v[:, -keep:, g]], axis=2))
    y = combine_groups(outs, lses).reshape(b, s, N_HEADS * HEAD_DIM).astype(xn.dtype) @ w_o
    return y, states


def attn_sample(xn, caches, w_qkv, q_gain, k_gain, w_o):
    b, t, _ = xn.shape
    q, k, v = attn_qkv(xn, PAST_LEN + jnp.arange(t), w_qkv, q_gain, k_gain)
    outs, lses, states = [], [], []
    for g, (win, dil) in enumerate(zip(WINDOWS, DILATIONS)):
        cache = caches[g]
        keep = cache.shape[1]
        k_all = jnp.concatenate([cache[:, :, 0].astype(k.dtype), k[:, :, g]], axis=1)
        v_all = jnp.concatenate([cache[:, :, 1].astype(v.dtype), v[:, :, g]], axis=1)
        o, l = dilated_sample(q[:, :, g], k_all, v_all, win, dil, PAST_LEN - keep)
        outs.append(o)
        lses.append(l)
        states.append(jnp.stack([k_all[:, -keep:], v_all[:, -keep:]], axis=2))
    y = combine_groups(outs, lses).reshape(b, t, N_HEADS * HEAD_DIM).astype(xn.dtype) @ w_o
    return y, states


def peer_ffn(x, w_q, sub_keys, u, v):
    shp = x.shape
    xf = x.reshape(-1, shp[-1])
    n = xf.shape[0]
    q = (xf @ w_q).astype(jnp.float32).reshape(n, PEER_HEADS, 2, PEER_D_HALF)
    s = jnp.einsum('nrpc,rpkc->nrpk', q, sub_keys.astype(jnp.float32))
    s1, i1 = lax.top_k(s[:, :, 0], PEER_TOPK)
    s2, i2 = lax.top_k(s[:, :, 1], PEER_TOPK)
    cand = (s1[..., :, None] + s2[..., None, :]).reshape(n, PEER_HEADS, PEER_TOPK * PEER_TOPK)
    cidx = (i1[..., :, None] * PEER_N_KEYS + i2[..., None, :]).reshape(n, PEER_HEADS, PEER_TOPK * PEER_TOPK)
    best, sel = lax.top_k(cand, PEER_TOPK)
    experts = jnp.take_along_axis(cidx, sel, axis=-1)
    gates = jax.nn.softmax(best, axis=-1)
    n_chunks = -(-n // PEER_CHUNK)
    pad = n_chunks * PEER_CHUNK - n

    def chunked(a):
        a = jnp.pad(a, ((0, pad),) + ((0, 0),) * (a.ndim - 1))
        return a.reshape((n_chunks, PEER_CHUNK) + a.shape[1:])

    def expert_block(args):
        xc, ec, gc = args
        hc = jnp.einsum('cd,crkd->crk', xc.astype(jnp.float32), u[ec].astype(jnp.float32))
        a = gc * jax.nn.gelu(hc, approximate=False)
        return jnp.einsum('crk,crkd->cd', a, v[ec].astype(jnp.float32))

    y = lax.map(expert_block, (chunked(xf), chunked(experts), chunked(gates)))
    return y.reshape(-1, shp[-1])[:n].astype(x.dtype).reshape(shp)


def setup_inputs(seed: int = 0) -> dict:
    key = jax.random.key(seed)
    ks = jax.random.split(key, 19)

    def nrm(k, shape, scale):
        return jax.random.normal(k, shape, jnp.float32) * scale

    qkv_cols = N_GROUPS * 3 * N_HEADS * HEAD_DIM
    kv_shape = lambda w: (DEC_BATCH, min(w, PAST_LEN), 2, N_HEADS, HEAD_DIM)
    return {
        'x_prompt': nrm(ks[0], (BATCH, SEQ, D_MODEL), 1.0),
        'x_sample': nrm(ks[1], (DEC_BATCH, DEC_SEQ, D_MODEL), 1.0),
        'state_conv': nrm(ks[2], (DEC_BATCH, CONV_WIDTH - 1, D_MODEL), 1.0),
        'cache_kv_w128': nrm(ks[3], kv_shape(WINDOWS[0]), 1.0),
        'cache_kv_w512': nrm(ks[4], kv_shape(WINDOWS[1]), 1.0),
        'cache_kv_w2048': nrm(ks[5], kv_shape(WINDOWS[2]), 1.0),
        'norm_mix': 1.0 + nrm(ks[6], (DEPTH, D_MODEL), 0.1),
        'norm_ffn': 1.0 + nrm(ks[7], (DEPTH, D_MODEL), 0.1),
        'conv_w_in': nrm(ks[8], (D_MODEL, 3 * D_MODEL), D_MODEL ** -0.5),
        'conv_w': nrm(ks[9], (CONV_WIDTH, D_MODEL), CONV_WIDTH ** -0.5),
        'conv_w_out': nrm(ks[10], (D_MODEL, D_MODEL), D_MODEL ** -0.5),
        'attn_w_qkv': nrm(ks[11], (D_MODEL, qkv_cols), D_MODEL ** -0.5),
        'attn_q_norm': 1.0 + nrm(ks[12], (N_GROUPS, HEAD_DIM), 0.1),
        'attn_k_norm': 1.0 + nrm(ks[13], (N_GROUPS, HEAD_DIM), 0.1),
        'attn_w_o': nrm(ks[14], (N_HEADS * HEAD_DIM, D_MODEL), (N_HEADS * HEAD_DIM) ** -0.5),
        'peer_w_q': nrm(ks[15], (DEPTH, D_MODEL, PEER_HEADS * PEER_D_KEY), D_MODEL ** -0.5),
        'peer_sub_keys': nrm(ks[16], (DEPTH, PEER_HEADS, 2, PEER_N_KEYS, PEER_D_HALF), PEER_D_HALF ** -0.5),
        'peer_u': nrm(ks[17], (DEPTH, PEER_N_EXPERTS, D_MODEL), D_MODEL ** -0.5),
        'peer_v': nrm(ks[18], (DEPTH, PEER_N_EXPERTS, D_MODEL), PEER_HEADS ** -0.5),
    }


def reference(x_prompt, x_sample, state_conv, cache_kv_w128, cache_kv_w512, cache_kv_w2048,
              norm_mix, norm_ffn, conv_w_in, conv_w, conv_w_out,
              attn_w_qkv, attn_q_norm, attn_k_norm, attn_w_o,
              peer_w_q, peer_sub_keys, peer_u, peer_v):
    xp, xs = x_prompt, x_sample
    kv_caches = (cache_kv_w128, cache_kv_w512, cache_kv_w2048)
    for i in range(DEPTH):
        hp = rms_norm(xp, norm_mix[i])
        hs = rms_norm(xs, norm_mix[i])
        if i % 2 == 0:
            zero_state = jnp.zeros((hp.shape[0], CONV_WIDTH - 1, D_MODEL), hp.dtype)
            yp, conv_state_prompt = short_conv_mixer(hp, zero_state, conv_w_in, conv_w, conv_w_out)
            ys, conv_state_sample = short_conv_mixer(hs, state_conv, conv_w_in, conv_w, conv_w_out)
        else:
            yp, kv_prompt = attn_prompt(hp, attn_w_qkv, attn_q_norm, attn_k_norm, attn_w_o)
            ys, kv_sample = attn_sample(hs, kv_caches, attn_w_qkv, attn_q_norm, attn_k_norm, attn_w_o)
        xp = xp + yp
        xs = xs + ys
        xp = xp + peer_ffn(rms_norm(xp, norm_ffn[i]), peer_w_q[i], peer_sub_keys[i], peer_u[i], peer_v[i])
        xs = xs + peer_ffn(rms_norm(xs, norm_ffn[i]), peer_w_q[i], peer_sub_keys[i], peer_u[i], peer_v[i])
    return (xp, xs, conv_state_prompt, conv_state_sample,
            kv_prompt[0], kv_prompt[1], kv_prompt[2],
            kv_sample[0], kv_sample[1], kv_sample[2])
```

```python
import functools
import math

import jax
import jax.numpy as jnp
from jax import lax
from jax.experimental import pallas as pl
from jax.experimental.pallas import tpu as pltpu

F32 = jnp.float32
BF16 = jnp.bfloat16

V7X_LANES = 128
V7X_SUBLANES = 8
V7X_VMEM_BYTES = 64 * 1024 * 1024
VMEM_LIMIT_BYTES = V7X_VMEM_BYTES - 8 * 1024 * 1024

NORM_EPS = 1e-6
CONV_WIDTH = 3
WINDOWS = (128, 512, 2048)
DILATIONS = (1, 4, 16)
N_GROUPS = 3
HEAD_DIM = 128
ATTN_BLOCK = 128
ATTN_SCALE = HEAD_DIM ** -0.5
ROPE_THETA = 10000.0
PAST_LEN = 16384
assert PAST_LEN >= max(WINDOWS)
PEER_HEADS = 8
PEER_N_KEYS = 128
PEER_TOPK = 16
PEER_D_HALF = 128
SQRT_HALF = math.sqrt(0.5)
NOT_SELECTED = 64.0
NEG_INF = float("-inf")

NT_DIMS = (((1,), (1,)), ((), ()))
TN_DIMS = (((0,), (0,)), ((), ()))


def _params(*semantics):
    return pltpu.CompilerParams(dimension_semantics=semantics, vmem_limit_bytes=VMEM_LIMIT_BYTES)


def _rms_norm(x, g):
    return x * lax.rsqrt(jnp.mean(x * x, axis=-1, keepdims=True) + NORM_EPS) * g


def _split_dot(a, b01):
    hi = a.astype(BF16)
    lo = (a - hi.astype(F32)).astype(BF16)
    return (jnp.dot(hi, b01, preferred_element_type=F32)
            + jnp.dot(lo, b01, preferred_element_type=F32))


def _conv_in_kernel(x_ref, g_ref, wb_ref, wc_ref, wh_ref, b_ref, u_ref, xn_ref):
    @pl.when(pl.program_id(1) == 0)
    def _():
        xn_ref[...] = _rms_norm(x_ref[...], g_ref[...]).astype(BF16)

    xn = xn_ref[...]
    b_ref[...] = jnp.dot(xn, wb_ref[...], preferred_element_type=F32)
    c = jnp.dot(xn, wc_ref[...], preferred_element_type=F32)
    h = jnp.dot(xn, wh_ref[...], preferred_element_type=F32)
    u_ref[...] = c * h


def conv_in(x, g, w_in, *, tm, tn):
    n, d = x.shape
    nj = d // tn
    return pl.pallas_call(
        _conv_in_kernel,
        grid=(n // tm, nj),
        in_specs=[
            pl.BlockSpec((tm, d), lambda i, j: (i, 0)),
            pl.BlockSpec((1, d), lambda i, j: (0, 0)),
            pl.BlockSpec((d, tn), lambda i, j: (0, j)),
            pl.BlockSpec((d, tn), lambda i, j: (0, j + nj)),
            pl.BlockSpec((d, tn), lambda i, j: (0, j + 2 * nj)),
        ],
        out_specs=[pl.BlockSpec((tm, tn), lambda i, j: (i, j)),
                   pl.BlockSpec((tm, tn), lambda i, j: (i, j))],
        out_shape=[jax.ShapeDtypeStruct((n, d), F32), jax.ShapeDtypeStruct((n, d), F32)],
        scratch_shapes=[pltpu.VMEM((tm, d), BF16)],
        compiler_params=_params("arbitrary", "arbitrary"),
        name="conv_in",
    )(x, g, w_in, w_in, w_in)


def _conv_mix(cw_ref, um2, um1, u0):
    return cw_ref[0:1, :] * um2 + cw_ref[1:2, :] * um1 + cw_ref[2:3, :] * u0


def _conv_out_prompt_kernel(x_ref, b_ref, u_ref, up_ref, st_ref, cw_ref, wo_ref, o_ref, *, tiles_per_seq):
    i = pl.program_id(0)
    u = u_ref[...]
    first = (i % tiles_per_seq) == 0
    prev = jnp.where(first, st_ref[...], up_ref[V7X_SUBLANES - 2:V7X_SUBLANES, :])
    row = lax.broadcasted_iota(jnp.int32, (u.shape[0], 1), 0)
    um1 = jnp.where(row == 0, prev[1:2, :], pltpu.roll(u, 1, axis=0))
    um2 = jnp.where(row == 0, prev[0:1, :], jnp.where(row == 1, prev[1:2, :], pltpu.roll(u, 2, axis=0)))
    a = (b_ref[...] * _conv_mix(cw_ref, um2, um1, u)).astype(BF16)
    o_ref[...] = x_ref[...] + jnp.dot(a, wo_ref[...], preferred_element_type=F32)


def conv_out_prompt(x, bgate, u, state, conv_w, w_out, *, seq_len, tm):
    n, d = x.shape
    tps = seq_len // tm
    sub_per_tile = tm // V7X_SUBLANES
    tile = pl.BlockSpec((tm, d), lambda i: (i, 0))
    return pl.pallas_call(
        functools.partial(_conv_out_prompt_kernel, tiles_per_seq=tps),
        grid=(n // tm,),
        in_specs=[
            tile, tile, tile,
            pl.BlockSpec((V7X_SUBLANES, d), lambda i: (jnp.maximum(i * sub_per_tile - 1, 0), 0)),
            pl.BlockSpec((None, CONV_WIDTH - 1, d), lambda i: (i // tps, 0, 0)),
            pl.BlockSpec((CONV_WIDTH, d), lambda i: (0, 0)),
            pl.BlockSpec((d, d), lambda i: (0, 0)),
        ],
        out_specs=tile,
        out_shape=jax.ShapeDtypeStruct((n, d), F32),
        compiler_params=_params("arbitrary"),
        name="conv_out_prompt",
    )(x, bgate, u, u, state, conv_w, w_out)


def _conv_out_sample_kernel(x_ref, b_ref, um2_ref, um1_ref, u_ref, cw_ref, wo_ref, o_ref):
    a = (b_ref[...] * _conv_mix(cw_ref, um2_ref[...], um1_ref[...], u_ref[...])).astype(BF16)
    o_ref[...] = x_ref[...] + jnp.dot(a, wo_ref[...], preferred_element_type=F32)


def conv_out_sample(x, bgate, um2, um1, u0, conv_w, w_out):
    n, d = x.shape
    full = pl.BlockSpec((n, d), lambda i: (0, 0))
    return pl.pallas_call(
        _conv_out_sample_kernel,
        grid=(1,),
        in_specs=[full, full, full, full, full,
                  pl.BlockSpec((CONV_WIDTH, d), lambda i: (0, 0)),
                  pl.BlockSpec((d, d), lambda i: (0, 0))],
        out_specs=full,
        out_shape=jax.ShapeDtypeStruct((n, d), F32),
        compiler_params=_params("arbitrary"),
        name="conv_out_sample",
    )(x, bgate, um2, um1, u0, conv_w, w_out)


CAND_ROWS = 72


def _cand_flat_index():
    pairs = []
    for b in range(4):
        pairs += [(a, b) for a in range(16 if b == 0 else 8)]
    pairs += [(0, b) if b >= 4 else None for b in range(16)]
    pairs += [(1, b) if b >= 4 else None for b in range(8)]
    pairs += [(2, b) if b == 4 else None for b in range(8)]
    rows = [a * PEER_TOPK + b if pair is not None and (a + 1) * (b + 1) <= PEER_TOPK else 256 + i
            for i, pair in enumerate(pairs) for a, b in [pair or (99, 99)]]
    assert len(rows) == CAND_ROWS and len({r for r in rows if r < 256}) == 50
    return jnp.broadcast_to(jnp.asarray(rows, F32).reshape(CAND_ROWS, 1), (CAND_ROWS, V7X_LANES))


def _top16(s, key_iota):
    work = s
    rank = jnp.full(s.shape, NOT_SELECTED, F32)
    val_row = lax.broadcasted_iota(jnp.int32, (PEER_TOPK, s.shape[1]), 0)
    vals = jnp.zeros((PEER_TOPK, s.shape[1]), F32)
    for a in range(PEER_TOPK):
        m = jnp.max(work, axis=0, keepdims=True)
        first = jnp.min(jnp.where(work == m, key_iota, float(PEER_N_KEYS)), axis=0, keepdims=True)
        sel = key_iota == first
        rank = jnp.where(sel, float(a), rank)
        work = jnp.where(sel, NEG_INF, work)
        vals = jnp.where(val_row == a, m, vals)
    return vals, rank


def _peer_select(s1, s2, fidx):
    lanes = s1.shape[1]
    key_iota = lax.broadcasted_iota(jnp.int32, (PEER_N_KEYS, lanes), 0).astype(F32)
    v1, rank1 = _top16(s1, key_iota)
    v2, rank2 = _top16(s2, key_iota)

    r16 = lax.broadcasted_iota(jnp.int32, (16, lanes), 0)
    r8 = lax.broadcasted_iota(jnp.int32, (8, lanes), 0)
    v1_lo, v2_lo = v1[0:8, :], v2[0:8, :]
    cand = jnp.concatenate([
        v1 + v2[0:1, :],
        v1_lo + v2[1:2, :],
        jnp.where(r8 < 5, v1_lo + v2[2:3, :], NEG_INF),
        jnp.where(r8 < 4, v1_lo + v2[3:4, :], NEG_INF),
        jnp.where(r16 >= 4, v1[0:1, :] + v2, NEG_INF),
        jnp.where(r8 >= 4, v1[1:2, :] + v2_lo, NEG_INF),
        jnp.where(r8 == 4, v1[2:3, :] + v2_lo, NEG_INF),
    ], axis=0)
    c_top = v1[0:1, :] + v2[0:1, :]
    taken = jnp.zeros(cand.shape, F32)
    z = jnp.zeros((1, lanes), F32)
    for _ in range(PEER_TOPK):
        m = jnp.max(cand, axis=0, keepdims=True)
        first = jnp.min(jnp.where(cand == m, fidx, 1024.0), axis=0, keepdims=True)
        sel = fidx == first
        taken = jnp.where(sel, 1.0, taken)
        cand = jnp.where(sel, NEG_INF, cand)
        z = z + jnp.exp(m - c_top)

    extra = jnp.where(r8 == 0, jnp.sum(taken[40:56, :], axis=0, keepdims=True),
                      jnp.where(r8 == 1, jnp.sum(taken[56:64, :], axis=0, keepdims=True),
                                jnp.where(r8 == 2, jnp.sum(taken[64:72, :], axis=0, keepdims=True), 0.0)))
    cnt_lo = taken[0:8, :] + taken[16:24, :] + taken[24:32, :] + taken[32:40, :] + extra
    cnt_vec = jnp.concatenate([cnt_lo, taken[8:16, :]], axis=0)
    cnt = jnp.zeros(s1.shape, F32)
    for a in range(PEER_TOPK):
        cnt = jnp.where(rank1 == float(a), cnt_vec[a:a + 1, :], cnt)

    e1n = jnp.exp(s1 - v1[0:1, :]) / z
    e2 = jnp.exp(s2 - v2[0:1, :])
    return rank2, e2, cnt, e1n


def _peer_query_kernel(x_ref, g_ref, wq_ref, keys_ref, fidx_ref,
                       xn_ref, rank2_ref, e2_ref, cnt_ref, e1n_ref, sc_ref):
    tm = x_ref.shape[0]
    xn = _rms_norm(x_ref[...], g_ref[...]).astype(BF16)
    xn_ref[...] = xn
    q = jnp.dot(xn, wq_ref[...], preferred_element_type=F32).astype(BF16)
    for rp in range(2 * PEER_HEADS):
        q_rp = q[:, rp * PEER_D_HALF:(rp + 1) * PEER_D_HALF]
        sc_ref[rp] = lax.dot_general(keys_ref[rp], q_rp, NT_DIMS, preferred_element_type=F32)

    fidx = fidx_ref[...]

    def head_body(r, carry):
        for s in range(tm // V7X_LANES):
            cols = slice(s * V7X_LANES, (s + 1) * V7X_LANES)
            rank2, e2, cnt, e1n = _peer_select(sc_ref[2 * r, :, cols], sc_ref[2 * r + 1, :, cols], fidx)
            rank2_ref[r, :, cols] = rank2
            e2_ref[r, :, cols] = e2
            cnt_ref[r, :, cols] = cnt
            e1n_ref[r, :, cols] = e1n
        return carry

    lax.fori_loop(0, PEER_HEADS, head_body, 0)


def peer_query(x, g, w_q, keys, *, tm):
    n, d = x.shape
    dq = w_q.shape[1]
    sel = jax.ShapeDtypeStruct((PEER_HEADS, PEER_N_KEYS, n), F32)
    sel_spec = pl.BlockSpec((PEER_HEADS, PEER_N_KEYS, tm), lambda i: (0, 0, i))
    return pl.pallas_call(
        _peer_query_kernel,
        grid=(n // tm,),
        in_specs=[
            pl.BlockSpec((tm, d), lambda i: (i, 0)),
            pl.BlockSpec((1, d), lambda i: (0, 0)),
            pl.BlockSpec((d, dq), lambda i: (0, 0)),
            pl.BlockSpec((2 * PEER_HEADS, PEER_N_KEYS, PEER_D_HALF), lambda i: (0, 0, 0)),
            pl.BlockSpec((CAND_ROWS, V7X_LANES), lambda i: (0, 0)),
        ],
        out_specs=[pl.BlockSpec((tm, d), lambda i: (i, 0)), sel_spec, sel_spec, sel_spec, sel_spec],
        out_shape=[jax.ShapeDtypeStruct((n, d), BF16), sel, sel, sel, sel],
        scratch_shapes=[pltpu.VMEM((2 * PEER_HEADS, PEER_N_KEYS, tm), F32)],
        compiler_params=_params("arbitrary"),
        name="peer_query",
    )(x, g, w_q, keys, _cand_flat_index())


def _peer_expert_kernel(xn_ref, x_ref, rank2_ref, e2_ref, cnt_ref, e1n_ref, u_ref, v_ref,
                        o_ref, acc_ref, h_ref, a_ref, cnt_s, e1n_s):
    c = pl.program_id(1)
    te, tm = h_ref.shape
    slabs = te // PEER_N_KEYS

    @pl.when(c == 0)
    def _():
        acc_ref[...] = jnp.zeros_like(acc_ref)

    h_ref[...] = lax.dot_general(u_ref[...], xn_ref[...], NT_DIMS, preferred_element_type=F32)

    for k in range(slabs):
        for r in range(PEER_HEADS):
            cnt_s[k, r:r + 1, :] = cnt_ref[r, k:k + 1, :]
            e1n_s[k, r:r + 1, :] = e1n_ref[r, k:k + 1, :]

    def slab_body(k, carry):
        rows = pl.ds(pl.multiple_of(k * PEER_N_KEYS, PEER_N_KEYS), PEER_N_KEYS)
        cnt_k, e1n_k = cnt_s[k], e1n_s[k]
        for s in range(tm // V7X_LANES):
            cols = slice(s * V7X_LANES, (s + 1) * V7X_LANES)
            gate = jnp.zeros((PEER_N_KEYS, V7X_LANES), F32)
            for r in range(PEER_HEADS):
                live = rank2_ref[r, :, cols] < cnt_k[r:r + 1, cols]
                gate = gate + jnp.where(live, e2_ref[r, :, cols], 0.0) * e1n_k[r:r + 1, cols]
            h = h_ref[rows, cols]
            act = 0.5 * h * (1.0 + lax.erf(h * SQRT_HALF))
            a_ref[rows, cols] = (gate * act).astype(BF16)
        return carry

    lax.fori_loop(0, te // PEER_N_KEYS, slab_body, 0)
    acc_ref[...] += lax.dot_general(a_ref[...], v_ref[...], TN_DIMS, preferred_element_type=F32)

    @pl.when(c == pl.num_programs(1) - 1)
    def _():
        o_ref[...] = x_ref[...] + acc_ref[...]


def peer_experts(xn, x, rank2, e2, cnt, e1n, u, v, *, tm, te):
    n, d = x.shape
    ne = u.shape[0]
    slabs = te // PEER_N_KEYS
    tok = pl.BlockSpec((tm, d), lambda i, c: (i, 0))
    by_i2 = pl.BlockSpec((PEER_HEADS, PEER_N_KEYS, tm), lambda i, c: (0, 0, i))
    by_i1 = pl.BlockSpec((PEER_HEADS, slabs, tm), lambda i, c: (0, c, i))
    wts = pl.BlockSpec((te, d), lambda i, c: (c, 0))
    return pl.pallas_call(
        _peer_expert_kernel,
        grid=(n // tm, ne // te),
        in_specs=[tok, tok, by_i2, by_i2, by_i1, by_i1, wts, wts],
        out_specs=tok,
        out_shape=jax.ShapeDtypeStruct((n, d), F32),
        scratch_shapes=[pltpu.VMEM((tm, d), F32), pltpu.VMEM((te, tm), F32), pltpu.VMEM((te, tm), BF16),
                        pltpu.VMEM((slabs, PEER_HEADS, tm), F32), pltpu.VMEM((slabs, PEER_HEADS, tm), F32)],
        compiler_params=_params("arbitrary", "arbitrary"),
        name="peer_experts",
    )(xn, x, rank2, e2, cnt, e1n, u, v)


def peer_layer(x, g, w_q, keys, u, v, *, tm, te):
    xn, rank2, e2, cnt, e1n = peer_query(x, g, w_q, keys, tm=tm)
    return peer_experts(xn, x, rank2, e2, cnt, e1n, u, v, tm=tm, te=te)


def _qkv_kernel(x_ref, g_ref, w_ref, qg_ref, kg_ref, cos_ref, sin_ref, o_ref, xn_ref):
    j = pl.program_id(1)

    @pl.when(j == 0)
    def _():
        xn_ref[...] = _rms_norm(x_ref[...], g_ref[...]).astype(BF16)

    y = jnp.dot(xn_ref[...], w_ref[...], preferred_element_type=F32)
    kind = j % 3

    @pl.when(kind == 2)
    def _():
        o_ref[...] = y

    @pl.when(kind < 2)
    def _():
        gain = jnp.where(kind == 0, qg_ref[...], kg_ref[...])
        cos, sin = cos_ref[...], sin_ref[...]
        for h in range(y.shape[1] // HEAD_DIM):
            cols = slice(h * HEAD_DIM, (h + 1) * HEAD_DIM)
            yn = _rms_norm(y[:, cols], gain)
            o_ref[:, cols] = yn * cos + pltpu.roll(yn, HEAD_DIM // 2, axis=1) * sin


def qkv_project(x, g, w_qkv, q_gain, k_gain, cos, sin, *, tm, rope_tiles):
    n, d = x.shape
    cols = w_qkv.shape[1]
    tn = cols // (3 * N_GROUPS)
    gain = pl.BlockSpec((None, 1, HEAD_DIM), lambda i, j: (j // 3, 0, 0))
    rope = pl.BlockSpec((tm, HEAD_DIM), lambda i, j: (i % rope_tiles, 0))
    return pl.pallas_call(
        _qkv_kernel,
        grid=(n // tm, 3 * N_GROUPS),
        in_specs=[
            pl.BlockSpec((tm, d), lambda i, j: (i, 0)),
            pl.BlockSpec((1, d), lambda i, j: (0, 0)),
            pl.BlockSpec((d, tn), lambda i, j: (0, j)),
            gain, gain, rope, rope,
        ],
        out_specs=pl.BlockSpec((tm, tn), lambda i, j: (i, j)),
        out_shape=jax.ShapeDtypeStruct((n, cols), F32),
        scratch_shapes=[pltpu.VMEM((tm, d), BF16)],
        compiler_params=_params("arbitrary", "arbitrary"),
        name="qkv_project",
    )(x, g, w_qkv, q_gain.reshape(N_GROUPS, 1, HEAD_DIM), k_gain.reshape(N_GROUPS, 1, HEAD_DIM), cos, sin)


def _spread_heads(cols_list, width):
    rows = cols_list[0].shape[0]
    lane = lax.broadcasted_iota(jnp.int32, (rows, width), 1)
    out = jnp.zeros((rows, width), F32)
    for h, col in enumerate(cols_list):
        out = jnp.where(lane == h, col, out)
    return out


def _attn_prompt_kernel(q_ref, kc_ref, kp_ref, vc_ref, vp_ref, o_ref, lse_ref, *, window_blocks):
    i = pl.program_id(2)
    qb = q_ref.shape[0]
    qi = lax.broadcasted_iota(jnp.int32, (qb, 2 * qb), 0)
    kj = lax.broadcasted_iota(jnp.int32, (qb, 2 * qb), 1)
    live = (kj <= qi + qb) & (kj >= qi + qb - window_blocks) & ((i > 0) | (kj >= qb))
    lses = []
    for h in range(q_ref.shape[1] // HEAD_DIM):
        cols = slice(h * HEAD_DIM, (h + 1) * HEAD_DIM)
        q = q_ref[:, cols].astype(BF16)
        k = jnp.concatenate([kp_ref[:, cols], kc_ref[:, cols]], axis=0).astype(BF16)
        v = jnp.concatenate([vp_ref[:, cols], vc_ref[:, cols]], axis=0).astype(BF16)
        s = lax.dot_general(q, k, NT_DIMS, preferred_element_type=F32) * ATTN_SCALE
        s = jnp.where(live, s, NEG_INF)
        m = jnp.max(s, axis=-1, keepdims=True)
        p = jnp.exp(s - m)
        l = jnp.sum(p, axis=-1, keepdims=True)
        o_ref[:, cols] = jnp.dot(p.astype(BF16), v, preferred_element_type=F32) / l
        lses.append(m + jnp.log(l))
    lse_ref[...] = _spread_heads(lses, lse_ref.shape[1])


def attn_prompt_group(qkv, group, *, batch, seq_len):
    dil, win = DILATIONS[group], WINDOWS[group]
    cols = qkv.shape[1]
    hd = cols // (3 * N_GROUPS)
    n_strided = seq_len // dil
    assert n_strided % ATTN_BLOCK == 0 and win // dil <= ATTN_BLOCK
    nb = n_strided // ATTN_BLOCK
    per_row = cols // hd
    view = qkv.reshape(batch, n_strided, dil * cols)

    def col(kind):
        return lambda b, r, i: (b, i, r * per_row + 3 * group + kind)

    def col_prev(kind):
        return lambda b, r, i: (b, jnp.maximum(i - 1, 0), r * per_row + 3 * group + kind)

    blk = (None, ATTN_BLOCK, hd)
    o, lse = pl.pallas_call(
        functools.partial(_attn_prompt_kernel, window_blocks=win // dil),
        grid=(batch, dil, nb),
        in_specs=[pl.BlockSpec(blk, col(0)),
                  pl.BlockSpec(blk, col(1)), pl.BlockSpec(blk, col_prev(1)),
                  pl.BlockSpec(blk, col(2)), pl.BlockSpec(blk, col_prev(2))],
        out_specs=[pl.BlockSpec(blk, lambda b, r, i: (b, i, r)),
                   pl.BlockSpec((None, ATTN_BLOCK, V7X_LANES), lambda b, r, i: (b, i, r))],
        out_shape=[jax.ShapeDtypeStruct((batch, n_strided, dil * hd), F32),
                   jax.ShapeDtypeStruct((batch, n_strided, dil * V7X_LANES), F32)],
        compiler_params=_params("arbitrary", "arbitrary", "arbitrary"),
        name=f"attn_prompt_g{group}",
    )(view, view, view, view, view)
    return o.reshape(batch * seq_len, hd), lse.reshape(batch * seq_len, V7X_LANES)


def _head_segments(n_heads, transpose):
    shape = (V7X_LANES, n_heads * HEAD_DIM) if transpose else (n_heads * HEAD_DIM, V7X_LANES)
    c = lax.broadcasted_iota(jnp.int32, shape, 1 if transpose else 0)
    h = lax.broadcasted_iota(jnp.int32, shape, 0 if transpose else 1)
    return jnp.where((c >= h * HEAD_DIM) & (c < (h + 1) * HEAD_DIM), 1.0, 0.0).astype(BF16)


def _attn_sample_kernel(qkv_ref, c0_ref, c1_ref, c2_ref, o_ref, lse_ref):
    t = pl.program_id(1)
    hd = o_ref.shape[2]
    n_heads = hd // HEAD_DIM
    t_new = qkv_ref.shape[0]
    seg = _head_segments(n_heads, transpose=False)
    seg_t = _head_segments(n_heads, transpose=True)
    row_c = lax.broadcasted_iota(jnp.int32, (c0_ref.shape[0], V7X_LANES), 0)
    row_n = lax.broadcasted_iota(jnp.int32, (t_new, V7X_LANES), 0)
    for g, cache_ref in enumerate((c0_ref, c1_ref, c2_ref)):
        base = 3 * g * hd
        q = qkv_ref[pl.ds(t, 1), base:base + hd]
        k_new = qkv_ref[:, base + hd:base + 2 * hd]
        v_new = qkv_ref[:, base + 2 * hd:base + 3 * hd]
        k_old, v_old = cache_ref[:, 0:hd], cache_ref[:, hd:2 * hd]
        s_old = _split_dot(k_old * q, seg) * ATTN_SCALE
        s_new = _split_dot(k_new * q, seg) * ATTN_SCALE
        if DILATIONS[g] == 1:
            s_old = jnp.where(row_c >= t, s_old, NEG_INF)
            s_new = jnp.where(row_n <= t, s_new, NEG_INF)
        else:
            s_new = jnp.where(row_n == t, s_new, NEG_INF)
        m = jnp.maximum(jnp.max(s_old, axis=0, keepdims=True), jnp.max(s_new, axis=0, keepdims=True))
        p_old, p_new = jnp.exp(s_old - m), jnp.exp(s_new - m)
        l = jnp.sum(p_old, axis=0, keepdims=True) + jnp.sum(p_new, axis=0, keepdims=True)
        w_old = _split_dot(p_old / l, seg_t)
        w_new = _split_dot(p_new / l, seg_t)
        o = jnp.sum(w_old * v_old, axis=0, keepdims=True) + jnp.sum(w_new * v_new, axis=0, keepdims=True)
        o_ref[g, pl.ds(t, 1), :] = o
        lse_ref[g, pl.ds(t, 1), :] = m + jnp.log(l)


def attn_sample(qkv, caches, *, batch, t_new):
    cols = qkv.shape[1]
    hd = cols // (3 * N_GROUPS)
    views, specs = [], []
    for g, cache in enumerate(caches):
        dil, keep = DILATIONS[g], cache.shape[1]
        assert keep == WINDOWS[g] and keep // dil == ATTN_BLOCK and (dil == 1 or t_new <= dil)
        views.append(cache.reshape(batch, keep // dil, dil * 2 * hd))
        if dil == 1:
            specs.append(pl.BlockSpec((None, ATTN_BLOCK, 2 * hd), lambda b, t: (b, 0, 0)))
        else:
            specs.append(pl.BlockSpec((None, ATTN_BLOCK, 2 * hd), lambda b, t: (b, 0, t)))
    rows_new = -(-t_new // V7X_SUBLANES) * V7X_SUBLANES
    qkv_rows = jnp.pad(qkv.reshape(batch, t_new, cols), ((0, 0), (0, rows_new - t_new), (0, 0)))
    o, lse = pl.pallas_call(
        _attn_sample_kernel,
        grid=(batch, t_new),
        in_specs=[pl.BlockSpec((None, rows_new, cols), lambda b, t: (b, 0, 0))] + specs,
        out_specs=[pl.BlockSpec((N_GROUPS, None, t_new, hd), lambda b, t: (0, b, 0, 0)),
                   pl.BlockSpec((N_GROUPS, None, t_new, V7X_LANES), lambda b, t: (0, b, 0, 0))],
        out_shape=[jax.ShapeDtypeStruct((N_GROUPS, batch, t_new, hd), F32),
                   jax.ShapeDtypeStruct((N_GROUPS, batch, t_new, V7X_LANES), F32)],
        compiler_params=_params("arbitrary", "arbitrary"),
        name="attn_sample",
    )(qkv_rows, *views)
    return o.reshape(N_GROUPS, batch * t_new, hd), lse.reshape(N_GROUPS, batch * t_new, V7X_LANES)


def _attn_out_kernel(x_ref, o_ref, lse_ref, wo_ref, y_ref):
    hd = o_ref.shape[2]
    seg_t = _head_segments(hd // HEAD_DIM, transpose=True)
    lse = [lse_ref[g] for g in range(N_GROUPS)]
    m = functools.reduce(jnp.maximum, lse)
    e = [jnp.exp(l - m) for l in lse]
    tot = functools.reduce(jnp.add, e)
    mix = jnp.zeros(o_ref.shape[1:], F32)
    for g in range(N_GROUPS):
        mix = mix + _split_dot(e[g] / tot, seg_t) * o_ref[g]
    y_ref[...] = x_ref[...] + jnp.dot(mix.astype(BF16), wo_ref[...], preferred_element_type=F32)


def attn_out(x, o, lse, w_o, *, tm):
    n, d = x.shape
    hd = o.shape[2]
    return pl.pallas_call(
        _attn_out_kernel,
        grid=(n // tm,),
        in_specs=[pl.BlockSpec((tm, d), lambda i: (i, 0)),
                  pl.BlockSpec((N_GROUPS, tm, hd), lambda i: (0, i, 0)),
                  pl.BlockSpec((N_GROUPS, tm, V7X_LANES), lambda i: (0, i, 0)),
                  pl.BlockSpec((hd, d), lambda i: (0, 0))],
        out_specs=pl.BlockSpec((tm, d), lambda i: (i, 0)),
        out_shape=jax.ShapeDtypeStruct((n, d), F32),
        compiler_params=_params("arbitrary"),
        name="attn_out",
    )(x, o, lse, w_o)


def _rope_tables(pos):
    half = HEAD_DIM // 2
    inv_freq = ROPE_THETA ** (-jnp.arange(half, dtype=F32) / half)
    ang = pos.astype(F32)[:, None] * inv_freq
    cos, sin = jnp.cos(ang), jnp.sin(ang)
    return jnp.concatenate([cos, cos], axis=-1), jnp.concatenate([-sin, sin], axis=-1)


def _pad_rows(a, rows):
    return jnp.pad(a, ((0, rows - a.shape[0]),) + ((0, 0),) * (a.ndim - 1))


PROMPT_TM = 512
ATTN_OUT_TM = 256
PEER_TE = 1024
CONV_TN = 512
SAMPLE_PEER_ROWS = V7X_LANES


def kernel(x_prompt, x_sample, state_conv, cache_kv_w128, cache_kv_w512, cache_kv_w2048,
           norm_mix, norm_ffn, conv_w_in, conv_w, conv_w_out,
           attn_w_qkv, attn_q_norm, attn_k_norm, attn_w_o,
           peer_w_q, peer_sub_keys, peer_u, peer_v):
    batch, seq, d = x_prompt.shape
    dec_batch, dec_seq, _ = x_sample.shape
    n_p, n_s = batch * seq, dec_batch * dec_seq
    caches = (cache_kv_w128, cache_kv_w512, cache_kv_w2048)
    n_heads = caches[0].shape[3]
    hd = n_heads * HEAD_DIM

    w_in, w_out = conv_w_in.astype(BF16), conv_w_out.astype(BF16)
    w_qkv, w_o = attn_w_qkv.astype(BF16), attn_w_o.astype(BF16)
    w_q = peer_w_q.astype(BF16)
    keys = peer_sub_keys.astype(BF16).reshape(peer_sub_keys.shape[0], 2 * PEER_HEADS, PEER_N_KEYS, PEER_D_HALF)
    pu, pv = peer_u.astype(BF16), peer_v.astype(BF16)
    g_mix = norm_mix.reshape(-1, 1, d)
    g_ffn = norm_ffn.reshape(-1, 1, d)

    xp = x_prompt.reshape(n_p, d)
    xs = x_sample.reshape(n_s, d)

    def peer(x, layer, tm):
        return peer_layer(x, g_ffn[layer], w_q[layer], keys[layer], pu[layer], pv[layer], tm=tm, te=PEER_TE)

    def peer_sample(x, layer):
        return peer(_pad_rows(x, SAMPLE_PEER_ROWS), layer, SAMPLE_PEER_ROWS)[:n_s]

    bp, up = conv_in(xp, g_mix[0], w_in, tm=PROMPT_TM, tn=CONV_TN)
    xp = conv_out_prompt(xp, bp, up, jnp.zeros((batch, CONV_WIDTH - 1, d), F32), conv_w, w_out,
                         seq_len=seq, tm=PROMPT_TM)
    conv_state_prompt = up.reshape(batch, seq, d)[:, seq - (CONV_WIDTH - 1):]

    bs, us = conv_in(xs, g_mix[0], w_in, tm=n_s, tn=CONV_TN)
    u_ext = jnp.concatenate([state_conv, us.reshape(dec_batch, dec_seq, d)], axis=1)
    shifted = [u_ext[:, k:k + dec_seq].reshape(n_s, d) for k in range(CONV_WIDTH)]
    xs = conv_out_sample(xs, bs, shifted[0], shifted[1], shifted[2], conv_w, w_out)
    conv_state_sample = u_ext[:, dec_seq:]

    xp = peer(xp, 0, PROMPT_TM)
    xs = peer_sample(xs, 0)

    cos_p, sin_p = _rope_tables(jnp.arange(seq))
    qkv_p = qkv_project(xp, g_mix[1], w_qkv, attn_q_norm, attn_k_norm, cos_p, sin_p,
                        tm=PROMPT_TM, rope_tiles=seq // PROMPT_TM)
    outs, lses = zip(*[attn_prompt_group(qkv_p, g, batch=batch, seq_len=seq) for g in range(N_GROUPS)])
    xp = attn_out(xp, jnp.stack(outs), jnp.stack(lses), w_o, tm=ATTN_OUT_TM)

    cos_s, sin_s = _rope_tables(jnp.tile(PAST_LEN + jnp.arange(dec_seq), dec_batch))
    qkv_s = qkv_project(xs, g_mix[1], w_qkv, attn_q_norm, attn_k_norm, cos_s, sin_s, tm=n_s, rope_tiles=1)
    o_s, lse_s = attn_sample(qkv_s, caches, batch=dec_batch, t_new=dec_seq)
    xs = attn_out(xs, o_s, lse_s, w_o, tm=n_s)

    xp = peer(xp, 1, PROMPT_TM)
    xs = peer_sample(xs, 1)

    qkv_p5 = qkv_p.reshape(batch, seq, N_GROUPS, 3, n_heads, HEAD_DIM)
    qkv_s5 = qkv_s.reshape(dec_batch, dec_seq, N_GROUPS, 3, n_heads, HEAD_DIM)
    kv_prompt = [qkv_p5[:, seq - min(w, seq):, g, 1:3] for g, w in enumerate(WINDOWS)]
    kv_sample = [jnp.concatenate([caches[g][:, dec_seq:], qkv_s5[:, :, g, 1:3]], axis=1) for g in range(N_GROUPS)]

    return (xp.reshape(batch, seq, d), xs.reshape(dec_batch, dec_seq, d),
            conv_state_prompt, conv_state_sample, *kv_prompt, *kv_sample)
```

```python
import functools
import math

import jax
import jax.numpy as jnp
import numpy as np
from jax import lax
from jax.experimental import pallas as pl
from jax.experimental.pallas import tpu as pltpu

F32 = jnp.float32
BF16 = jnp.bfloat16

V7X_LANES = 128
V7X_SUBLANES = 8
V7X_VMEM_BYTES = 64 * 1024 * 1024
VMEM_LIMIT_BYTES = V7X_VMEM_BYTES - 8 * 1024 * 1024

NORM_EPS = 1e-6
CONV_WIDTH = 3
WINDOWS = (128, 512, 2048)
DILATIONS = (1, 4, 16)
N_GROUPS = 3
HEAD_DIM = 128
ATTN_BLOCK = 128
ATTN_SCALE = HEAD_DIM ** -0.5
ROPE_THETA = 10000.0
PAST_LEN = 16384
assert PAST_LEN >= max(WINDOWS)
PEER_HEADS = 8
PEER_N_KEYS = 128
PEER_TOPK = 16
PEER_D_HALF = 128
SQRT_HALF = math.sqrt(0.5)
NOT_SELECTED = 64.0
NEG_INF = float("-inf")

NT_DIMS = (((1,), (1,)), ((), ()))
TN_DIMS = (((0,), (0,)), ((), ()))


def _params(*semantics):
    return pltpu.CompilerParams(dimension_semantics=semantics, vmem_limit_bytes=VMEM_LIMIT_BYTES)


def _rms_norm(x, g):
    return x * lax.rsqrt(jnp.mean(x * x, axis=-1, keepdims=True) + NORM_EPS) * g


def _split_dot(a, b01):
    hi = a.astype(BF16)
    lo = (a - hi.astype(F32)).astype(BF16)
    return (jnp.dot(hi, b01, preferred_element_type=F32)
            + jnp.dot(lo, b01, preferred_element_type=F32))


def _conv_in_kernel(x_ref, g_ref, wb_ref, wc_ref, wh_ref, b_ref, u_ref, xn_ref):
    @pl.when(pl.program_id(1) == 0)
    def _():
        xn_ref[...] = _rms_norm(x_ref[...], g_ref[...]).astype(BF16)

    xn = xn_ref[...]
    b_ref[...] = jnp.dot(xn, wb_ref[...], preferred_element_type=F32)
    c = jnp.dot(xn, wc_ref[...], preferred_element_type=F32)
    h = jnp.dot(xn, wh_ref[...], preferred_element_type=F32)
    u_ref[...] = c * h


def conv_in(x, g, w_in, *, tm, tn):
    n, d = x.shape
    nj = d // tn
    return pl.pallas_call(
        _conv_in_kernel,
        grid=(n // tm, nj),
        in_specs=[
            pl.BlockSpec((tm, d), lambda i, j: (i, 0)),
            pl.BlockSpec((1, d), lambda i, j: (0, 0)),
            pl.BlockSpec((d, tn), lambda i, j: (0, j)),
            pl.BlockSpec((d, tn), lambda i, j: (0, j + nj)),
            pl.BlockSpec((d, tn), lambda i, j: (0, j + 2 * nj)),
        ],
        out_specs=[pl.BlockSpec((tm, tn), lambda i, j: (i, j)),
                   pl.BlockSpec((tm, tn), lambda i, j: (i, j))],
        out_shape=[jax.ShapeDtypeStruct((n, d), F32), jax.ShapeDtypeStruct((n, d), F32)],
        scratch_shapes=[pltpu.VMEM((tm, d), BF16)],
        compiler_params=_params("arbitrary", "arbitrary"),
        name="conv_in",
    )(x, g, w_in, w_in, w_in)


def _conv_mix(cw_ref, um2, um1, u0):
    return cw_ref[0:1, :] * um2 + cw_ref[1:2, :] * um1 + cw_ref[2:3, :] * u0


def _conv_out_prompt_kernel(x_ref, b_ref, u_ref, up_ref, st_ref, cw_ref, wo_ref, o_ref, *, tiles_per_seq):
    i = pl.program_id(0)
    u = u_ref[...]
    first = (i % tiles_per_seq) == 0
    prev = jnp.where(first, st_ref[...], up_ref[V7X_SUBLANES - 2:V7X_SUBLANES, :])
    row = lax.broadcasted_iota(jnp.int32, (u.shape[0], 1), 0)
    um1 = jnp.where(row == 0, prev[1:2, :], pltpu.roll(u, 1, axis=0))
    um2 = jnp.where(row == 0, prev[0:1, :], jnp.where(row == 1, prev[1:2, :], pltpu.roll(u, 2, axis=0)))
    a = (b_ref[...] * _conv_mix(cw_ref, um2, um1, u)).astype(BF16)
    o_ref[...] = x_ref[...] + jnp.dot(a, wo_ref[...], preferred_element_type=F32)


def conv_out_prompt(x, bgate, u, state, conv_w, w_out, *, seq_len, tm):
    n, d = x.shape
    tps = seq_len // tm
    sub_per_tile = tm // V7X_SUBLANES
    tile = pl.BlockSpec((tm, d), lambda i: (i, 0))
    return pl.pallas_call(
        functools.partial(_conv_out_prompt_kernel, tiles_per_seq=tps),
        grid=(n // tm,),
        in_specs=[
            tile, tile, tile,
            pl.BlockSpec((V7X_SUBLANES, d), lambda i: (jnp.maximum(i * sub_per_tile - 1, 0), 0)),
            pl.BlockSpec((None, CONV_WIDTH - 1, d), lambda i: (i // tps, 0, 0)),
            pl.BlockSpec((CONV_WIDTH, d), lambda i: (0, 0)),
            pl.BlockSpec((d, d), lambda i: (0, 0)),
        ],
        out_specs=tile,
        out_shape=jax.ShapeDtypeStruct((n, d), F32),
        compiler_params=_params("arbitrary"),
        name="conv_out_prompt",
    )(x, bgate, u, u, state, conv_w, w_out)


def _conv_out_sample_kernel(x_ref, b_ref, um2_ref, um1_ref, u_ref, cw_ref, wo_ref, o_ref):
    a = (b_ref[...] * _conv_mix(cw_ref, um2_ref[...], um1_ref[...], u_ref[...])).astype(BF16)
    o_ref[...] = x_ref[...] + jnp.dot(a, wo_ref[...], preferred_element_type=F32)


def conv_out_sample(x, bgate, um2, um1, u0, conv_w, w_out):
    n, d = x.shape
    full = pl.BlockSpec((n, d), lambda i: (0, 0))
    return pl.pallas_call(
        _conv_out_sample_kernel,
        grid=(1,),
        in_specs=[full, full, full, full, full,
                  pl.BlockSpec((CONV_WIDTH, d), lambda i: (0, 0)),
                  pl.BlockSpec((d, d), lambda i: (0, 0))],
        out_specs=full,
        out_shape=jax.ShapeDtypeStruct((n, d), F32),
        compiler_params=_params("arbitrary"),
        name="conv_out_sample",
    )(x, bgate, um2, um1, u0, conv_w, w_out)


CAND_ROWS = 72


def _cand_flat_index():
    pairs = []
    for b in range(4):
        pairs += [(a, b) for a in range(16 if b == 0 else 8)]
    pairs += [(0, b) if b >= 4 else None for b in range(16)]
    pairs += [(1, b) if b >= 4 else None for b in range(8)]
    pairs += [(2, b) if b == 4 else None for b in range(8)]
    rows = [a * PEER_TOPK + b if pair is not None and (a + 1) * (b + 1) <= PEER_TOPK else 256 + i
            for i, pair in enumerate(pairs) for a, b in [pair or (99, 99)]]
    assert len(rows) == CAND_ROWS and len({r for r in rows if r < 256}) == 50
    return jnp.broadcast_to(jnp.asarray(rows, F32).reshape(CAND_ROWS, 1), (CAND_ROWS, V7X_LANES))


def _top16(s, key_iota):
    work = s
    rank = jnp.full(s.shape, NOT_SELECTED, F32)
    val_row = lax.broadcasted_iota(jnp.int32, (PEER_TOPK, s.shape[1]), 0)
    vals = jnp.zeros((PEER_TOPK, s.shape[1]), F32)
    for a in range(PEER_TOPK):
        m = jnp.max(work, axis=0, keepdims=True)
        first = jnp.min(jnp.where(work == m, key_iota, float(PEER_N_KEYS)), axis=0, keepdims=True)
        sel = key_iota == first
        rank = jnp.where(sel, float(a), rank)
        work = jnp.where(sel, NEG_INF, work)
        vals = jnp.where(val_row == a, m, vals)
    return vals, rank


def _peer_select(s1, s2, fidx):
    lanes = s1.shape[1]
    key_iota = lax.broadcasted_iota(jnp.int32, (PEER_N_KEYS, lanes), 0).astype(F32)
    v1, rank1 = _top16(s1, key_iota)
    v2, rank2 = _top16(s2, key_iota)

    r16 = lax.broadcasted_iota(jnp.int32, (16, lanes), 0)
    r8 = lax.broadcasted_iota(jnp.int32, (8, lanes), 0)
    v1_lo, v2_lo = v1[0:8, :], v2[0:8, :]
    cand = jnp.concatenate([
        v1 + v2[0:1, :],
        v1_lo + v2[1:2, :],
        jnp.where(r8 < 5, v1_lo + v2[2:3, :], NEG_INF),
        jnp.where(r8 < 4, v1_lo + v2[3:4, :], NEG_INF),
        jnp.where(r16 >= 4, v1[0:1, :] + v2, NEG_INF),
        jnp.where(r8 >= 4, v1[1:2, :] + v2_lo, NEG_INF),
        jnp.where(r8 == 4, v1[2:3, :] + v2_lo, NEG_INF),
    ], axis=0)
    c_top = v1[0:1, :] + v2[0:1, :]
    taken = jnp.zeros(cand.shape, F32)
    z = jnp.zeros((1, lanes), F32)
    for _ in range(PEER_TOPK):
        m = jnp.max(cand, axis=0, keepdims=True)
        first = jnp.min(jnp.where(cand == m, fidx, 1024.0), axis=0, keepdims=True)
        sel = fidx == first
        taken = jnp.where(sel, 1.0, taken)
        cand = jnp.where(sel, NEG_INF, cand)
        z = z + jnp.exp(m - c_top)

    extra = jnp.where(r8 == 0, jnp.sum(taken[40:56, :], axis=0, keepdims=True),
                      jnp.where(r8 == 1, jnp.sum(taken[56:64, :], axis=0, keepdims=True),
                                jnp.where(r8 == 2, jnp.sum(taken[64:72, :], axis=0, keepdims=True), 0.0)))
    cnt_lo = taken[0:8, :] + taken[16:24, :] + taken[24:32, :] + taken[32:40, :] + extra
    cnt_vec = jnp.concatenate([cnt_lo, taken[8:16, :]], axis=0)
    cnt = jnp.zeros(s1.shape, F32)
    for a in range(PEER_TOPK):
        cnt = jnp.where(rank1 == float(a), cnt_vec[a:a + 1, :], cnt)

    e1n = jnp.exp(s1 - v1[0:1, :]) / z
    e2 = jnp.exp(s2 - v2[0:1, :])
    return rank2, e2, cnt, e1n


def _peer_query_kernel(x_ref, g_ref, wq_ref, keys_ref, fidx_ref,
                       xn_ref, rank2_ref, e2_ref, cnt_ref, e1n_ref, sc_ref):
    tm = x_ref.shape[0]
    xn = _rms_norm(x_ref[...], g_ref[...]).astype(BF16)
    xn_ref[...] = xn
    q = jnp.dot(xn, wq_ref[...], preferred_element_type=F32).astype(BF16)
    for rp in range(2 * PEER_HEADS):
        q_rp = q[:, rp * PEER_D_HALF:(rp + 1) * PEER_D_HALF]
        sc_ref[rp] = lax.dot_general(keys_ref[rp], q_rp, NT_DIMS, preferred_element_type=F32)

    fidx = fidx_ref[...]

    def head_body(r, carry):
        for s in range(tm // V7X_LANES):
            cols = slice(s * V7X_LANES, (s + 1) * V7X_LANES)
            rank2, e2, cnt, e1n = _peer_select(sc_ref[2 * r, :, cols], sc_ref[2 * r + 1, :, cols], fidx)
            rank2_ref[s, r] = rank2
            e2_ref[s, r] = e2
            cnt_ref[r, :, cols] = cnt
            e1n_ref[r, :, cols] = e1n
        return carry

    lax.fori_loop(0, PEER_HEADS, head_body, 0)


def peer_query(x, g, w_q, keys, *, tm):
    n, d = x.shape
    dq = w_q.shape[1]
    groups = tm // V7X_LANES
    by_i2 = jax.ShapeDtypeStruct((n // V7X_LANES, PEER_HEADS, PEER_N_KEYS, V7X_LANES), F32)
    by_i2_spec = pl.BlockSpec((groups, PEER_HEADS, PEER_N_KEYS, V7X_LANES), lambda i: (i, 0, 0, 0))
    sel = jax.ShapeDtypeStruct((PEER_HEADS, PEER_N_KEYS, n), F32)
    sel_spec = pl.BlockSpec((PEER_HEADS, PEER_N_KEYS, tm), lambda i: (0, 0, i))
    return pl.pallas_call(
        _peer_query_kernel,
        grid=(n // tm,),
        in_specs=[
            pl.BlockSpec((tm, d), lambda i: (i, 0)),
            pl.BlockSpec((1, d), lambda i: (0, 0)),
            pl.BlockSpec((d, dq), lambda i: (0, 0)),
            pl.BlockSpec((2 * PEER_HEADS, PEER_N_KEYS, PEER_D_HALF), lambda i: (0, 0, 0)),
            pl.BlockSpec((CAND_ROWS, V7X_LANES), lambda i: (0, 0)),
        ],
        out_specs=[pl.BlockSpec((tm, d), lambda i: (i, 0)), by_i2_spec, by_i2_spec, sel_spec, sel_spec],
        out_shape=[jax.ShapeDtypeStruct((n, d), BF16), by_i2, by_i2, sel, sel],
        scratch_shapes=[pltpu.VMEM((2 * PEER_HEADS, PEER_N_KEYS, tm), F32)],
        compiler_params=_params("arbitrary"),
        name="peer_query",
    )(x, g, w_q, keys, _cand_flat_index())


GATE_ROWS = 32


def _gate_slabs(slabs, h_ref, a_ref, rank2_ref, e2_ref, cnt_ref, e1n_ref):
    for k in slabs:
        for s in range(h_ref.shape[0]):
            cols = slice(s * V7X_LANES, (s + 1) * V7X_LANES)
            cnt = [cnt_ref[r, k:k + 1, cols] for r in range(PEER_HEADS)]
            e1n = [e1n_ref[r, k:k + 1, cols] for r in range(PEER_HEADS)]
            for i2 in range(0, PEER_N_KEYS, GATE_ROWS):
                gate = jnp.zeros((GATE_ROWS, V7X_LANES), F32)
                for r in range(PEER_HEADS):
                    live = rank2_ref[s, r, i2:i2 + GATE_ROWS, :] < cnt[r]
                    gate = gate + jnp.where(live, e2_ref[s, r, i2:i2 + GATE_ROWS, :], 0.0) * e1n[r]
                rows = slice(k * PEER_N_KEYS + i2, k * PEER_N_KEYS + i2 + GATE_ROWS)
                h = h_ref[s, rows, :]
                act = 0.5 * h * (1.0 + lax.erf(h * SQRT_HALF))
                a_ref[rows, cols] = (gate * act).astype(BF16)


def _peer_expert_kernel(xn_ref, x_ref, rank2_ref, e2_ref, cnt_ref, e1n_ref, u_ref, v_ref,
                        o_ref, h_ref, a_ref):
    c = pl.program_id(1)

    @pl.when(c == 0)
    def _():
        o_ref[...] = x_ref[...]

    h = lax.dot_general(u_ref[...], xn_ref[...], NT_DIMS, preferred_element_type=F32)
    for s in range(h_ref.shape[0]):
        h_ref[s] = h[:, s * V7X_LANES:(s + 1) * V7X_LANES]
    _gate_slabs(range(u_ref.shape[0] // PEER_N_KEYS), h_ref, a_ref, rank2_ref, e2_ref, cnt_ref, e1n_ref)
    o_ref[...] += lax.dot_general(a_ref[...], v_ref[...], TN_DIMS, preferred_element_type=F32)


def peer_experts(xn, x, rank2, e2, cnt, e1n, u, v, *, tm, te):
    n, d = x.shape
    tok = pl.BlockSpec((tm, d), lambda i, c: (i, 0))
    groups = tm // V7X_LANES
    by_i2 = pl.BlockSpec((groups, PEER_HEADS, PEER_N_KEYS, V7X_LANES), lambda i, c: (i, 0, 0, 0))
    by_i1 = pl.BlockSpec((PEER_HEADS, te // PEER_N_KEYS, tm), lambda i, c: (0, c, i))
    wts = pl.BlockSpec((te, d), lambda i, c: (c, 0))
    return pl.pallas_call(
        _peer_expert_kernel,
        grid=(n // tm, u.shape[0] // te),
        in_specs=[tok, tok, by_i2, by_i2, by_i1, by_i1, wts, wts],
        out_specs=tok,
        out_shape=jax.ShapeDtypeStruct((n, d), F32),
        scratch_shapes=[pltpu.VMEM((groups, te, V7X_LANES), F32), pltpu.VMEM((te, tm), BF16)],
        compiler_params=_params("arbitrary", "arbitrary"),
        name="peer_experts",
    )(xn, x, rank2, e2, cnt, e1n, u, v)


def peer_layer(x, g, w_q, keys, u, v, *, tm, te):
    xn, rank2, e2, cnt, e1n = peer_query(x, g, w_q, keys, tm=tm)
    return peer_experts(xn, x, rank2, e2, cnt, e1n, u, v, tm=tm, te=te)


def _residue_order(tm, dil):
    return np.arange(tm).reshape(tm // dil, dil).T.reshape(tm)


def _qkv_kernel(x_ref, g_ref, w_ref, qg_ref, kg_ref, cos_ref, sin_ref, o_ref, xs_ref, xp_ref, *, dil):
    tm, d = x_ref.shape
    hd = w_ref.shape[1] // 3
    per_res = tm // dil
    if dil > 1:
        for c in range(d // V7X_LANES):
            xs_ref[c] = x_ref[:, c * V7X_LANES:(c + 1) * V7X_LANES]
        for res in range(dil):
            for c in range(d // V7X_LANES):
                xp_ref[res * per_res:(res + 1) * per_res, c * V7X_LANES:(c + 1) * V7X_LANES] = (
                    xs_ref[c, pl.ds(res, per_res, stride=dil), :])
        x = xp_ref[...]
    else:
        x = x_ref[...]
    xn = _rms_norm(x, g_ref[...]).astype(BF16)
    cos, sin = cos_ref[...], sin_ref[...]

    def store(col0, val):
        for res in range(dil):
            o_ref[res, :, col0:col0 + val.shape[1]] = val[res * per_res:(res + 1) * per_res, :]

    for kind, gain_ref in enumerate((qg_ref, kg_ref, None)):
        y = jnp.dot(xn, w_ref[:, kind * hd:(kind + 1) * hd], preferred_element_type=F32)
        if gain_ref is None:
            store(kind * hd, y)
            continue
        gain = gain_ref[...]
        for h in range(hd // HEAD_DIM):
            yn = _rms_norm(y[:, h * HEAD_DIM:(h + 1) * HEAD_DIM], gain)
            store(kind * hd + h * HEAD_DIM, yn * cos + pltpu.roll(yn, HEAD_DIM // 2, axis=1) * sin)


def qkv_project(x, g, w_qkv, q_gain, k_gain, pos, group, *, dil, batch, tm):
    n, d = x.shape
    seq = n // batch
    hd3 = w_qkv.shape[1] // N_GROUPS
    tiles = seq // tm
    per_res = tm // dil
    assert per_res % V7X_SUBLANES == 0
    order = (np.arange(tiles)[:, None] * tm + _residue_order(tm, dil)[None, :]).reshape(seq)
    cos, sin = _rope_tables(jnp.asarray(np.asarray(pos)[order]))
    gain = pl.BlockSpec((None, 1, HEAD_DIM), lambda i: (group, 0, 0))
    rope = pl.BlockSpec((tm, HEAD_DIM), lambda i: (i % tiles, 0))
    return pl.pallas_call(
        functools.partial(_qkv_kernel, dil=dil),
        grid=(n // tm,),
        in_specs=[
            pl.BlockSpec((tm, d), lambda i: (i, 0)),
            pl.BlockSpec((1, d), lambda i: (0, 0)),
            pl.BlockSpec((d, hd3), lambda i: (0, group), pipeline_mode=pl.Buffered(1)),
            gain, gain, rope, rope,
        ],
        out_specs=pl.BlockSpec((None, dil, per_res, hd3), lambda i: (i // tiles, 0, i % tiles, 0)),
        out_shape=jax.ShapeDtypeStruct((batch, dil, seq // dil, hd3), F32),
        scratch_shapes=[pltpu.VMEM((d // V7X_LANES, tm, V7X_LANES), F32), pltpu.VMEM((tm, d), F32)],
        compiler_params=_params("arbitrary"),
        name=f"qkv_project_g{group}_d{dil}",
    )(x, g, w_qkv, q_gain.reshape(N_GROUPS, 1, HEAD_DIM), k_gain.reshape(N_GROUPS, 1, HEAD_DIM), cos, sin)


def _spread_heads(cols_list, width):
    rows = cols_list[0].shape[0]
    lane = lax.broadcasted_iota(jnp.int32, (rows, width), 1)
    out = jnp.zeros((rows, width), F32)
    for h, col in enumerate(cols_list):
        out = jnp.where(lane == h, col, out)
    return out


def _attn_prompt_kernel(q_ref, kc_ref, kp_ref, vc_ref, vp_ref, o_ref, lse_ref, *, window_blocks):
    i = pl.program_id(2)
    qb = q_ref.shape[0]
    qi = lax.broadcasted_iota(jnp.int32, (qb, 2 * qb), 0)
    kj = lax.broadcasted_iota(jnp.int32, (qb, 2 * qb), 1)
    live = (kj <= qi + qb) & (kj >= qi + qb - window_blocks) & ((i > 0) | (kj >= qb))
    lses = []
    for h in range(q_ref.shape[1] // HEAD_DIM):
        cols = slice(h * HEAD_DIM, (h + 1) * HEAD_DIM)
        q = q_ref[:, cols].astype(BF16)
        k = jnp.concatenate([kp_ref[:, cols], kc_ref[:, cols]], axis=0).astype(BF16)
        v = jnp.concatenate([vp_ref[:, cols], vc_ref[:, cols]], axis=0).astype(BF16)
        s = lax.dot_general(q, k, NT_DIMS, preferred_element_type=F32) * ATTN_SCALE
        s = jnp.where(live, s, NEG_INF)
        m = jnp.max(s, axis=-1, keepdims=True)
        p = jnp.exp(s - m)
        l = jnp.sum(p, axis=-1, keepdims=True)
        o_ref[:, cols] = jnp.dot(p.astype(BF16), v, preferred_element_type=F32) / l
        lses.append(m + jnp.log(l))
    lse_ref[...] = _spread_heads(lses, lse_ref.shape[1])


def attn_prompt_group(qkv, group):
    batch, dil, n_strided, hd3 = qkv.shape
    hd = hd3 // 3
    win = WINDOWS[group]
    assert dil == DILATIONS[group] and n_strided % ATTN_BLOCK == 0 and win // dil <= ATTN_BLOCK
    blk = (None, None, ATTN_BLOCK, hd)

    def cur(kind):
        return pl.BlockSpec(blk, lambda b, r, i: (b, r, i, kind))

    def prev(kind):
        return pl.BlockSpec(blk, lambda b, r, i: (b, r, jnp.maximum(i - 1, 0), kind))

    return pl.pallas_call(
        functools.partial(_attn_prompt_kernel, window_blocks=win // dil),
        grid=(batch, dil, n_strided // ATTN_BLOCK),
        in_specs=[cur(0), cur(1), prev(1), cur(2), prev(2)],
        out_specs=[pl.BlockSpec(blk, lambda b, r, i: (b, r, i, 0)),
                   pl.BlockSpec((None, None, ATTN_BLOCK, V7X_LANES), lambda b, r, i: (b, r, i, 0))],
        out_shape=[jax.ShapeDtypeStruct((batch, dil, n_strided, hd), F32),
                   jax.ShapeDtypeStruct((batch, dil, n_strided, V7X_LANES), F32)],
        compiler_params=_params("arbitrary", "arbitrary", "arbitrary"),
        name=f"attn_prompt_g{group}",
    )(qkv, qkv, qkv, qkv, qkv)


def _head_segments(n_heads, transpose):
    shape = (V7X_LANES, n_heads * HEAD_DIM) if transpose else (n_heads * HEAD_DIM, V7X_LANES)
    c = lax.broadcasted_iota(jnp.int32, shape, 1 if transpose else 0)
    h = lax.broadcasted_iota(jnp.int32, shape, 0 if transpose else 1)
    return jnp.where((c >= h * HEAD_DIM) & (c < (h + 1) * HEAD_DIM), 1.0, 0.0).astype(BF16)


def _attn_sample_kernel(q0_ref, q1_ref, q2_ref, c0_ref, c1_ref, c2_ref, o_ref, lse_ref):
    t = pl.program_id(1)
    hd = o_ref.shape[2]
    n_heads = hd // HEAD_DIM
    t_new = q0_ref.shape[0]
    seg = _head_segments(n_heads, transpose=False)
    seg_t = _head_segments(n_heads, transpose=True)
    row_c = lax.broadcasted_iota(jnp.int32, (c0_ref.shape[0], V7X_LANES), 0)
    row_n = lax.broadcasted_iota(jnp.int32, (t_new, V7X_LANES), 0)
    for g, (qkv_ref, cache_ref) in enumerate(zip((q0_ref, q1_ref, q2_ref), (c0_ref, c1_ref, c2_ref))):
        q = qkv_ref[pl.ds(t, 1), 0:hd]
        k_new = qkv_ref[:, hd:2 * hd]
        v_new = qkv_ref[:, 2 * hd:3 * hd]
        k_old, v_old = cache_ref[:, 0:hd], cache_ref[:, hd:2 * hd]
        s_old = _split_dot(k_old * q, seg) * ATTN_SCALE
        s_new = _split_dot(k_new * q, seg) * ATTN_SCALE
        if DILATIONS[g] == 1:
            s_old = jnp.where(row_c >= t, s_old, NEG_INF)
            s_new = jnp.where(row_n <= t, s_new, NEG_INF)
        else:
            s_new = jnp.where(row_n == t, s_new, NEG_INF)
        m = jnp.maximum(jnp.max(s_old, axis=0, keepdims=True), jnp.max(s_new, axis=0, keepdims=True))
        p_old, p_new = jnp.exp(s_old - m), jnp.exp(s_new - m)
        l = jnp.sum(p_old, axis=0, keepdims=True) + jnp.sum(p_new, axis=0, keepdims=True)
        w_old = _split_dot(p_old / l, seg_t)
        w_new = _split_dot(p_new / l, seg_t)
        o = jnp.sum(w_old * v_old, axis=0, keepdims=True) + jnp.sum(w_new * v_new, axis=0, keepdims=True)
        o_ref[g, pl.ds(t, 1), :] = o
        lse_ref[g, pl.ds(t, 1), :] = m + jnp.log(l)


def attn_sample(qkv, caches, *, batch, t_new):
    cols = qkv[0].shape[1]
    hd = cols // 3
    views, specs = [], []
    for g, cache in enumerate(caches):
        dil, keep = DILATIONS[g], cache.shape[1]
        assert keep == WINDOWS[g] and keep // dil == ATTN_BLOCK and (dil == 1 or t_new <= dil)
        views.append(cache.reshape(batch, keep // dil, dil * 2 * hd))
        if dil == 1:
            specs.append(pl.BlockSpec((None, ATTN_BLOCK, 2 * hd), lambda b, t: (b, 0, 0)))
        else:
            specs.append(pl.BlockSpec((None, ATTN_BLOCK, 2 * hd), lambda b, t: (b, 0, t)))
    rows_new = -(-t_new // V7X_SUBLANES) * V7X_SUBLANES
    qkv_rows = [jnp.pad(a.reshape(batch, t_new, cols), ((0, 0), (0, rows_new - t_new), (0, 0))) for a in qkv]
    o, lse = pl.pallas_call(
        _attn_sample_kernel,
        grid=(batch, t_new),
        in_specs=[pl.BlockSpec((None, rows_new, cols), lambda b, t: (b, 0, 0))] * N_GROUPS + specs,
        out_specs=[pl.BlockSpec((N_GROUPS, None, t_new, hd), lambda b, t: (0, b, 0, 0)),
                   pl.BlockSpec((N_GROUPS, None, t_new, V7X_LANES), lambda b, t: (0, b, 0, 0))],
        out_shape=[jax.ShapeDtypeStruct((N_GROUPS, batch, t_new, hd), F32),
                   jax.ShapeDtypeStruct((N_GROUPS, batch, t_new, V7X_LANES), F32)],
        compiler_params=_params("arbitrary", "arbitrary"),
        name="attn_sample",
    )(*qkv_rows, *views)
    return o.reshape(N_GROUPS, batch * t_new, hd), lse.reshape(N_GROUPS, batch * t_new, V7X_LANES)


def _attn_out_kernel(x_ref, o0_ref, o1_ref, o2_ref, l0_ref, l1_ref, l2_ref, wo_ref, y_ref,
                     og_ref, lg_ref, mix_ref):
    tm = x_ref.shape[0]
    hd = wo_ref.shape[0]
    heads = hd // HEAD_DIM
    for g, (o_ref, l_ref) in enumerate(zip((o0_ref, o1_ref, o2_ref), (l0_ref, l1_ref, l2_ref))):
        dil = o_ref.shape[0]
        for res in range(dil):
            rows = pl.ds(res, tm // dil, stride=dil) if dil > 1 else slice(0, tm)
            lg_ref[g, rows, :] = l_ref[res]
            for h in range(heads):
                og_ref[g, h, rows, :] = o_ref[res, :, h * HEAD_DIM:(h + 1) * HEAD_DIM]
    lse = [lg_ref[g] for g in range(N_GROUPS)]
    m = functools.reduce(jnp.maximum, lse)
    e = [jnp.exp(l - m) for l in lse]
    tot = functools.reduce(jnp.add, e)
    seg_t = _head_segments(heads, transpose=True)
    w = [_split_dot(e[g] / tot, seg_t) for g in range(N_GROUPS)]
    for h in range(heads):
        cols = slice(h * HEAD_DIM, (h + 1) * HEAD_DIM)
        mix = functools.reduce(jnp.add, [w[g][:, cols] * og_ref[g, h] for g in range(N_GROUPS)])
        mix_ref[:, cols] = mix.astype(BF16)
    y_ref[...] = x_ref[...] + jnp.dot(mix_ref[...], wo_ref[...], preferred_element_type=F32)


def attn_out(x, outs, lses, w_o, *, tm):
    n, d = x.shape
    batch = outs[0].shape[0]
    hd = outs[0].shape[3]
    tiles = n // batch // tm

    def by_residue(a):
        dil = a.shape[1]
        return pl.BlockSpec((None, dil, tm // dil, a.shape[3]), lambda i: (i // tiles, 0, i % tiles, 0))

    return pl.pallas_call(
        _attn_out_kernel,
        grid=(n // tm,),
        in_specs=[pl.BlockSpec((tm, d), lambda i: (i, 0))] + [by_residue(a) for a in (*outs, *lses)]
        + [pl.BlockSpec((hd, d), lambda i: (0, 0))],
        out_specs=pl.BlockSpec((tm, d), lambda i: (i, 0)),
        out_shape=jax.ShapeDtypeStruct((n, d), F32),
        scratch_shapes=[pltpu.VMEM((N_GROUPS, hd // HEAD_DIM, tm, HEAD_DIM), F32),
                        pltpu.VMEM((N_GROUPS, tm, V7X_LANES), F32), pltpu.VMEM((tm, hd), BF16)],
        compiler_params=_params("arbitrary"),
        name="attn_out",
    )(x, *outs, *lses, w_o)


def _rope_tables(pos):
    half = HEAD_DIM // 2
    inv_freq = ROPE_THETA ** (-jnp.arange(half, dtype=F32) / half)
    ang = pos.astype(F32)[:, None] * inv_freq
    cos, sin = jnp.cos(ang), jnp.sin(ang)
    return jnp.concatenate([cos, cos], axis=-1), jnp.concatenate([-sin, sin], axis=-1)


def _pad_rows(a, rows):
    return jnp.pad(a, ((0, rows - a.shape[0]),) + ((0, 0),) * (a.ndim - 1))


PROMPT_TM = 512
ATTN_OUT_TM = 256
QKV_TM = 256
PEER_TE = 1024
CONV_TN = 512
SAMPLE_PEER_ROWS = V7X_LANES


def kernel(x_prompt, x_sample, state_conv, cache_kv_w128, cache_kv_w512, cache_kv_w2048,
           norm_mix, norm_ffn, conv_w_in, conv_w, conv_w_out,
           attn_w_qkv, attn_q_norm, attn_k_norm, attn_w_o,
           peer_w_q, peer_sub_keys, peer_u, peer_v):
    batch, seq, d = x_prompt.shape
    dec_batch, dec_seq, _ = x_sample.shape
    n_p, n_s = batch * seq, dec_batch * dec_seq
    caches = (cache_kv_w128, cache_kv_w512, cache_kv_w2048)
    n_heads = caches[0].shape[3]
    hd = n_heads * HEAD_DIM

    w_in, w_out = conv_w_in.astype(BF16), conv_w_out.astype(BF16)
    w_qkv, w_o = attn_w_qkv.astype(BF16), attn_w_o.astype(BF16)
    w_q = peer_w_q.astype(BF16)
    keys = peer_sub_keys.astype(BF16).reshape(peer_sub_keys.shape[0], 2 * PEER_HEADS, PEER_N_KEYS, PEER_D_HALF)
    pu, pv = peer_u.astype(BF16), peer_v.astype(BF16)
    g_mix = norm_mix.reshape(-1, 1, d)
    g_ffn = norm_ffn.reshape(-1, 1, d)

    xp = x_prompt.reshape(n_p, d)
    xs = x_sample.reshape(n_s, d)

    def peer(x, layer, tm):
        return peer_layer(x, g_ffn[layer], w_q[layer], keys[layer], pu[layer], pv[layer], tm=tm, te=PEER_TE)

    def peer_sample(x, layer):
        return peer(_pad_rows(x, SAMPLE_PEER_ROWS), layer, SAMPLE_PEER_ROWS)[:n_s]

    bp, up = conv_in(xp, g_mix[0], w_in, tm=PROMPT_TM, tn=CONV_TN)
    xp = conv_out_prompt(xp, bp, up, jnp.zeros((batch, CONV_WIDTH - 1, d), F32), conv_w, w_out,
                         seq_len=seq, tm=PROMPT_TM)
    conv_state_prompt = up.reshape(batch, seq, d)[:, seq - (CONV_WIDTH - 1):]

    bs, us = conv_in(xs, g_mix[0], w_in, tm=n_s, tn=CONV_TN)
    u_ext = jnp.concatenate([state_conv, us.reshape(dec_batch, dec_seq, d)], axis=1)
    shifted = [u_ext[:, k:k + dec_seq].reshape(n_s, d) for k in range(CONV_WIDTH)]
    xs = conv_out_sample(xs, bs, shifted[0], shifted[1], shifted[2], conv_w, w_out)
    conv_state_sample = u_ext[:, dec_seq:]

    xp = peer(xp, 0, PROMPT_TM)
    xs = peer_sample(xs, 0)

    qkv_p = [qkv_project(xp, g_mix[1], w_qkv, attn_q_norm, attn_k_norm, np.arange(seq), g,
                         dil=DILATIONS[g], batch=batch, tm=QKV_TM) for g in range(N_GROUPS)]
    outs, lses = zip(*[attn_prompt_group(qkv_p[g], g) for g in range(N_GROUPS)])
    xp = attn_out(xp, outs, lses, w_o, tm=ATTN_OUT_TM)

    pos_s = np.tile(PAST_LEN + np.arange(dec_seq), dec_batch)
    qkv_s = [qkv_project(xs, g_mix[1], w_qkv, attn_q_norm, attn_k_norm, pos_s, g, dil=1, batch=1, tm=n_s)
             .reshape(n_s, 3 * hd) for g in range(N_GROUPS)]
    o_s, lse_s = attn_sample(qkv_s, caches, batch=dec_batch, t_new=dec_seq)
    xs = attn_out(xs, [o_s[g].reshape(1, 1, n_s, hd) for g in range(N_GROUPS)],
                  [lse_s[g].reshape(1, 1, n_s, V7X_LANES) for g in range(N_GROUPS)], w_o, tm=n_s)

    xp = peer(xp, 1, PROMPT_TM)
    xs = peer_sample(xs, 1)

    kv_prompt = []
    for g, (w, dil) in enumerate(zip(WINDOWS, DILATIONS)):
        keep = min(w, seq)
        tail = qkv_p[g][:, :, (seq - keep) // dil:, hd:]
        kv_prompt.append(tail.transpose(0, 2, 1, 3).reshape(batch, keep, 2, n_heads, HEAD_DIM))
    kv_sample = [jnp.concatenate([caches[g][:, dec_seq:],
                                  qkv_s[g][:, hd:].reshape(dec_batch, dec_seq, 2, n_heads, HEAD_DIM)], axis=1)
                 for g in range(N_GROUPS)]

    return (xp.reshape(batch, seq, d), xs.reshape(dec_batch, dec_seq, d),
            conv_state_prompt, conv_state_sample, *kv_prompt, *kv_sample)
```

```python
import functools
import math

import jax
import jax.numpy as jnp
import numpy as np
from jax import lax
from jax.experimental import pallas as pl
from jax.experimental.pallas import tpu as pltpu

F32 = jnp.float32
BF16 = jnp.bfloat16

V7X_LANES = 128
V7X_SUBLANES = 8
V7X_VMEM_BYTES = 64 * 1024 * 1024
VMEM_LIMIT_BYTES = V7X_VMEM_BYTES - 8 * 1024 * 1024

NORM_EPS = 1e-6
CONV_WIDTH = 3
WINDOWS = (128, 512, 2048)
DILATIONS = (1, 4, 16)
N_GROUPS = 3
HEAD_DIM = 128
ATTN_BLOCK = 128
ATTN_SCALE = HEAD_DIM ** -0.5
ROPE_THETA = 10000.0
PAST_LEN = 16384
assert PAST_LEN >= max(WINDOWS)
PEER_HEADS = 8
PEER_N_KEYS = 128
PEER_TOPK = 16
PEER_D_HALF = 128
SQRT_HALF = math.sqrt(0.5)
RANK_MASK = 31
NOT_SELECTED = float(RANK_MASK)
NEG_INF = float("-inf")

NT_DIMS = (((1,), (1,)), ((), ()))
TN_DIMS = (((0,), (0,)), ((), ()))


def _params(*semantics):
    return pltpu.CompilerParams(dimension_semantics=semantics, vmem_limit_bytes=VMEM_LIMIT_BYTES)


def _rms_norm(x, g):
    return x * lax.rsqrt(jnp.mean(x * x, axis=-1, keepdims=True) + NORM_EPS) * g


def _split_dot(a, b01):
    hi = a.astype(BF16)
    lo = (a - hi.astype(F32)).astype(BF16)
    return (jnp.dot(hi, b01, preferred_element_type=F32)
            + jnp.dot(lo, b01, preferred_element_type=F32))


def _conv_in_kernel(x_ref, g_ref, wb_ref, wc_ref, wh_ref, b_ref, u_ref, xn_ref):
    @pl.when(pl.program_id(1) == 0)
    def _():
        xn_ref[...] = _rms_norm(x_ref[...], g_ref[...]).astype(BF16)

    xn = xn_ref[...]
    b_ref[...] = jnp.dot(xn, wb_ref[...], preferred_element_type=F32)
    c = jnp.dot(xn, wc_ref[...], preferred_element_type=F32)
    h = jnp.dot(xn, wh_ref[...], preferred_element_type=F32)
    u_ref[...] = c * h


def conv_in(x, g, w_in, *, tm, tn):
    n, d = x.shape
    nj = d // tn
    return pl.pallas_call(
        _conv_in_kernel,
        grid=(n // tm, nj),
        in_specs=[
            pl.BlockSpec((tm, d), lambda i, j: (i, 0)),
            pl.BlockSpec((1, d), lambda i, j: (0, 0)),
            pl.BlockSpec((d, tn), lambda i, j: (0, j)),
            pl.BlockSpec((d, tn), lambda i, j: (0, j + nj)),
            pl.BlockSpec((d, tn), lambda i, j: (0, j + 2 * nj)),
        ],
        out_specs=[pl.BlockSpec((tm, tn), lambda i, j: (i, j)),
                   pl.BlockSpec((tm, tn), lambda i, j: (i, j))],
        out_shape=[jax.ShapeDtypeStruct((n, d), F32), jax.ShapeDtypeStruct((n, d), F32)],
        scratch_shapes=[pltpu.VMEM((tm, d), BF16)],
        compiler_params=_params("arbitrary", "arbitrary"),
        name="conv_in",
    )(x, g, w_in, w_in, w_in)


def _conv_mix(cw_ref, um2, um1, u0):
    return cw_ref[0:1, :] * um2 + cw_ref[1:2, :] * um1 + cw_ref[2:3, :] * u0


def _conv_out_prompt_kernel(x_ref, b_ref, u_ref, up_ref, st_ref, cw_ref, wo_ref, o_ref, *, tiles_per_seq):
    i = pl.program_id(0)
    u = u_ref[...]
    first = (i % tiles_per_seq) == 0
    prev = jnp.where(first, st_ref[...], up_ref[V7X_SUBLANES - 2:V7X_SUBLANES, :])
    row = lax.broadcasted_iota(jnp.int32, (u.shape[0], 1), 0)
    um1 = jnp.where(row == 0, prev[1:2, :], pltpu.roll(u, 1, axis=0))
    um2 = jnp.where(row == 0, prev[0:1, :], jnp.where(row == 1, prev[1:2, :], pltpu.roll(u, 2, axis=0)))
    a = (b_ref[...] * _conv_mix(cw_ref, um2, um1, u)).astype(BF16)
    o_ref[...] = x_ref[...] + jnp.dot(a, wo_ref[...], preferred_element_type=F32)


def conv_out_prompt(x, bgate, u, state, conv_w, w_out, *, seq_len, tm):
    n, d = x.shape
    tps = seq_len // tm
    sub_per_tile = tm // V7X_SUBLANES
    tile = pl.BlockSpec((tm, d), lambda i: (i, 0))
    return pl.pallas_call(
        functools.partial(_conv_out_prompt_kernel, tiles_per_seq=tps),
        grid=(n // tm,),
        in_specs=[
            tile, tile, tile,
            pl.BlockSpec((V7X_SUBLANES, d), lambda i: (jnp.maximum(i * sub_per_tile - 1, 0), 0)),
            pl.BlockSpec((None, CONV_WIDTH - 1, d), lambda i: (i // tps, 0, 0)),
            pl.BlockSpec((CONV_WIDTH, d), lambda i: (0, 0)),
            pl.BlockSpec((d, d), lambda i: (0, 0)),
        ],
        out_specs=tile,
        out_shape=jax.ShapeDtypeStruct((n, d), F32),
        compiler_params=_params("arbitrary"),
        name="conv_out_prompt",
    )(x, bgate, u, u, state, conv_w, w_out)


def _conv_out_sample_kernel(x_ref, b_ref, um2_ref, um1_ref, u_ref, cw_ref, wo_ref, o_ref):
    a = (b_ref[...] * _conv_mix(cw_ref, um2_ref[...], um1_ref[...], u_ref[...])).astype(BF16)
    o_ref[...] = x_ref[...] + jnp.dot(a, wo_ref[...], preferred_element_type=F32)


def conv_out_sample(x, bgate, um2, um1, u0, conv_w, w_out):
    n, d = x.shape
    full = pl.BlockSpec((n, d), lambda i: (0, 0))
    return pl.pallas_call(
        _conv_out_sample_kernel,
        grid=(1,),
        in_specs=[full, full, full, full, full,
                  pl.BlockSpec((CONV_WIDTH, d), lambda i: (0, 0)),
                  pl.BlockSpec((d, d), lambda i: (0, 0))],
        out_specs=full,
        out_shape=jax.ShapeDtypeStruct((n, d), F32),
        compiler_params=_params("arbitrary"),
        name="conv_out_sample",
    )(x, bgate, um2, um1, u0, conv_w, w_out)


CAND_ROWS = 72


def _cand_flat_index():
    pairs = []
    for b in range(4):
        pairs += [(a, b) for a in range(16 if b == 0 else 8)]
    pairs += [(0, b) if b >= 4 else None for b in range(16)]
    pairs += [(1, b) if b >= 4 else None for b in range(8)]
    pairs += [(2, b) if b == 4 else None for b in range(8)]
    rows = [a * PEER_TOPK + b if pair is not None and (a + 1) * (b + 1) <= PEER_TOPK else 256 + i
            for i, pair in enumerate(pairs) for a, b in [pair or (99, 99)]]
    assert len(rows) == CAND_ROWS and len({r for r in rows if r < 256}) == 50
    return jnp.broadcast_to(jnp.asarray(rows, F32).reshape(CAND_ROWS, 1), (CAND_ROWS, V7X_LANES))


def _top16(s, key_iota):
    work = s
    rank = jnp.full(s.shape, NOT_SELECTED, F32)
    val_row = lax.broadcasted_iota(jnp.int32, (PEER_TOPK, s.shape[1]), 0)
    vals = jnp.zeros((PEER_TOPK, s.shape[1]), F32)
    for a in range(PEER_TOPK):
        m = jnp.max(work, axis=0, keepdims=True)
        first = jnp.min(jnp.where(work == m, key_iota, float(PEER_N_KEYS)), axis=0, keepdims=True)
        sel = key_iota == first
        rank = jnp.where(sel, float(a), rank)
        work = jnp.where(sel, NEG_INF, work)
        vals = jnp.where(val_row == a, m, vals)
    return vals, rank


def _peer_select(s1, s2, fidx):
    lanes = s1.shape[1]
    key_iota = lax.broadcasted_iota(jnp.int32, (PEER_N_KEYS, lanes), 0).astype(F32)
    v1, rank1 = _top16(s1, key_iota)
    v2, rank2 = _top16(s2, key_iota)

    r16 = lax.broadcasted_iota(jnp.int32, (16, lanes), 0)
    r8 = lax.broadcasted_iota(jnp.int32, (8, lanes), 0)
    v1_lo, v2_lo = v1[0:8, :], v2[0:8, :]
    cand = jnp.concatenate([
        v1 + v2[0:1, :],
        v1_lo + v2[1:2, :],
        jnp.where(r8 < 5, v1_lo + v2[2:3, :], NEG_INF),
        jnp.where(r8 < 4, v1_lo + v2[3:4, :], NEG_INF),
        jnp.where(r16 >= 4, v1[0:1, :] + v2, NEG_INF),
        jnp.where(r8 >= 4, v1[1:2, :] + v2_lo, NEG_INF),
        jnp.where(r8 == 4, v1[2:3, :] + v2_lo, NEG_INF),
    ], axis=0)
    c_top = v1[0:1, :] + v2[0:1, :]
    taken = jnp.zeros(cand.shape, F32)
    z = jnp.zeros((1, lanes), F32)
    for _ in range(PEER_TOPK):
        m = jnp.max(cand, axis=0, keepdims=True)
        first = jnp.min(jnp.where(cand == m, fidx, 1024.0), axis=0, keepdims=True)
        sel = fidx == first
        taken = jnp.where(sel, 1.0, taken)
        cand = jnp.where(sel, NEG_INF, cand)
        z = z + jnp.exp(m - c_top)

    extra = jnp.where(r8 == 0, jnp.sum(taken[40:56, :], axis=0, keepdims=True),
                      jnp.where(r8 == 1, jnp.sum(taken[56:64, :], axis=0, keepdims=True),
                                jnp.where(r8 == 2, jnp.sum(taken[64:72, :], axis=0, keepdims=True), 0.0)))
    cnt_lo = taken[0:8, :] + taken[16:24, :] + taken[24:32, :] + taken[32:40, :] + extra
    cnt_vec = jnp.concatenate([cnt_lo, taken[8:16, :]], axis=0)
    cnt = jnp.zeros(s1.shape, F32)
    for a in range(PEER_TOPK):
        cnt = jnp.where(rank1 == float(a), cnt_vec[a:a + 1, :], cnt)

    e1n = jnp.exp(s1 - v1[0:1, :]) / z
    e2 = jnp.exp(s2 - v2[0:1, :])
    return rank2, e2, cnt, e1n


def _peer_query_kernel(x_ref, g_ref, wq_ref, keys_ref, fidx_ref,
                       xn_ref, e2r_ref, cnt_ref, e1n_ref, sc_ref):
    tm = x_ref.shape[0]
    xn = _rms_norm(x_ref[...], g_ref[...]).astype(BF16)
    xn_ref[...] = xn
    q = jnp.dot(xn, wq_ref[...], preferred_element_type=F32).astype(BF16)
    for rp in range(2 * PEER_HEADS):
        q_rp = q[:, rp * PEER_D_HALF:(rp + 1) * PEER_D_HALF]
        sc_ref[rp] = lax.dot_general(keys_ref[rp], q_rp, NT_DIMS, preferred_element_type=F32)

    fidx = fidx_ref[...]

    def head_body(r, carry):
        for s in range(tm // V7X_LANES):
            cols = slice(s * V7X_LANES, (s + 1) * V7X_LANES)
            rank2, e2, cnt, e1n = _peer_select(sc_ref[2 * r, :, cols], sc_ref[2 * r + 1, :, cols], fidx)
            e2_bits = pltpu.bitcast(e2, jnp.int32) & ~RANK_MASK
            e2r_ref[s, r] = pltpu.bitcast(e2_bits | rank2.astype(jnp.int32), F32)
            cnt_ref[r, :, cols] = cnt
            e1n_ref[r, :, cols] = e1n
        return carry

    lax.fori_loop(0, PEER_HEADS, head_body, 0)


def peer_query(x, g, w_q, keys, *, tm):
    n, d = x.shape
    dq = w_q.shape[1]
    groups = tm // V7X_LANES
    by_i2 = jax.ShapeDtypeStruct((n // V7X_LANES, PEER_HEADS, PEER_N_KEYS, V7X_LANES), F32)
    by_i2_spec = pl.BlockSpec((groups, PEER_HEADS, PEER_N_KEYS, V7X_LANES), lambda i: (i, 0, 0, 0))
    sel = jax.ShapeDtypeStruct((PEER_HEADS, PEER_N_KEYS, n), F32)
    sel_spec = pl.BlockSpec((PEER_HEADS, PEER_N_KEYS, tm), lambda i: (0, 0, i))
    return pl.pallas_call(
        _peer_query_kernel,
        grid=(n // tm,),
        in_specs=[
            pl.BlockSpec((tm, d), lambda i: (i, 0)),
            pl.BlockSpec((1, d), lambda i: (0, 0)),
            pl.BlockSpec((d, dq), lambda i: (0, 0)),
            pl.BlockSpec((2 * PEER_HEADS, PEER_N_KEYS, PEER_D_HALF), lambda i: (0, 0, 0)),
            pl.BlockSpec((CAND_ROWS, V7X_LANES), lambda i: (0, 0)),
        ],
        out_specs=[pl.BlockSpec((tm, d), lambda i: (i, 0)), by_i2_spec, sel_spec, sel_spec],
        out_shape=[jax.ShapeDtypeStruct((n, d), BF16), by_i2, sel, sel],
        scratch_shapes=[pltpu.VMEM((2 * PEER_HEADS, PEER_N_KEYS, tm), F32)],
        compiler_params=_params("arbitrary"),
        name="peer_query",
    )(x, g, w_q, keys, _cand_flat_index())


GATE_ROWS = 32


def _gate_chunk(h_ref, a_ref, e2r_ref, cnt_ref, e1n_ref):
    groups, te, _ = h_ref.shape
    for k in range(te // PEER_N_KEYS):
        for s in range(groups):
            cols = slice(s * V7X_LANES, (s + 1) * V7X_LANES)
            cnt = [cnt_ref[r, k:k + 1, cols].astype(jnp.int32) for r in range(PEER_HEADS)]
            e1n = [e1n_ref[r, k:k + 1, cols] for r in range(PEER_HEADS)]
            for i2 in range(0, PEER_N_KEYS, GATE_ROWS):
                gate = jnp.zeros((GATE_ROWS, V7X_LANES), F32)
                for r in range(PEER_HEADS):
                    e2r = e2r_ref[s, r, i2:i2 + GATE_ROWS, :]
                    live = (pltpu.bitcast(e2r, jnp.int32) & RANK_MASK) < cnt[r]
                    gate = gate + jnp.where(live, e2r, 0.0) * e1n[r]
                row0 = k * PEER_N_KEYS + i2
                h = h_ref[s, row0:row0 + GATE_ROWS, :]
                act = 0.5 * h * (1.0 + lax.erf(h * SQRT_HALF))
                a_ref[row0:row0 + GATE_ROWS, cols] = (gate * act).astype(BF16)


def _peer_expert_kernel(xn_ref, x_ref, e2r_ref, cnt_ref, e1n_ref, u_ref, v_ref, o_ref, h_ref, a_ref):
    c = pl.program_id(1)

    @pl.when(c == 0)
    def _():
        o_ref[...] = x_ref[...]

    h = lax.dot_general(u_ref[...], xn_ref[...], NT_DIMS, preferred_element_type=F32)
    for s in range(h_ref.shape[0]):
        h_ref[s] = h[:, s * V7X_LANES:(s + 1) * V7X_LANES]
    _gate_chunk(h_ref, a_ref, e2r_ref, cnt_ref, e1n_ref)
    o_ref[...] += lax.dot_general(a_ref[...], v_ref[...], TN_DIMS, preferred_element_type=F32)


def peer_experts(xn, x, e2r, cnt, e1n, u, v, *, tm, te):
    n, d = x.shape
    tok = pl.BlockSpec((tm, d), lambda i, c: (i, 0))
    groups = tm // V7X_LANES
    by_i2 = pl.BlockSpec((groups, PEER_HEADS, PEER_N_KEYS, V7X_LANES), lambda i, c: (i, 0, 0, 0))
    by_i1 = pl.BlockSpec((PEER_HEADS, te // PEER_N_KEYS, tm), lambda i, c: (0, c, i))
    wts = pl.BlockSpec((te, d), lambda i, c: (c, 0))
    return pl.pallas_call(
        _peer_expert_kernel,
        grid=(n // tm, u.shape[0] // te),
        in_specs=[tok, tok, by_i2, by_i1, by_i1, wts, wts],
        out_specs=tok,
        out_shape=jax.ShapeDtypeStruct((n, d), F32),
        scratch_shapes=[pltpu.VMEM((groups, te, V7X_LANES), F32), pltpu.VMEM((te, tm), BF16)],
        compiler_params=_params("arbitrary", "arbitrary"),
        name="peer_experts",
    )(xn, x, e2r, cnt, e1n, u, v)


def peer_layer(x, g, w_q, keys, u, v, *, tm, te):
    xn, e2r, cnt, e1n = peer_query(x, g, w_q, keys, tm=tm)
    return peer_experts(xn, x, e2r, cnt, e1n, u, v, tm=tm, te=te)


def _residue_order(tm, dil):
    return np.arange(tm).reshape(tm // dil, dil).T.reshape(tm)


def _qkv_kernel(x_ref, g_ref, w_ref, qg_ref, kg_ref, cos_ref, sin_ref, o_ref, *rest, dil, tiles, tail_tile):
    kv_ref = None if tail_tile is None else rest[0]
    xs_ref, xp_ref = rest[-2:]
    tm, d = x_ref.shape
    hd = w_ref.shape[1] // 3
    per_res = tm // dil
    if dil > 1:
        for c in range(d // V7X_LANES):
            xs_ref[c] = x_ref[:, c * V7X_LANES:(c + 1) * V7X_LANES]
        for res in range(dil):
            for c in range(d // V7X_LANES):
                xp_ref[res * per_res:(res + 1) * per_res, c * V7X_LANES:(c + 1) * V7X_LANES] = (
                    xs_ref[c, pl.ds(res, per_res, stride=dil), :])
        x = xp_ref[...]
    else:
        x = x_ref[...]
    xn = _rms_norm(x, g_ref[...]).astype(BF16)
    cos, sin = cos_ref[...], sin_ref[...]

    def store(col0, val):
        for res in range(dil):
            o_ref[res, :, col0:col0 + val.shape[1]] = val[res * per_res:(res + 1) * per_res, :]

    for kind, gain_ref in enumerate((qg_ref, kg_ref, None)):
        y = jnp.dot(xn, w_ref[:, kind * hd:(kind + 1) * hd], preferred_element_type=F32)
        if gain_ref is None:
            store(kind * hd, y)
            continue
        gain = gain_ref[...]
        for h in range(hd // HEAD_DIM):
            yn = _rms_norm(y[:, h * HEAD_DIM:(h + 1) * HEAD_DIM], gain)
            store(kind * hd + h * HEAD_DIM, yn * cos + pltpu.roll(yn, HEAD_DIM // 2, axis=1) * sin)

    if kv_ref is None:
        return

    @pl.when(pl.program_id(0) % tiles >= tail_tile)
    def _():
        rows = kv_ref.shape[0]
        if dil == 1:
            kv_ref[...] = o_ref[0, tm - rows:, hd:]
            return
        slabs = xs_ref.shape[0]
        for half in range(2 * hd // (slabs * V7X_LANES)):
            col0 = hd + half * slabs * V7X_LANES
            for res in range(dil):
                for c in range(slabs):
                    xs_ref[c, pl.ds(res, per_res, stride=dil), :] = (
                        o_ref[res, :, col0 + c * V7X_LANES:col0 + (c + 1) * V7X_LANES])
            for c in range(slabs):
                kv_ref[:, col0 - hd + c * V7X_LANES:col0 - hd + (c + 1) * V7X_LANES] = xs_ref[c]


def qkv_project(x, g, w_qkv, q_gain, k_gain, pos, group, *, dil, batch, tm, keep=0):
    n, d = x.shape
    seq = n // batch
    hd3 = w_qkv.shape[1] // N_GROUPS
    tiles = seq // tm
    per_res = tm // dil
    assert per_res % V7X_SUBLANES == 0
    out_specs = [pl.BlockSpec((None, dil, per_res, hd3), lambda i: (i // tiles, 0, i % tiles, 0))]
    out_shape = [jax.ShapeDtypeStruct((batch, dil, seq // dil, hd3), F32)]
    tail_tile = None
    if keep:
        kv_rows = min(keep, tm)
        assert keep % kv_rows == 0 and (dil == 1 or kv_rows == tm)
        tail_tile = tiles - keep // kv_rows
        out_specs.append(pl.BlockSpec((None, kv_rows, 2 * hd3 // 3),
                                      lambda i: (i // tiles, jnp.maximum(i % tiles - tail_tile, 0), 0)))
        out_shape.append(jax.ShapeDtypeStruct((batch, keep, 2 * hd3 // 3), F32))
    order = (np.arange(tiles)[:, None] * tm + _residue_order(tm, dil)[None, :]).reshape(seq)
    cos, sin = _rope_tables(jnp.asarray(np.asarray(pos)[order]))
    gain = pl.BlockSpec((None, 1, HEAD_DIM), lambda i: (group, 0, 0))
    rope = pl.BlockSpec((tm, HEAD_DIM), lambda i: (i % tiles, 0))
    outs = pl.pallas_call(
        functools.partial(_qkv_kernel, dil=dil, tiles=tiles, tail_tile=tail_tile),
        grid=(n // tm,),
        in_specs=[
            pl.BlockSpec((tm, d), lambda i: (i, 0)),
            pl.BlockSpec((1, d), lambda i: (0, 0)),
            pl.BlockSpec((d, hd3), lambda i: (0, group), pipeline_mode=pl.Buffered(1)),
            gain, gain, rope, rope,
        ],
        out_specs=out_specs,
        out_shape=out_shape,
        scratch_shapes=[pltpu.VMEM((d // V7X_LANES, tm, V7X_LANES), F32), pltpu.VMEM((tm, d), F32)],
        compiler_params=_params("arbitrary"),
        name=f"qkv_project_g{group}_d{dil}",
    )(x, g, w_qkv, q_gain.reshape(N_GROUPS, 1, HEAD_DIM), k_gain.reshape(N_GROUPS, 1, HEAD_DIM), cos, sin)
    return outs if keep else outs[0]


def _spread_heads(cols_list, width):
    rows = cols_list[0].shape[0]
    lane = lax.broadcasted_iota(jnp.int32, (rows, width), 1)
    out = jnp.zeros((rows, width), F32)
    for h, col in enumerate(cols_list):
        out = jnp.where(lane == h, col, out)
    return out


def _attn_prompt_kernel(q_ref, kc_ref, kp_ref, vc_ref, vp_ref, o_ref, lse_ref, *, window_blocks):
    i = pl.program_id(2)
    qb = q_ref.shape[0]
    qi = lax.broadcasted_iota(jnp.int32, (qb, 2 * qb), 0)
    kj = lax.broadcasted_iota(jnp.int32, (qb, 2 * qb), 1)
    live = (kj <= qi + qb) & (kj >= qi + qb - window_blocks) & ((i > 0) | (kj >= qb))
    lses = []
    for h in range(q_ref.shape[1] // HEAD_DIM):
        cols = slice(h * HEAD_DIM, (h + 1) * HEAD_DIM)
        q = q_ref[:, cols].astype(BF16)
        k = jnp.concatenate([kp_ref[:, cols], kc_ref[:, cols]], axis=0).astype(BF16)
        v = jnp.concatenate([vp_ref[:, cols], vc_ref[:, cols]], axis=0).astype(BF16)
        s = lax.dot_general(q, k, NT_DIMS, preferred_element_type=F32) * ATTN_SCALE
        s = jnp.where(live, s, NEG_INF)
        m = jnp.max(s, axis=-1, keepdims=True)
        p = jnp.exp(s - m)
        l = jnp.sum(p, axis=-1, keepdims=True)
        o_ref[:, cols] = jnp.dot(p.astype(BF16), v, preferred_element_type=F32) / l
        lses.append(m + jnp.log(l))
    lse_ref[...] = _spread_heads(lses, lse_ref.shape[1])


def attn_prompt_group(qkv, group):
    batch, dil, n_strided, hd3 = qkv.shape
    hd = hd3 // 3
    win = WINDOWS[group]
    assert dil == DILATIONS[group] and n_strided % ATTN_BLOCK == 0 and win // dil <= ATTN_BLOCK
    blk = (None, None, ATTN_BLOCK, hd)

    def cur(kind):
        return pl.BlockSpec(blk, lambda b, r, i: (b, r, i, kind))

    def prev(kind):
        return pl.BlockSpec(blk, lambda b, r, i: (b, r, jnp.maximum(i - 1, 0), kind))

    return pl.pallas_call(
        functools.partial(_attn_prompt_kernel, window_blocks=win // dil),
        grid=(batch, dil, n_strided // ATTN_BLOCK),
        in_specs=[cur(0), cur(1), prev(1), cur(2), prev(2)],
        out_specs=[pl.BlockSpec(blk, lambda b, r, i: (b, r, i, 0)),
                   pl.BlockSpec((None, None, ATTN_BLOCK, V7X_LANES), lambda b, r, i: (b, r, i, 0))],
        out_shape=[jax.ShapeDtypeStruct((batch, dil, n_strided, hd), F32),
                   jax.ShapeDtypeStruct((batch, dil, n_strided, V7X_LANES), F32)],
        compiler_params=_params("arbitrary", "arbitrary", "arbitrary"),
        name=f"attn_prompt_g{group}",
    )(qkv, qkv, qkv, qkv, qkv)


def _head_segments(n_heads, transpose):
    shape = (V7X_LANES, n_heads * HEAD_DIM) if transpose else (n_heads * HEAD_DIM, V7X_LANES)
    c = lax.broadcasted_iota(jnp.int32, shape, 1 if transpose else 0)
    h = lax.broadcasted_iota(jnp.int32, shape, 0 if transpose else 1)
    return jnp.where((c >= h * HEAD_DIM) & (c < (h + 1) * HEAD_DIM), 1.0, 0.0).astype(BF16)


def _attn_sample_kernel(q0_ref, q1_ref, q2_ref, c0_ref, c1_ref, c2_ref, o_ref, lse_ref):
    t = pl.program_id(1)
    hd = o_ref.shape[2]
    n_heads = hd // HEAD_DIM
    t_new = q0_ref.shape[0]
    seg = _head_segments(n_heads, transpose=False)
    seg_t = _head_segments(n_heads, transpose=True)
    row_c = lax.broadcasted_iota(jnp.int32, (c0_ref.shape[0], V7X_LANES), 0)
    row_n = lax.broadcasted_iota(jnp.int32, (t_new, V7X_LANES), 0)
    for g, (qkv_ref, cache_ref) in enumerate(zip((q0_ref, q1_ref, q2_ref), (c0_ref, c1_ref, c2_ref))):
        q = qkv_ref[pl.ds(t, 1), 0:hd]
        k_new = qkv_ref[:, hd:2 * hd]
        v_new = qkv_ref[:, 2 * hd:3 * hd]
        k_old, v_old = cache_ref[:, 0:hd], cache_ref[:, hd:2 * hd]
        s_old = _split_dot(k_old * q, seg) * ATTN_SCALE
        s_new = _split_dot(k_new * q, seg) * ATTN_SCALE
        if DILATIONS[g] == 1:
            s_old = jnp.where(row_c >= t, s_old, NEG_INF)
            s_new = jnp.where(row_n <= t, s_new, NEG_INF)
        else:
            s_new = jnp.where(row_n == t, s_new, NEG_INF)
        m = jnp.maximum(jnp.max(s_old, axis=0, keepdims=True), jnp.max(s_new, axis=0, keepdims=True))
        p_old, p_new = jnp.exp(s_old - m), jnp.exp(s_new - m)
        l = jnp.sum(p_old, axis=0, keepdims=True) + jnp.sum(p_new, axis=0, keepdims=True)
        w_old = _split_dot(p_old / l, seg_t)
        w_new = _split_dot(p_new / l, seg_t)
        o = jnp.sum(w_old * v_old, axis=0, keepdims=True) + jnp.sum(w_new * v_new, axis=0, keepdims=True)
        o_ref[g, pl.ds(t, 1), :] = o
        lse_ref[g, pl.ds(t, 1), :] = m + jnp.log(l)


def attn_sample(qkv, caches, *, batch, t_new):
    cols = qkv[0].shape[1]
    hd = cols // 3
    views, specs = [], []
    for g, cache in enumerate(caches):
        dil, keep = DILATIONS[g], cache.shape[1]
        assert keep == WINDOWS[g] and keep // dil == ATTN_BLOCK and (dil == 1 or t_new <= dil)
        if dil == 1:
            views.append(cache.reshape(batch, keep, 2 * hd))
            specs.append(pl.BlockSpec((None, ATTN_BLOCK, 2 * hd), lambda b, t: (b, 0, 0)))
        else:
            by_res = cache.reshape(batch, keep // dil, dil, 2 * hd)[:, :, :t_new].transpose(0, 2, 1, 3)
            views.append(by_res)
            specs.append(pl.BlockSpec((None, None, ATTN_BLOCK, 2 * hd), lambda b, t: (b, t, 0, 0)))
    rows_new = -(-t_new // V7X_SUBLANES) * V7X_SUBLANES
    qkv_rows = [jnp.pad(a.reshape(batch, t_new, cols), ((0, 0), (0, rows_new - t_new), (0, 0))) for a in qkv]
    o, lse = pl.pallas_call(
        _attn_sample_kernel,
        grid=(batch, t_new),
        in_specs=[pl.BlockSpec((None, rows_new, cols), lambda b, t: (b, 0, 0))] * N_GROUPS + specs,
        out_specs=[pl.BlockSpec((N_GROUPS, None, t_new, hd), lambda b, t: (0, b, 0, 0)),
                   pl.BlockSpec((N_GROUPS, None, t_new, V7X_LANES), lambda b, t: (0, b, 0, 0))],
        out_shape=[jax.ShapeDtypeStruct((N_GROUPS, batch, t_new, hd), F32),
                   jax.ShapeDtypeStruct((N_GROUPS, batch, t_new, V7X_LANES), F32)],
        compiler_params=_params("arbitrary", "arbitrary"),
        name="attn_sample",
    )(*qkv_rows, *views)
    return o.reshape(N_GROUPS, batch * t_new, hd), lse.reshape(N_GROUPS, batch * t_new, V7X_LANES)


def _attn_out_kernel(x_ref, o0_ref, o1_ref, o2_ref, l0_ref, l1_ref, l2_ref, wo_ref, y_ref,
                     og_ref, lg_ref, mix_ref):
    tm = x_ref.shape[0]
    hd = wo_ref.shape[0]
    heads = hd // HEAD_DIM
    for g, (o_ref, l_ref) in enumerate(zip((o0_ref, o1_ref, o2_ref), (l0_ref, l1_ref, l2_ref))):
        dil = o_ref.shape[0]
        for res in range(dil):
            rows = pl.ds(res, tm // dil, stride=dil) if dil > 1 else slice(0, tm)
            lg_ref[g, rows, :] = l_ref[res]
            for h in range(heads):
                og_ref[g, h, rows, :] = o_ref[res, :, h * HEAD_DIM:(h + 1) * HEAD_DIM]
    lse = [lg_ref[g] for g in range(N_GROUPS)]
    m = functools.reduce(jnp.maximum, lse)
    e = [jnp.exp(l - m) for l in lse]
    tot = functools.reduce(jnp.add, e)
    seg_t = _head_segments(heads, transpose=True)
    w = [_split_dot(e[g] / tot, seg_t) for g in range(N_GROUPS)]
    for h in range(heads):
        cols = slice(h * HEAD_DIM, (h + 1) * HEAD_DIM)
        mix = functools.reduce(jnp.add, [w[g][:, cols] * og_ref[g, h] for g in range(N_GROUPS)])
        mix_ref[:, cols] = mix.astype(BF16)
    y_ref[...] = x_ref[...] + jnp.dot(mix_ref[...], wo_ref[...], preferred_element_type=F32)


def attn_out(x, outs, lses, w_o, *, tm):
    n, d = x.shape
    batch = outs[0].shape[0]
    hd = outs[0].shape[3]
    tiles = n // batch // tm

    def by_residue(a):
        dil = a.shape[1]
        return pl.BlockSpec((None, dil, tm // dil, a.shape[3]), lambda i: (i // tiles, 0, i % tiles, 0))

    return pl.pallas_call(
        _attn_out_kernel,
        grid=(n // tm,),
        in_specs=[pl.BlockSpec((tm, d), lambda i: (i, 0))] + [by_residue(a) for a in (*outs, *lses)]
        + [pl.BlockSpec((hd, d), lambda i: (0, 0))],
        out_specs=pl.BlockSpec((tm, d), lambda i: (i, 0)),
        out_shape=jax.ShapeDtypeStruct((n, d), F32),
        scratch_shapes=[pltpu.VMEM((N_GROUPS, hd // HEAD_DIM, tm, HEAD_DIM), F32),
                        pltpu.VMEM((N_GROUPS, tm, V7X_LANES), F32), pltpu.VMEM((tm, hd), BF16)],
        compiler_params=_params("arbitrary"),
        name="attn_out",
    )(x, *outs, *lses, w_o)


def _rope_tables(pos):
    half = HEAD_DIM // 2
    inv_freq = ROPE_THETA ** (-jnp.arange(half, dtype=F32) / half)
    ang = pos.astype(F32)[:, None] * inv_freq
    cos, sin = jnp.cos(ang), jnp.sin(ang)
    return jnp.concatenate([cos, cos], axis=-1), jnp.concatenate([-sin, sin], axis=-1)


def _pad_rows(a, rows):
    return jnp.pad(a, ((0, rows - a.shape[0]),) + ((0, 0),) * (a.ndim - 1))


PROMPT_TM = 512
ATTN_OUT_TM = 256
QKV_TM = 256
PEER_TE = 1024
CONV_TN = 512
SAMPLE_PEER_ROWS = V7X_LANES


def kernel(x_prompt, x_sample, state_conv, cache_kv_w128, cache_kv_w512, cache_kv_w2048,
           norm_mix, norm_ffn, conv_w_in, conv_w, conv_w_out,
           attn_w_qkv, attn_q_norm, attn_k_norm, attn_w_o,
           peer_w_q, peer_sub_keys, peer_u, peer_v):
    batch, seq, d = x_prompt.shape
    dec_batch, dec_seq, _ = x_sample.shape
    n_p, n_s = batch * seq, dec_batch * dec_seq
    caches = (cache_kv_w128, cache_kv_w512, cache_kv_w2048)
    n_heads = caches[0].shape[3]
    hd = n_heads * HEAD_DIM

    w_in, w_out = conv_w_in.astype(BF16), conv_w_out.astype(BF16)
    w_qkv, w_o = attn_w_qkv.astype(BF16), attn_w_o.astype(BF16)
    w_q = peer_w_q.astype(BF16)
    keys = peer_sub_keys.astype(BF16).reshape(peer_sub_keys.shape[0], 2 * PEER_HEADS, PEER_N_KEYS, PEER_D_HALF)
    pu, pv = peer_u.astype(BF16), peer_v.astype(BF16)
    g_mix = norm_mix.reshape(-1, 1, d)
    g_ffn = norm_ffn.reshape(-1, 1, d)

    xp = x_prompt.reshape(n_p, d)
    xs = x_sample.reshape(n_s, d)

    def peer(x, layer, tm):
        return peer_layer(x, g_ffn[layer], w_q[layer], keys[layer], pu[layer], pv[layer], tm=tm, te=PEER_TE)

    def peer_sample(x, layer):
        return peer(_pad_rows(x, SAMPLE_PEER_ROWS), layer, SAMPLE_PEER_ROWS)[:n_s]

    bp, up = conv_in(xp, g_mix[0], w_in, tm=PROMPT_TM, tn=CONV_TN)
    xp = conv_out_prompt(xp, bp, up, jnp.zeros((batch, CONV_WIDTH - 1, d), F32), conv_w, w_out,
                         seq_len=seq, tm=PROMPT_TM)
    conv_state_prompt = up.reshape(batch, seq, d)[:, seq - (CONV_WIDTH - 1):]

    bs, us = conv_in(xs, g_mix[0], w_in, tm=n_s, tn=CONV_TN)
    u_ext = jnp.concatenate([state_conv, us.reshape(dec_batch, dec_seq, d)], axis=1)
    shifted = [u_ext[:, k:k + dec_seq].reshape(n_s, d) for k in range(CONV_WIDTH)]
    xs = conv_out_sample(xs, bs, shifted[0], shifted[1], shifted[2], conv_w, w_out)
    conv_state_sample = u_ext[:, dec_seq:]

    xp = peer(xp, 0, PROMPT_TM)
    xs = peer_sample(xs, 0)

    qkv_p, kv_prompt = zip(*[
        qkv_project(xp, g_mix[1], w_qkv, attn_q_norm, attn_k_norm, np.arange(seq), g,
                    dil=DILATIONS[g], batch=batch, tm=QKV_TM, keep=min(WINDOWS[g], seq)) for g in range(N_GROUPS)])
    outs, lses = zip(*[attn_prompt_group(qkv_p[g], g) for g in range(N_GROUPS)])
    xp = attn_out(xp, outs, lses, w_o, tm=ATTN_OUT_TM)

    pos_s = np.tile(PAST_LEN + np.arange(dec_seq), dec_batch)
    qkv_s = [qkv_project(xs, g_mix[1], w_qkv, attn_q_norm, attn_k_norm, pos_s, g, dil=1, batch=1, tm=n_s)
             .reshape(n_s, 3 * hd) for g in range(N_GROUPS)]
    o_s, lse_s = attn_sample(qkv_s, caches, batch=dec_batch, t_new=dec_seq)
    xs = attn_out(xs, [o_s[g].reshape(1, 1, n_s, hd) for g in range(N_GROUPS)],
                  [lse_s[g].reshape(1, 1, n_s, V7X_LANES) for g in range(N_GROUPS)], w_o, tm=n_s)

    xp = peer(xp, 1, PROMPT_TM)
    xs = peer_sample(xs, 1)

    kv_prompt = [kv.reshape(batch, kv.shape[1], 2, n_heads, HEAD_DIM) for kv in kv_prompt]
    kv_sample = [jnp.concatenate([caches[g][:, dec_seq:],
                                  qkv_s[g][:, hd:].reshape(dec_batch, dec_seq, 2, n_heads, HEAD_DIM)], axis=1)
                 for g in range(N_GROUPS)]

    return (xp.reshape(batch, seq, d), xs.reshape(dec_batch, dec_seq, d),
            conv_state_prompt, conv_state_sample, *kv_prompt, *kv_sample)
```

```python
import functools
import math

import jax
import jax.numpy as jnp
import numpy as np
from jax import lax
from jax.experimental import pallas as pl
from jax.experimental.pallas import tpu as pltpu

F32 = jnp.float32
BF16 = jnp.bfloat16

V7X_LANES = 128
V7X_SUBLANES = 8
V7X_VMEM_BYTES = 64 * 1024 * 1024
VMEM_LIMIT_BYTES = V7X_VMEM_BYTES - 8 * 1024 * 1024

NORM_EPS = 1e-6
CONV_WIDTH = 3
WINDOWS = (128, 512, 2048)
DILATIONS = (1, 4, 16)
N_GROUPS = 3
HEAD_DIM = 128
ATTN_BLOCK = 128
ATTN_SCALE = HEAD_DIM ** -0.5
ROPE_THETA = 10000.0
PAST_LEN = 16384
assert PAST_LEN >= max(WINDOWS)
PEER_HEADS = 8
PEER_N_KEYS = 128
PEER_TOPK = 16
PEER_D_HALF = 128
SQRT_HALF = math.sqrt(0.5)
RANK_MASK = 31
NOT_SELECTED = float(RANK_MASK)
NEG_INF = float("-inf")

NT_DIMS = (((1,), (1,)), ((), ()))
TN_DIMS = (((0,), (0,)), ((), ()))


def _params(*semantics):
    return pltpu.CompilerParams(dimension_semantics=semantics, vmem_limit_bytes=VMEM_LIMIT_BYTES)


def _rms_norm(x, g):
    return x * lax.rsqrt(jnp.mean(x * x, axis=-1, keepdims=True) + NORM_EPS) * g


def _split_dot(a, b01):
    hi = a.astype(BF16)
    lo = (a - hi.astype(F32)).astype(BF16)
    return (jnp.dot(hi, b01, preferred_element_type=F32)
            + jnp.dot(lo, b01, preferred_element_type=F32))


def _conv_in_kernel(x_ref, g_ref, wb_ref, wc_ref, wh_ref, b_ref, u_ref, xn_ref):
    @pl.when(pl.program_id(1) == 0)
    def _():
        xn_ref[...] = _rms_norm(x_ref[...], g_ref[...]).astype(BF16)

    xn = xn_ref[...]
    b_ref[...] = jnp.dot(xn, wb_ref[...], preferred_element_type=F32)
    c = jnp.dot(xn, wc_ref[...], preferred_element_type=F32)
    h = jnp.dot(xn, wh_ref[...], preferred_element_type=F32)
    u_ref[...] = c * h


def conv_in(x, g, w_in, *, tm, tn):
    n, d = x.shape
    nj = d // tn
    return pl.pallas_call(
        _conv_in_kernel,
        grid=(n // tm, nj),
        in_specs=[
            pl.BlockSpec((tm, d), lambda i, j: (i, 0)),
            pl.BlockSpec((1, d), lambda i, j: (0, 0)),
            pl.BlockSpec((d, tn), lambda i, j: (0, j)),
            pl.BlockSpec((d, tn), lambda i, j: (0, j + nj)),
            pl.BlockSpec((d, tn), lambda i, j: (0, j + 2 * nj)),
        ],
        out_specs=[pl.BlockSpec((tm, tn), lambda i, j: (i, j)),
                   pl.BlockSpec((tm, tn), lambda i, j: (i, j))],
        out_shape=[jax.ShapeDtypeStruct((n, d), F32), jax.ShapeDtypeStruct((n, d), F32)],
        scratch_shapes=[pltpu.VMEM((tm, d), BF16)],
        compiler_params=_params("arbitrary", "arbitrary"),
        name="conv_in",
    )(x, g, w_in, w_in, w_in)


def _conv_mix(cw_ref, um2, um1, u0):
    return cw_ref[0:1, :] * um2 + cw_ref[1:2, :] * um1 + cw_ref[2:3, :] * u0


def _conv_out_prompt_kernel(x_ref, b_ref, u_ref, up_ref, st_ref, cw_ref, wo_ref, o_ref, *, tiles_per_seq):
    i = pl.program_id(0)
    u = u_ref[...]
    first = (i % tiles_per_seq) == 0
    prev = jnp.where(first, st_ref[...], up_ref[V7X_SUBLANES - 2:V7X_SUBLANES, :])
    row = lax.broadcasted_iota(jnp.int32, (u.shape[0], 1), 0)
    um1 = jnp.where(row == 0, prev[1:2, :], pltpu.roll(u, 1, axis=0))
    um2 = jnp.where(row == 0, prev[0:1, :], jnp.where(row == 1, prev[1:2, :], pltpu.roll(u, 2, axis=0)))
    a = (b_ref[...] * _conv_mix(cw_ref, um2, um1, u)).astype(BF16)
    o_ref[...] = x_ref[...] + jnp.dot(a, wo_ref[...], preferred_element_type=F32)


def conv_out_prompt(x, bgate, u, state, conv_w, w_out, *, seq_len, tm):
    n, d = x.shape
    tps = seq_len // tm
    sub_per_tile = tm // V7X_SUBLANES
    tile = pl.BlockSpec((tm, d), lambda i: (i, 0))
    return pl.pallas_call(
        functools.partial(_conv_out_prompt_kernel, tiles_per_seq=tps),
        grid=(n // tm,),
        in_specs=[
            tile, tile, tile,
            pl.BlockSpec((V7X_SUBLANES, d), lambda i: (jnp.maximum(i * sub_per_tile - 1, 0), 0)),
            pl.BlockSpec((None, CONV_WIDTH - 1, d), lambda i: (i // tps, 0, 0)),
            pl.BlockSpec((CONV_WIDTH, d), lambda i: (0, 0)),
            pl.BlockSpec((d, d), lambda i: (0, 0)),
        ],
        out_specs=tile,
        out_shape=jax.ShapeDtypeStruct((n, d), F32),
        compiler_params=_params("arbitrary"),
        name="conv_out_prompt",
    )(x, bgate, u, u, state, conv_w, w_out)


def _conv_out_sample_kernel(x_ref, b_ref, um2_ref, um1_ref, u_ref, cw_ref, wo_ref, o_ref):
    a = (b_ref[...] * _conv_mix(cw_ref, um2_ref[...], um1_ref[...], u_ref[...])).astype(BF16)
    o_ref[...] = x_ref[...] + jnp.dot(a, wo_ref[...], preferred_element_type=F32)


def conv_out_sample(x, bgate, um2, um1, u0, conv_w, w_out):
    n, d = x.shape
    full = pl.BlockSpec((n, d), lambda i: (0, 0))
    return pl.pallas_call(
        _conv_out_sample_kernel,
        grid=(1,),
        in_specs=[full, full, full, full, full,
                  pl.BlockSpec((CONV_WIDTH, d), lambda i: (0, 0)),
                  pl.BlockSpec((d, d), lambda i: (0, 0))],
        out_specs=full,
        out_shape=jax.ShapeDtypeStruct((n, d), F32),
        compiler_params=_params("arbitrary"),
        name="conv_out_sample",
    )(x, bgate, um2, um1, u0, conv_w, w_out)


CAND_ROWS = 72


def _cand_flat_index():
    pairs = []
    for b in range(4):
        pairs += [(a, b) for a in range(16 if b == 0 else 8)]
    pairs += [(0, b) if b >= 4 else None for b in range(16)]
    pairs += [(1, b) if b >= 4 else None for b in range(8)]
    pairs += [(2, b) if b == 4 else None for b in range(8)]
    rows = [a * PEER_TOPK + b if pair is not None and (a + 1) * (b + 1) <= PEER_TOPK else 256 + i
            for i, pair in enumerate(pairs) for a, b in [pair or (99, 99)]]
    assert len(rows) == CAND_ROWS and len({r for r in rows if r < 256}) == 50
    return jnp.broadcast_to(jnp.asarray(rows, F32).reshape(CAND_ROWS, 1), (CAND_ROWS, V7X_LANES))


def _top16(s, key_iota, exact_ties):
    work = s
    rank = jnp.full(s.shape, NOT_SELECTED, F32)
    val_row = lax.broadcasted_iota(jnp.int32, (PEER_TOPK, s.shape[1]), 0)
    vals = jnp.zeros((PEER_TOPK, s.shape[1]), F32)
    for a in range(PEER_TOPK):
        m = jnp.max(work, axis=0, keepdims=True)
        sel = work == m
        if exact_ties:
            first = jnp.min(jnp.where(sel, key_iota, float(PEER_N_KEYS)), axis=0, keepdims=True)
            sel = key_iota == first
        rank = jnp.where(sel, float(a), rank)
        work = jnp.where(sel, NEG_INF, work)
        vals = jnp.where(val_row == a, m, vals)
    return vals, rank


def _ranked_rows(rank):
    return jnp.sum(jnp.where(rank < NOT_SELECTED, 1.0, 0.0), axis=0, keepdims=True)


def _peer_select(s1, s2, fidx, exact_ties):
    lanes = s1.shape[1]
    key_iota = lax.broadcasted_iota(jnp.int32, (PEER_N_KEYS, lanes), 0).astype(F32)
    v1, rank1 = _top16(s1, key_iota, exact_ties)
    v2, rank2 = _top16(s2, key_iota, exact_ties)

    r16 = lax.broadcasted_iota(jnp.int32, (16, lanes), 0)
    r8 = lax.broadcasted_iota(jnp.int32, (8, lanes), 0)
    v1_lo, v2_lo = v1[0:8, :], v2[0:8, :]
    cand = jnp.concatenate([
        v1 + v2[0:1, :],
        v1_lo + v2[1:2, :],
        jnp.where(r8 < 5, v1_lo + v2[2:3, :], NEG_INF),
        jnp.where(r8 < 4, v1_lo + v2[3:4, :], NEG_INF),
        jnp.where(r16 >= 4, v1[0:1, :] + v2, NEG_INF),
        jnp.where(r8 >= 4, v1[1:2, :] + v2_lo, NEG_INF),
        jnp.where(r8 == 4, v1[2:3, :] + v2_lo, NEG_INF),
    ], axis=0)
    c_top = v1[0:1, :] + v2[0:1, :]
    taken = jnp.zeros(cand.shape, F32)
    z = jnp.zeros((1, lanes), F32)
    for _ in range(PEER_TOPK):
        m = jnp.max(cand, axis=0, keepdims=True)
        first = jnp.min(jnp.where(cand == m, fidx, 1024.0), axis=0, keepdims=True)
        sel = fidx == first
        taken = jnp.where(sel, 1.0, taken)
        cand = jnp.where(sel, NEG_INF, cand)
        z = z + jnp.exp(m - c_top)

    extra = jnp.where(r8 == 0, jnp.sum(taken[40:56, :], axis=0, keepdims=True),
                      jnp.where(r8 == 1, jnp.sum(taken[56:64, :], axis=0, keepdims=True),
                                jnp.where(r8 == 2, jnp.sum(taken[64:72, :], axis=0, keepdims=True), 0.0)))
    cnt_lo = taken[0:8, :] + taken[16:24, :] + taken[24:32, :] + taken[32:40, :] + extra
    cnt_vec = jnp.concatenate([cnt_lo, taken[8:16, :]], axis=0)
    cnt = jnp.zeros(s1.shape, F32)
    for a in range(PEER_TOPK):
        cnt = jnp.where(rank1 == float(a), cnt_vec[a:a + 1, :], cnt)

    e1n = jnp.exp(s1 - v1[0:1, :]) / z
    e2 = jnp.exp(s2 - v2[0:1, :])
    return rank2, e2, cnt, e1n, jnp.maximum(_ranked_rows(rank1), _ranked_rows(rank2))


def _peer_query_kernel(x_ref, g_ref, wq_ref, keys_ref, fidx_ref,
                       xn_ref, e2r_ref, cnt_ref, e1n_ref, sc_ref):
    tm = x_ref.shape[0]
    xn = _rms_norm(x_ref[...], g_ref[...]).astype(BF16)
    xn_ref[...] = xn
    q = jnp.dot(xn, wq_ref[...], preferred_element_type=F32).astype(BF16)
    for rp in range(2 * PEER_HEADS):
        q_rp = q[:, rp * PEER_D_HALF:(rp + 1) * PEER_D_HALF]
        sc_ref[rp] = lax.dot_general(keys_ref[rp], q_rp, NT_DIMS, preferred_element_type=F32)

    fidx = fidx_ref[...]

    def select_head(r, exact_ties):
        ranked = jnp.zeros((1, V7X_LANES), F32)
        for s in range(tm // V7X_LANES):
            cols = slice(s * V7X_LANES, (s + 1) * V7X_LANES)
            rank2, e2, cnt, e1n, n = _peer_select(sc_ref[2 * r, :, cols], sc_ref[2 * r + 1, :, cols], fidx,
                                                  exact_ties)
            e2_bits = pltpu.bitcast(e2, jnp.int32) & ~RANK_MASK
            e2r_ref[s, r] = pltpu.bitcast(e2_bits | rank2.astype(jnp.int32), F32)
            cnt_ref[r, :, cols] = cnt
            e1n_ref[r, :, cols] = e1n
            ranked = jnp.maximum(ranked, n)
        return ranked

    def head_body(r, carry):
        ranked = select_head(r, exact_ties=False)

        @pl.when(jnp.max(ranked) > float(PEER_TOPK))
        def _():
            select_head(r, exact_ties=True)

        return carry

    lax.fori_loop(0, PEER_HEADS, head_body, 0)


def peer_query(x, g, w_q, keys, layer, *, tm):
    n, d = x.shape
    dq = w_q.shape[2]
    groups = tm // V7X_LANES
    by_i2 = jax.ShapeDtypeStruct((n // V7X_LANES, PEER_HEADS, PEER_N_KEYS, V7X_LANES), F32)
    by_i2_spec = pl.BlockSpec((groups, PEER_HEADS, PEER_N_KEYS, V7X_LANES), lambda i: (i, 0, 0, 0))
    sel = jax.ShapeDtypeStruct((PEER_HEADS, PEER_N_KEYS, n), F32)
    sel_spec = pl.BlockSpec((PEER_HEADS, PEER_N_KEYS, tm), lambda i: (0, 0, i))
    return pl.pallas_call(
        _peer_query_kernel,
        grid=(n // tm,),
        in_specs=[
            pl.BlockSpec((tm, d), lambda i: (i, 0)),
            pl.BlockSpec((1, d), lambda i: (0, 0)),
            pl.BlockSpec((None, d, dq), lambda i: (layer, 0, 0)),
            pl.BlockSpec((None, 2 * PEER_HEADS, PEER_N_KEYS, PEER_D_HALF), lambda i: (layer, 0, 0, 0)),
            pl.BlockSpec((CAND_ROWS, V7X_LANES), lambda i: (0, 0)),
        ],
        out_specs=[pl.BlockSpec((tm, d), lambda i: (i, 0)), by_i2_spec, sel_spec, sel_spec],
        out_shape=[jax.ShapeDtypeStruct((n, d), BF16), by_i2, sel, sel],
        scratch_shapes=[pltpu.VMEM((2 * PEER_HEADS, PEER_N_KEYS, tm), F32)],
        compiler_params=_params("arbitrary"),
        name="peer_query",
    )(x, g, w_q, keys, _cand_flat_index())


GATE_ROWS = 32


def _gate_chunk(h, a_ref, e2r_ref, cnt_ref, e1n_ref):
    te, tm = h.shape
    for k in range(te // PEER_N_KEYS):
        for s in range(tm // V7X_LANES):
            cols = slice(s * V7X_LANES, (s + 1) * V7X_LANES)
            cnt = [cnt_ref[r, k:k + 1, cols].astype(jnp.int32) for r in range(PEER_HEADS)]
            e1n = [e1n_ref[r, k:k + 1, cols] for r in range(PEER_HEADS)]
            for i2 in range(0, PEER_N_KEYS, GATE_ROWS):
                gate = jnp.zeros((GATE_ROWS, V7X_LANES), F32)
                for r in range(PEER_HEADS):
                    e2r = e2r_ref[s, r, i2:i2 + GATE_ROWS, :]
                    live = (pltpu.bitcast(e2r, jnp.int32) & RANK_MASK) < cnt[r]
                    gate = gate + jnp.where(live, e2r, 0.0) * e1n[r]
                row0 = k * PEER_N_KEYS + i2
                hp = h[row0:row0 + GATE_ROWS, cols]
                act = 0.5 * hp * (1.0 + lax.erf(hp * SQRT_HALF))
                a_ref[row0:row0 + GATE_ROWS, cols] = (gate * act).astype(BF16)


def _peer_expert_kernel(xn_ref, x_ref, e2r_ref, cnt_ref, e1n_ref, u_ref, v_ref, o_ref, a_ref):
    c = pl.program_id(1)

    @pl.when(c == 0)
    def _():
        o_ref[...] = x_ref[...]

    h = lax.dot_general(u_ref[...], xn_ref[...], NT_DIMS, preferred_element_type=F32)
    _gate_chunk(h, a_ref, e2r_ref, cnt_ref, e1n_ref)
    o_ref[...] += lax.dot_general(a_ref[...], v_ref[...], TN_DIMS, preferred_element_type=F32)


def peer_experts(xn, x, e2r, cnt, e1n, u, v, layer, *, tm, te):
    n, d = x.shape
    tok = pl.BlockSpec((tm, d), lambda i, c: (i, 0))
    groups = tm // V7X_LANES
    by_i2 = pl.BlockSpec((groups, PEER_HEADS, PEER_N_KEYS, V7X_LANES), lambda i, c: (i, 0, 0, 0))
    by_i1 = pl.BlockSpec((PEER_HEADS, te // PEER_N_KEYS, tm), lambda i, c: (0, c, i))
    wts = pl.BlockSpec((None, te, d), lambda i, c: (layer, c, 0))
    return pl.pallas_call(
        _peer_expert_kernel,
        grid=(n // tm, u.shape[1] // te),
        in_specs=[tok, tok, by_i2, by_i1, by_i1, wts, wts],
        out_specs=tok,
        out_shape=jax.ShapeDtypeStruct((n, d), F32),
        scratch_shapes=[pltpu.VMEM((te, tm), BF16)],
        compiler_params=_params("arbitrary", "arbitrary"),
        name="peer_experts",
    )(xn, x, e2r, cnt, e1n, u, v)


def peer_layer(x, g, w_q, keys, u, v, layer, *, tm, te):
    xn, e2r, cnt, e1n = peer_query(x, g, w_q, keys, layer, tm=tm)
    return peer_experts(xn, x, e2r, cnt, e1n, u, v, layer, tm=tm, te=te)


def _residue_order(tm, dil):
    return np.arange(tm).reshape(tm // dil, dil).T.reshape(tm)


def _qkv_kernel(x_ref, g_ref, w_ref, qg_ref, kg_ref, cos_ref, sin_ref, o_ref, *rest, dil, tiles, tail_tile):
    kv_ref = None if tail_tile is None else rest[0]
    xs_ref, xp_ref = rest[-2:]
    tm, d = x_ref.shape
    hd = w_ref.shape[1] // 3
    per_res = tm // dil
    if dil > 1:
        for c in range(d // V7X_LANES):
            xs_ref[c] = x_ref[:, c * V7X_LANES:(c + 1) * V7X_LANES]
        for res in range(dil):
            for c in range(d // V7X_LANES):
                xp_ref[res * per_res:(res + 1) * per_res, c * V7X_LANES:(c + 1) * V7X_LANES] = (
                    xs_ref[c, pl.ds(res, per_res, stride=dil), :])
        x = xp_ref[...]
    else:
        x = x_ref[...]
    xn = _rms_norm(x, g_ref[...]).astype(BF16)
    cos, sin = cos_ref[...], sin_ref[...]

    def store(col0, val):
        for res in range(dil):
            o_ref[res, :, col0:col0 + val.shape[1]] = val[res * per_res:(res + 1) * per_res, :]

    for kind, gain_ref in enumerate((qg_ref, kg_ref, None)):
        y = jnp.dot(xn, w_ref[:, kind * hd:(kind + 1) * hd], preferred_element_type=F32)
        if gain_ref is None:
            store(kind * hd, y)
            continue
        gain = gain_ref[...]
        for h in range(hd // HEAD_DIM):
            yn = _rms_norm(y[:, h * HEAD_DIM:(h + 1) * HEAD_DIM], gain)
            store(kind * hd + h * HEAD_DIM, yn * cos + pltpu.roll(yn, HEAD_DIM // 2, axis=1) * sin)

    if kv_ref is None:
        return

    @pl.when(pl.program_id(0) % tiles >= tail_tile)
    def _():
        rows = kv_ref.shape[0]
        if dil == 1:
            kv_ref[...] = o_ref[0, tm - rows:, hd:]
            return
        slabs = xs_ref.shape[0]
        for half in range(2 * hd // (slabs * V7X_LANES)):
            col0 = hd + half * slabs * V7X_LANES
            for res in range(dil):
                for c in range(slabs):
                    xs_ref[c, pl.ds(res, per_res, stride=dil), :] = (
                        o_ref[res, :, col0 + c * V7X_LANES:col0 + (c + 1) * V7X_LANES])
            for c in range(slabs):
                kv_ref[:, col0 - hd + c * V7X_LANES:col0 - hd + (c + 1) * V7X_LANES] = xs_ref[c]


def qkv_project(x, g, w_qkv, q_gain, k_gain, pos, group, *, dil, batch, tm, keep=0):
    n, d = x.shape
    seq = n // batch
    hd3 = w_qkv.shape[1] // N_GROUPS
    tiles = seq // tm
    per_res = tm // dil
    assert per_res % V7X_SUBLANES == 0
    out_specs = [pl.BlockSpec((None, dil, per_res, hd3), lambda i: (i // tiles, 0, i % tiles, 0))]
    out_shape = [jax.ShapeDtypeStruct((batch, dil, seq // dil, hd3), F32)]
    tail_tile = None
    if keep:
        kv_rows = min(keep, tm)
        assert keep % kv_rows == 0 and (dil == 1 or kv_rows == tm)
        tail_tile = tiles - keep // kv_rows
        out_specs.append(pl.BlockSpec((None, kv_rows, 2 * hd3 // 3),
                                      lambda i: (i // tiles, jnp.maximum(i % tiles - tail_tile, 0), 0)))
        out_shape.append(jax.ShapeDtypeStruct((batch, keep, 2 * hd3 // 3), F32))
    order = (np.arange(tiles)[:, None] * tm + _residue_order(tm, dil)[None, :]).reshape(seq)
    cos, sin = _rope_tables(jnp.asarray(np.asarray(pos)[order]))
    gain = pl.BlockSpec((None, 1, HEAD_DIM), lambda i: (group, 0, 0))
    rope = pl.BlockSpec((tm, HEAD_DIM), lambda i: (i % tiles, 0))
    outs = pl.pallas_call(
        functools.partial(_qkv_kernel, dil=dil, tiles=tiles, tail_tile=tail_tile),
        grid=(n // tm,),
        in_specs=[
            pl.BlockSpec((tm, d), lambda i: (i, 0)),
            pl.BlockSpec((1, d), lambda i: (0, 0)),
            pl.BlockSpec((d, hd3), lambda i: (0, group), pipeline_mode=pl.Buffered(1)),
            gain, gain, rope, rope,
        ],
        out_specs=out_specs,
        out_shape=out_shape,
        scratch_shapes=[pltpu.VMEM((d // V7X_LANES, tm, V7X_LANES), F32), pltpu.VMEM((tm, d), F32)],
        compiler_params=_params("arbitrary"),
        name=f"qkv_project_g{group}_d{dil}",
    )(x, g, w_qkv, q_gain.reshape(N_GROUPS, 1, HEAD_DIM), k_gain.reshape(N_GROUPS, 1, HEAD_DIM), cos, sin)
    return outs if keep else outs[0]


def _spread_heads(cols_list, width):
    rows = cols_list[0].shape[0]
    lane = lax.broadcasted_iota(jnp.int32, (rows, width), 1)
    out = jnp.zeros((rows, width), F32)
    for h, col in enumerate(cols_list):
        out = jnp.where(lane == h, col, out)
    return out


def _attn_prompt_kernel(q_ref, kc_ref, kp_ref, vc_ref, vp_ref, o_ref, lse_ref, *, window_blocks):
    i = pl.program_id(2)
    qb = q_ref.shape[0]
    qi = lax.broadcasted_iota(jnp.int32, (qb, 2 * qb), 0)
    kj = lax.broadcasted_iota(jnp.int32, (qb, 2 * qb), 1)
    live = (kj <= qi + qb) & (kj >= qi + qb - window_blocks) & ((i > 0) | (kj >= qb))
    lses = []
    for h in range(q_ref.shape[1] // HEAD_DIM):
        cols = slice(h * HEAD_DIM, (h + 1) * HEAD_DIM)
        q = q_ref[:, cols].astype(BF16)
        k = jnp.concatenate([kp_ref[:, cols], kc_ref[:, cols]], axis=0).astype(BF16)
        v = jnp.concatenate([vp_ref[:, cols], vc_ref[:, cols]], axis=0).astype(BF16)
        s = lax.dot_general(q, k, NT_DIMS, preferred_element_type=F32) * ATTN_SCALE
        s = jnp.where(live, s, NEG_INF)
        m = jnp.max(s, axis=-1, keepdims=True)
        p = jnp.exp(s - m)
        l = jnp.sum(p, axis=-1, keepdims=True)
        o_ref[:, cols] = jnp.dot(p.astype(BF16), v, preferred_element_type=F32) / l
        lses.append(m + jnp.log(l))
    lse_ref[...] = _spread_heads(lses, lse_ref.shape[1])


def attn_prompt_group(qkv, group):
    batch, dil, n_strided, hd3 = qkv.shape
    hd = hd3 // 3
    win = WINDOWS[group]
    assert dil == DILATIONS[group] and n_strided % ATTN_BLOCK == 0 and win // dil <= ATTN_BLOCK
    blk = (None, None, ATTN_BLOCK, hd)

    def cur(kind):
        return pl.BlockSpec(blk, lambda b, r, i: (b, r, i, kind))

    def prev(kind):
        return pl.BlockSpec(blk, lambda b, r, i: (b, r, jnp.maximum(i - 1, 0), kind))

    return pl.pallas_call(
        functools.partial(_attn_prompt_kernel, window_blocks=win // dil),
        grid=(batch, dil, n_strided // ATTN_BLOCK),
        in_specs=[cur(0), cur(1), prev(1), cur(2), prev(2)],
        out_specs=[pl.BlockSpec(blk, lambda b, r, i: (b, r, i, 0)),
                   pl.BlockSpec((None, None, ATTN_BLOCK, V7X_LANES), lambda b, r, i: (b, r, i, 0))],
        out_shape=[jax.ShapeDtypeStruct((batch, dil, n_strided, hd), F32),
                   jax.ShapeDtypeStruct((batch, dil, n_strided, V7X_LANES), F32)],
        compiler_params=_params("arbitrary", "arbitrary", "arbitrary"),
        name=f"attn_prompt_g{group}",
    )(qkv, qkv, qkv, qkv, qkv)


def _head_segments(n_heads, transpose):
    shape = (V7X_LANES, n_heads * HEAD_DIM) if transpose else (n_heads * HEAD_DIM, V7X_LANES)
    c = lax.broadcasted_iota(jnp.int32, shape, 1 if transpose else 0)
    h = lax.broadcasted_iota(jnp.int32, shape, 0 if transpose else 1)
    return jnp.where((c >= h * HEAD_DIM) & (c < (h + 1) * HEAD_DIM), 1.0, 0.0).astype(BF16)


def _attn_sample_kernel(q0_ref, q1_ref, q2_ref, c0_ref, c1_ref, c2_ref, o_ref, lse_ref):
    t = pl.program_id(1)
    hd = o_ref.shape[2]
    n_heads = hd // HEAD_DIM
    t_new = q0_ref.shape[0]
    seg = _head_segments(n_heads, transpose=False)
    seg_t = _head_segments(n_heads, transpose=True)
    row_c = lax.broadcasted_iota(jnp.int32, (c0_ref.shape[0], V7X_LANES), 0)
    row_n = lax.broadcasted_iota(jnp.int32, (t_new, V7X_LANES), 0)
    for g, (qkv_ref, cache_ref) in enumerate(zip((q0_ref, q1_ref, q2_ref), (c0_ref, c1_ref, c2_ref))):
        q = qkv_ref[pl.ds(t, 1), 0:hd]
        k_new = qkv_ref[:, hd:2 * hd]
        v_new = qkv_ref[:, 2 * hd:3 * hd]
        k_old, v_old = cache_ref[:, 0:hd], cache_ref[:, hd:2 * hd]
        s_old = _split_dot(k_old * q, seg) * ATTN_SCALE
        s_new = _split_dot(k_new * q, seg) * ATTN_SCALE
        if DILATIONS[g] == 1:
            s_old = jnp.where(row_c >= t, s_old, NEG_INF)
            s_new = jnp.where(row_n <= t, s_new, NEG_INF)
        else:
            s_new = jnp.where(row_n == t, s_new, NEG_INF)
        m = jnp.maximum(jnp.max(s_old, axis=0, keepdims=True), jnp.max(s_new, axis=0, keepdims=True))
        p_old, p_new = jnp.exp(s_old - m), jnp.exp(s_new - m)
        l = jnp.sum(p_old, axis=0, keepdims=True) + jnp.sum(p_new, axis=0, keepdims=True)
        w_old = _split_dot(p_old / l, seg_t)
        w_new = _split_dot(p_new / l, seg_t)
        o = jnp.sum(w_old * v_old, axis=0, keepdims=True) + jnp.sum(w_new * v_new, axis=0, keepdims=True)
        o_ref[g, pl.ds(t, 1), :] = o
        lse_ref[g, pl.ds(t, 1), :] = m + jnp.log(l)


def attn_sample(qkv, caches, *, batch, t_new):
    cols = qkv[0].shape[1]
    hd = cols // 3
    views, specs = [], []
    for g, cache in enumerate(caches):
        dil, keep = DILATIONS[g], cache.shape[1]
        assert keep == WINDOWS[g] and keep // dil == ATTN_BLOCK and (dil == 1 or t_new <= dil)
        if dil == 1:
            views.append(cache.reshape(batch, keep, 2 * hd))
            specs.append(pl.BlockSpec((None, ATTN_BLOCK, 2 * hd), lambda b, t: (b, 0, 0)))
        else:
            by_res = jnp.stack([cache[:, t::dil] for t in range(t_new)], axis=1)
            views.append(by_res.reshape(batch, t_new, keep // dil, 2 * hd))
            specs.append(pl.BlockSpec((None, None, ATTN_BLOCK, 2 * hd), lambda b, t: (b, t, 0, 0)))
    rows_new = -(-t_new // V7X_SUBLANES) * V7X_SUBLANES
    qkv_rows = [jnp.pad(a.reshape(batch, t_new, cols), ((0, 0), (0, rows_new - t_new), (0, 0))) for a in qkv]
    o, lse = pl.pallas_call(
        _attn_sample_kernel,
        grid=(batch, t_new),
        in_specs=[pl.BlockSpec((None, rows_new, cols), lambda b, t: (b, 0, 0))] * N_GROUPS + specs,
        out_specs=[pl.BlockSpec((N_GROUPS, None, t_new, hd), lambda b, t: (0, b, 0, 0)),
                   pl.BlockSpec((N_GROUPS, None, t_new, V7X_LANES), lambda b, t: (0, b, 0, 0))],
        out_shape=[jax.ShapeDtypeStruct((N_GROUPS, batch, t_new, hd), F32),
                   jax.ShapeDtypeStruct((N_GROUPS, batch, t_new, V7X_LANES), F32)],
        compiler_params=_params("arbitrary", "arbitrary"),
        name="attn_sample",
    )(*qkv_rows, *views)
    return o.reshape(N_GROUPS, batch * t_new, hd), lse.reshape(N_GROUPS, batch * t_new, V7X_LANES)


def _attn_out_kernel(x_ref, o0_ref, o1_ref, o2_ref, l0_ref, l1_ref, l2_ref, wo_ref, y_ref,
                     og_ref, lg_ref, mix_ref):
    tm = x_ref.shape[0]
    hd = wo_ref.shape[0]
    heads = hd // HEAD_DIM
    for g, (o_ref, l_ref) in enumerate(zip((o0_ref, o1_ref, o2_ref), (l0_ref, l1_ref, l2_ref))):
        dil = o_ref.shape[0]
        for res in range(dil):
            rows = pl.ds(res, tm // dil, stride=dil) if dil > 1 else slice(0, tm)
            lg_ref[g, rows, :] = l_ref[res]
            for h in range(heads):
                og_ref[g, h, rows, :] = o_ref[res, :, h * HEAD_DIM:(h + 1) * HEAD_DIM]
    lse = [lg_ref[g] for g in range(N_GROUPS)]
    m = functools.reduce(jnp.maximum, lse)
    e = [jnp.exp(l - m) for l in lse]
    tot = functools.reduce(jnp.add, e)
    seg_t = _head_segments(heads, transpose=True)
    w = [_split_dot(e[g] / tot, seg_t) for g in range(N_GROUPS)]
    for h in range(heads):
        cols = slice(h * HEAD_DIM, (h + 1) * HEAD_DIM)
        mix = functools.reduce(jnp.add, [w[g][:, cols] * og_ref[g, h] for g in range(N_GROUPS)])
        mix_ref[:, cols] = mix.astype(BF16)
    y_ref[...] = x_ref[...] + jnp.dot(mix_ref[...], wo_ref[...], preferred_element_type=F32)


def attn_out(x, outs, lses, w_o, *, tm):
    n, d = x.shape
    batch = outs[0].shape[0]
    hd = outs[0].shape[3]
    tiles = n // batch // tm

    def by_residue(a):
        dil = a.shape[1]
        return pl.BlockSpec((None, dil, tm // dil, a.shape[3]), lambda i: (i // tiles, 0, i % tiles, 0))

    return pl.pallas_call(
        _attn_out_kernel,
        grid=(n // tm,),
        in_specs=[pl.BlockSpec((tm, d), lambda i: (i, 0))] + [by_residue(a) for a in (*outs, *lses)]
        + [pl.BlockSpec((hd, d), lambda i: (0, 0))],
        out_specs=pl.BlockSpec((tm, d), lambda i: (i, 0)),
        out_shape=jax.ShapeDtypeStruct((n, d), F32),
        scratch_shapes=[pltpu.VMEM((N_GROUPS, hd // HEAD_DIM, tm, HEAD_DIM), F32),
                        pltpu.VMEM((N_GROUPS, tm, V7X_LANES), F32), pltpu.VMEM((tm, hd), BF16)],
        compiler_params=_params("arbitrary"),
        name="attn_out",
    )(x, *outs, *lses, w_o)


def _rope_tables(pos):
    half = HEAD_DIM // 2
    inv_freq = ROPE_THETA ** (-jnp.arange(half, dtype=F32) / half)
    ang = pos.astype(F32)[:, None] * inv_freq
    cos, sin = jnp.cos(ang), jnp.sin(ang)
    return jnp.concatenate([cos, cos], axis=-1), jnp.concatenate([-sin, sin], axis=-1)


def _pad_rows(a, rows):
    return jnp.pad(a, ((0, rows - a.shape[0]),) + ((0, 0),) * (a.ndim - 1))


PROMPT_TM = 512
ATTN_OUT_TM = 256
QKV_TM = 256
PEER_TE = 1024
CONV_TN = 512
SAMPLE_PEER_ROWS = V7X_LANES


def kernel(x_prompt, x_sample, state_conv, cache_kv_w128, cache_kv_w512, cache_kv_w2048,
           norm_mix, norm_ffn, conv_w_in, conv_w, conv_w_out,
           attn_w_qkv, attn_q_norm, attn_k_norm, attn_w_o,
           peer_w_q, peer_sub_keys, peer_u, peer_v):
    batch, seq, d = x_prompt.shape
    dec_batch, dec_seq, _ = x_sample.shape
    n_p, n_s = batch * seq, dec_batch * dec_seq
    caches = (cache_kv_w128, cache_kv_w512, cache_kv_w2048)
    n_heads = caches[0].shape[3]
    hd = n_heads * HEAD_DIM

    w_in, w_out = conv_w_in.astype(BF16), conv_w_out.astype(BF16)
    w_qkv, w_o = attn_w_qkv.astype(BF16), attn_w_o.astype(BF16)
    w_q = peer_w_q.astype(BF16)
    keys = peer_sub_keys.astype(BF16).reshape(peer_sub_keys.shape[0], 2 * PEER_HEADS, PEER_N_KEYS, PEER_D_HALF)
    pu, pv = peer_u.astype(BF16), peer_v.astype(BF16)
    g_mix = norm_mix.reshape(-1, 1, d)
    g_ffn = norm_ffn.reshape(-1, 1, d)

    xp = x_prompt.reshape(n_p, d)
    xs = x_sample.reshape(n_s, d)

    def peer(x, layer, tm):
        return peer_layer(x, g_ffn[layer], w_q, keys, pu, pv, layer, tm=tm, te=PEER_TE)

    def peer_sample(x, layer):
        return peer(_pad_rows(x, SAMPLE_PEER_ROWS), layer, SAMPLE_PEER_ROWS)[:n_s]

    bp, up = conv_in(xp, g_mix[0], w_in, tm=PROMPT_TM, tn=CONV_TN)
    xp = conv_out_prompt(xp, bp, up, jnp.zeros((batch, CONV_WIDTH - 1, d), F32), conv_w, w_out,
                         seq_len=seq, tm=PROMPT_TM)
    conv_state_prompt = up.reshape(batch, seq, d)[:, seq - (CONV_WIDTH - 1):]

    bs, us = conv_in(xs, g_mix[0], w_in, tm=n_s, tn=CONV_TN)
    u_ext = jnp.concatenate([state_conv, us.reshape(dec_batch, dec_seq, d)], axis=1)
    shifted = [u_ext[:, k:k + dec_seq].reshape(n_s, d) for k in range(CONV_WIDTH)]
    xs = conv_out_sample(xs, bs, shifted[0], shifted[1], shifted[2], conv_w, w_out)
    conv_state_sample = u_ext[:, dec_seq:]

    xp = peer(xp, 0, PROMPT_TM)
    xs = peer_sample(xs, 0)

    qkv_p, kv_prompt = zip(*[
        qkv_project(xp, g_mix[1], w_qkv, attn_q_norm, attn_k_norm, np.arange(seq), g,
                    dil=DILATIONS[g], batch=batch, tm=QKV_TM, keep=min(WINDOWS[g], seq)) for g in range(N_GROUPS)])
    outs, lses = zip(*[attn_prompt_group(qkv_p[g], g) for g in range(N_GROUPS)])
    xp = attn_out(xp, outs, lses, w_o, tm=ATTN_OUT_TM)

    pos_s = np.tile(PAST_LEN + np.arange(dec_seq), dec_batch)
    qkv_s = [qkv_project(xs, g_mix[1], w_qkv, attn_q_norm, attn_k_norm, pos_s, g, dil=1, batch=1, tm=n_s)
             .reshape(n_s, 3 * hd) for g in range(N_GROUPS)]
    o_s, lse_s = attn_sample(qkv_s, caches, batch=dec_batch, t_new=dec_seq)
    xs = attn_out(xs, [o_s[g].reshape(1, 1, n_s, hd) for g in range(N_GROUPS)],
                  [lse_s[g].reshape(1, 1, n_s, V7X_LANES) for g in range(N_GROUPS)], w_o, tm=n_s)

    xp = peer(xp, 1, PROMPT_TM)
    xs = peer_sample(xs, 1)

    kv_prompt = [kv.reshape(batch, kv.shape[1], 2, n_heads, HEAD_DIM) for kv in kv_prompt]
    kv_sample = [jnp.concatenate([caches[g][:, dec_seq:],
                                  qkv_s[g][:, hd:].reshape(dec_batch, dec_seq, 2, n_heads, HEAD_DIM)], axis=1)
                 for g in range(N_GROUPS)]

    return (xp.reshape(batch, seq, d), xs.reshape(dec_batch, dec_seq, d),
            conv_state_prompt, conv_state_sample, *kv_prompt, *kv_sample)
```

```python
import functools
import math

import jax
import jax.numpy as jnp
import numpy as np
from jax import lax
from jax.experimental import pallas as pl
from jax.experimental.pallas import tpu as pltpu

F32 = jnp.float32
BF16 = jnp.bfloat16

V7X_LANES = 128
V7X_SUBLANES = 8
V7X_VMEM_BYTES = 64 * 1024 * 1024
VMEM_LIMIT_BYTES = V7X_VMEM_BYTES - 8 * 1024 * 1024

NORM_EPS = 1e-6
CONV_WIDTH = 3
WINDOWS = (128, 512, 2048)
DILATIONS = (1, 4, 16)
N_GROUPS = 3
HEAD_DIM = 128
ATTN_BLOCK = 128
ATTN_SCALE = HEAD_DIM ** -0.5
ROPE_THETA = 10000.0
PAST_LEN = 16384
assert PAST_LEN >= max(WINDOWS)
PEER_HEADS = 8
PEER_N_KEYS = 128
PEER_TOPK = 16
PEER_D_HALF = 128
SQRT_HALF = math.sqrt(0.5)
RANK_MASK = 31
NOT_SELECTED = float(RANK_MASK)
NEG_INF = float("-inf")

NT_DIMS = (((1,), (1,)), ((), ()))
TN_DIMS = (((0,), (0,)), ((), ()))


def _params(*semantics):
    return pltpu.CompilerParams(dimension_semantics=semantics, vmem_limit_bytes=VMEM_LIMIT_BYTES)


def _rms_norm(x, g):
    return x * lax.rsqrt(jnp.mean(x * x, axis=-1, keepdims=True) + NORM_EPS) * g


def _split_dot(a, b01):
    hi = a.astype(BF16)
    lo = (a - hi.astype(F32)).astype(BF16)
    return (jnp.dot(hi, b01, preferred_element_type=F32)
            + jnp.dot(lo, b01, preferred_element_type=F32))


def _conv_in_kernel(x_ref, g_ref, wb_ref, wc_ref, wh_ref, b_ref, u_ref, xn_ref):
    @pl.when(pl.program_id(1) == 0)
    def _():
        xn_ref[...] = _rms_norm(x_ref[...], g_ref[...]).astype(BF16)

    xn = xn_ref[...]
    b_ref[...] = jnp.dot(xn, wb_ref[...], preferred_element_type=F32)
    c = jnp.dot(xn, wc_ref[...], preferred_element_type=F32)
    h = jnp.dot(xn, wh_ref[...], preferred_element_type=F32)
    u_ref[...] = c * h


def conv_in(x, g, w_in, *, tm, tn):
    n, d = x.shape
    nj = d // tn
    return pl.pallas_call(
        _conv_in_kernel,
        grid=(n // tm, nj),
        in_specs=[
            pl.BlockSpec((tm, d), lambda i, j: (i, 0)),
            pl.BlockSpec((1, d), lambda i, j: (0, 0)),
            pl.BlockSpec((d, tn), lambda i, j: (0, j)),
            pl.BlockSpec((d, tn), lambda i, j: (0, j + nj)),
            pl.BlockSpec((d, tn), lambda i, j: (0, j + 2 * nj)),
        ],
        out_specs=[pl.BlockSpec((tm, tn), lambda i, j: (i, j)),
                   pl.BlockSpec((tm, tn), lambda i, j: (i, j))],
        out_shape=[jax.ShapeDtypeStruct((n, d), F32), jax.ShapeDtypeStruct((n, d), F32)],
        scratch_shapes=[pltpu.VMEM((tm, d), BF16)],
        compiler_params=_params("arbitrary", "arbitrary"),
        name="conv_in",
    )(x, g, w_in, w_in, w_in)


def _conv_mix(cw_ref, um2, um1, u0):
    return cw_ref[0:1, :] * um2 + cw_ref[1:2, :] * um1 + cw_ref[2:3, :] * u0


def _conv_out_prompt_kernel(x_ref, b_ref, u_ref, up_ref, st_ref, cw_ref, wo_ref, o_ref, *, tiles_per_seq):
    i = pl.program_id(0)
    u = u_ref[...]
    first = (i % tiles_per_seq) == 0
    prev = jnp.where(first, st_ref[...], up_ref[V7X_SUBLANES - 2:V7X_SUBLANES, :])
    row = lax.broadcasted_iota(jnp.int32, (u.shape[0], 1), 0)
    um1 = jnp.where(row == 0, prev[1:2, :], pltpu.roll(u, 1, axis=0))
    um2 = jnp.where(row == 0, prev[0:1, :], jnp.where(row == 1, prev[1:2, :], pltpu.roll(u, 2, axis=0)))
    a = (b_ref[...] * _conv_mix(cw_ref, um2, um1, u)).astype(BF16)
    o_ref[...] = x_ref[...] + jnp.dot(a, wo_ref[...], preferred_element_type=F32)


def conv_out_prompt(x, bgate, u, state, conv_w, w_out, *, seq_len, tm):
    n, d = x.shape
    tps = seq_len // tm
    sub_per_tile = tm // V7X_SUBLANES
    tile = pl.BlockSpec((tm, d), lambda i: (i, 0))
    return pl.pallas_call(
        functools.partial(_conv_out_prompt_kernel, tiles_per_seq=tps),
        grid=(n // tm,),
        in_specs=[
            tile, tile, tile,
            pl.BlockSpec((V7X_SUBLANES, d), lambda i: (jnp.maximum(i * sub_per_tile - 1, 0), 0)),
            pl.BlockSpec((None, CONV_WIDTH - 1, d), lambda i: (i // tps, 0, 0)),
            pl.BlockSpec((CONV_WIDTH, d), lambda i: (0, 0)),
            pl.BlockSpec((d, d), lambda i: (0, 0)),
        ],
        out_specs=tile,
        out_shape=jax.ShapeDtypeStruct((n, d), F32),
        compiler_params=_params("arbitrary"),
        name="conv_out_prompt",
    )(x, bgate, u, u, state, conv_w, w_out)


def _conv_out_sample_kernel(x_ref, b_ref, um2_ref, um1_ref, u_ref, cw_ref, wo_ref, o_ref):
    a = (b_ref[...] * _conv_mix(cw_ref, um2_ref[...], um1_ref[...], u_ref[...])).astype(BF16)
    o_ref[...] = x_ref[...] + jnp.dot(a, wo_ref[...], preferred_element_type=F32)


def conv_out_sample(x, bgate, um2, um1, u0, conv_w, w_out):
    n, d = x.shape
    full = pl.BlockSpec((n, d), lambda i: (0, 0))
    return pl.pallas_call(
        _conv_out_sample_kernel,
        grid=(1,),
        in_specs=[full, full, full, full, full,
                  pl.BlockSpec((CONV_WIDTH, d), lambda i: (0, 0)),
                  pl.BlockSpec((d, d), lambda i: (0, 0))],
        out_specs=full,
        out_shape=jax.ShapeDtypeStruct((n, d), F32),
        compiler_params=_params("arbitrary"),
        name="conv_out_sample",
    )(x, bgate, um2, um1, u0, conv_w, w_out)


CAND_ROWS = 72


def _cand_flat_index():
    pairs = []
    for b in range(4):
        pairs += [(a, b) for a in range(16 if b == 0 else 8)]
    pairs += [(0, b) if b >= 4 else None for b in range(16)]
    pairs += [(1, b) if b >= 4 else None for b in range(8)]
    pairs += [(2, b) if b == 4 else None for b in range(8)]
    rows = [a * PEER_TOPK + b if pair is not None and (a + 1) * (b + 1) <= PEER_TOPK else 256 + i
            for i, pair in enumerate(pairs) for a, b in [pair or (99, 99)]]
    assert len(rows) == CAND_ROWS and len({r for r in rows if r < 256}) == 50
    return jnp.broadcast_to(jnp.asarray(rows, F32).reshape(CAND_ROWS, 1), (CAND_ROWS, V7X_LANES))


def _top16(s, key_iota, exact_ties):
    work = s
    rank = jnp.full(s.shape, NOT_SELECTED, F32)
    val_row = lax.broadcasted_iota(jnp.int32, (PEER_TOPK, s.shape[1]), 0)
    vals = jnp.zeros((PEER_TOPK, s.shape[1]), F32)
    for a in range(PEER_TOPK):
        m = jnp.max(work, axis=0, keepdims=True)
        sel = work == m
        if exact_ties:
            first = jnp.min(jnp.where(sel, key_iota, float(PEER_N_KEYS)), axis=0, keepdims=True)
            sel = key_iota == first
        rank = jnp.where(sel, float(a), rank)
        work = jnp.where(sel, NEG_INF, work)
        vals = jnp.where(val_row == a, m, vals)
    return vals, rank


def _ranked_rows(rank):
    return jnp.sum(jnp.where(rank < NOT_SELECTED, 1.0, 0.0), axis=0, keepdims=True)


def _peer_select(s1, s2, fidx, exact_ties):
    lanes = s1.shape[1]
    key_iota = lax.broadcasted_iota(jnp.int32, (PEER_N_KEYS, lanes), 0).astype(F32)
    v1, rank1 = _top16(s1, key_iota, exact_ties)
    v2, rank2 = _top16(s2, key_iota, exact_ties)

    r16 = lax.broadcasted_iota(jnp.int32, (16, lanes), 0)
    r8 = lax.broadcasted_iota(jnp.int32, (8, lanes), 0)
    v1_lo, v2_lo = v1[0:8, :], v2[0:8, :]
    cand = jnp.concatenate([
        v1 + v2[0:1, :],
        v1_lo + v2[1:2, :],
        jnp.where(r8 < 5, v1_lo + v2[2:3, :], NEG_INF),
        jnp.where(r8 < 4, v1_lo + v2[3:4, :], NEG_INF),
        jnp.where(r16 >= 4, v1[0:1, :] + v2, NEG_INF),
        jnp.where(r8 >= 4, v1[1:2, :] + v2_lo, NEG_INF),
        jnp.where(r8 == 4, v1[2:3, :] + v2_lo, NEG_INF),
    ], axis=0)
    c_top = v1[0:1, :] + v2[0:1, :]
    taken = jnp.zeros(cand.shape, F32)
    z = jnp.zeros((1, lanes), F32)
    for _ in range(PEER_TOPK):
        m = jnp.max(cand, axis=0, keepdims=True)
        sel = cand == m
        if exact_ties:
            first = jnp.min(jnp.where(sel, fidx, 1024.0), axis=0, keepdims=True)
            sel = fidx == first
        taken = jnp.where(sel, 1.0, taken)
        cand = jnp.where(sel, NEG_INF, cand)
        z = z + jnp.exp(m - c_top)

    extra = jnp.where(r8 == 0, jnp.sum(taken[40:56, :], axis=0, keepdims=True),
                      jnp.where(r8 == 1, jnp.sum(taken[56:64, :], axis=0, keepdims=True),
                                jnp.where(r8 == 2, jnp.sum(taken[64:72, :], axis=0, keepdims=True), 0.0)))
    cnt_lo = taken[0:8, :] + taken[16:24, :] + taken[24:32, :] + taken[32:40, :] + extra
    cnt_vec = jnp.concatenate([cnt_lo, taken[8:16, :]], axis=0)
    cnt = jnp.zeros(s1.shape, F32)
    for a in range(PEER_TOPK):
        cnt = jnp.where(rank1 == float(a), cnt_vec[a:a + 1, :], cnt)

    e1n = jnp.exp(s1 - v1[0:1, :]) / z
    e2 = jnp.exp(s2 - v2[0:1, :])
    ranked = jnp.maximum(jnp.maximum(_ranked_rows(rank1), _ranked_rows(rank2)),
                         jnp.sum(taken, axis=0, keepdims=True))
    return rank2, e2, cnt, e1n, ranked


def _peer_query_kernel(x_ref, g_ref, wq_ref, keys_ref, fidx_ref,
                       xn_ref, e2r_ref, cnt_ref, e1n_ref, sc_ref):
    tm = x_ref.shape[0]
    xn = _rms_norm(x_ref[...], g_ref[...]).astype(BF16)
    xn_ref[...] = xn
    q = jnp.dot(xn, wq_ref[...], preferred_element_type=F32).astype(BF16)
    for rp in range(2 * PEER_HEADS):
        q_rp = q[:, rp * PEER_D_HALF:(rp + 1) * PEER_D_HALF]
        sc_ref[rp] = lax.dot_general(keys_ref[rp], q_rp, NT_DIMS, preferred_element_type=F32)

    fidx = fidx_ref[...]

    def select_head(r, exact_ties):
        ranked = jnp.zeros((1, V7X_LANES), F32)
        for s in range(tm // V7X_LANES):
            cols = slice(s * V7X_LANES, (s + 1) * V7X_LANES)
            rank2, e2, cnt, e1n, n = _peer_select(sc_ref[2 * r, :, cols], sc_ref[2 * r + 1, :, cols], fidx,
                                                  exact_ties)
            e2_bits = pltpu.bitcast(e2, jnp.int32) & ~RANK_MASK
            e2r_ref[s, r] = pltpu.bitcast(e2_bits | rank2.astype(jnp.int32), F32)
            cnt_ref[r, :, cols] = cnt
            e1n_ref[r, :, cols] = e1n
            ranked = jnp.maximum(ranked, n)
        return ranked

    def head_body(r, carry):
        ranked = select_head(r, exact_ties=False)

        @pl.when(jnp.max(ranked) > float(PEER_TOPK))
        def _():
            select_head(r, exact_ties=True)

        return carry

    lax.fori_loop(0, PEER_HEADS, head_body, 0)


def peer_query(x, g, w_q, keys, layer, *, tm):
    n, d = x.shape
    dq = w_q.shape[2]
    groups = tm // V7X_LANES
    by_i2 = jax.ShapeDtypeStruct((n // V7X_LANES, PEER_HEADS, PEER_N_KEYS, V7X_LANES), F32)
    by_i2_spec = pl.BlockSpec((groups, PEER_HEADS, PEER_N_KEYS, V7X_LANES), lambda i: (i, 0, 0, 0))
    sel = jax.ShapeDtypeStruct((PEER_HEADS, PEER_N_KEYS, n), F32)
    sel_spec = pl.BlockSpec((PEER_HEADS, PEER_N_KEYS, tm), lambda i: (0, 0, i))
    return pl.pallas_call(
        _peer_query_kernel,
        grid=(n // tm,),
        in_specs=[
            pl.BlockSpec((tm, d), lambda i: (i, 0)),
            pl.BlockSpec((1, d), lambda i: (0, 0)),
            pl.BlockSpec((None, d, dq), lambda i: (layer, 0, 0)),
            pl.BlockSpec((None, 2 * PEER_HEADS, PEER_N_KEYS, PEER_D_HALF), lambda i: (layer, 0, 0, 0)),
            pl.BlockSpec((CAND_ROWS, V7X_LANES), lambda i: (0, 0)),
        ],
        out_specs=[pl.BlockSpec((tm, d), lambda i: (i, 0)), by_i2_spec, sel_spec, sel_spec],
        out_shape=[jax.ShapeDtypeStruct((n, d), BF16), by_i2, sel, sel],
        scratch_shapes=[pltpu.VMEM((2 * PEER_HEADS, PEER_N_KEYS, tm), F32)],
        compiler_params=_params("arbitrary"),
        name="peer_query",
    )(x, g, w_q, keys, _cand_flat_index())


GATE_ROWS = 32


def _gate_chunk(h, a_ref, e2r_ref, cnt_ref, e1n_ref):
    te, tm = h.shape
    for k in range(te // PEER_N_KEYS):
        for s in range(tm // V7X_LANES):
            cols = slice(s * V7X_LANES, (s + 1) * V7X_LANES)
            cnt = [cnt_ref[r, k:k + 1, cols].astype(jnp.int32) for r in range(PEER_HEADS)]
            e1n = [e1n_ref[r, k:k + 1, cols] for r in range(PEER_HEADS)]
            for i2 in range(0, PEER_N_KEYS, GATE_ROWS):
                gate = jnp.zeros((GATE_ROWS, V7X_LANES), F32)
                for r in range(PEER_HEADS):
                    e2r = e2r_ref[s, r, i2:i2 + GATE_ROWS, :]
                    live = (pltpu.bitcast(e2r, jnp.int32) & RANK_MASK) < cnt[r]
                    gate = gate + jnp.where(live, e2r, 0.0) * e1n[r]
                row0 = k * PEER_N_KEYS + i2
                hp = h[row0:row0 + GATE_ROWS, cols]
                act = 0.5 * hp * (1.0 + lax.erf(hp * SQRT_HALF))
                a_ref[row0:row0 + GATE_ROWS, cols] = (gate * act).astype(BF16)


def _peer_expert_kernel(xn_ref, x_ref, e2r_ref, cnt_ref, e1n_ref, u_ref, v_ref, o_ref, a_ref):
    c = pl.program_id(1)

    @pl.when(c == 0)
    def _():
        o_ref[...] = x_ref[...]

    h = lax.dot_general(u_ref[...], xn_ref[...], NT_DIMS, preferred_element_type=F32)
    _gate_chunk(h, a_ref, e2r_ref, cnt_ref, e1n_ref)
    o_ref[...] += lax.dot_general(a_ref[...], v_ref[...], TN_DIMS, preferred_element_type=F32)


def peer_experts(xn, x, e2r, cnt, e1n, u, v, layer, *, tm, te):
    n, d = x.shape
    tok = pl.BlockSpec((tm, d), lambda i, c: (i, 0))
    groups = tm // V7X_LANES
    by_i2 = pl.BlockSpec((groups, PEER_HEADS, PEER_N_KEYS, V7X_LANES), lambda i, c: (i, 0, 0, 0))
    by_i1 = pl.BlockSpec((PEER_HEADS, te // PEER_N_KEYS, tm), lambda i, c: (0, c, i))
    wts = pl.BlockSpec((None, te, d), lambda i, c: (layer, c, 0))
    return pl.pallas_call(
        _peer_expert_kernel,
        grid=(n // tm, u.shape[1] // te),
        in_specs=[tok, tok, by_i2, by_i1, by_i1, wts, wts],
        out_specs=tok,
        out_shape=jax.ShapeDtypeStruct((n, d), F32),
        scratch_shapes=[pltpu.VMEM((te, tm), BF16)],
        compiler_params=_params("arbitrary", "arbitrary"),
        name="peer_experts",
    )(xn, x, e2r, cnt, e1n, u, v)


def peer_layer(x, g, w_q, keys, u, v, layer, *, tm, te):
    xn, e2r, cnt, e1n = peer_query(x, g, w_q, keys, layer, tm=tm)
    return peer_experts(xn, x, e2r, cnt, e1n, u, v, layer, tm=tm, te=te)


def _residue_order(tm, dil):
    return np.arange(tm).reshape(tm // dil, dil).T.reshape(tm)


def _qkv_kernel(x_ref, g_ref, w_ref, qg_ref, kg_ref, cos_ref, sin_ref, o_ref, kv_ref, xs_ref, xp_ref, kvf_ref,
                *, dil, tiles, tail_tile):
    tm, d = x_ref.shape
    hd = w_ref.shape[1] // 3
    per_res = tm // dil
    if dil > 1:
        for c in range(d // V7X_LANES):
            xs_ref[c] = x_ref[:, c * V7X_LANES:(c + 1) * V7X_LANES]
        for res in range(dil):
            for c in range(d // V7X_LANES):
                xp_ref[res * per_res:(res + 1) * per_res, c * V7X_LANES:(c + 1) * V7X_LANES] = (
                    xs_ref[c, pl.ds(res, per_res, stride=dil), :])
        x = xp_ref[...]
    else:
        x = x_ref[...]
    xn = _rms_norm(x, g_ref[...]).astype(BF16)
    cos, sin = cos_ref[...], sin_ref[...]

    def store(col0, val):
        for res in range(dil):
            piece = val[res * per_res:(res + 1) * per_res, :]
            o_ref[res, :, col0:col0 + val.shape[1]] = piece.astype(BF16)
            if col0 >= hd:
                kvf_ref[res, :, col0 - hd:col0 - hd + val.shape[1]] = piece

    for kind, gain_ref in enumerate((qg_ref, kg_ref, None)):
        y = jnp.dot(xn, w_ref[:, kind * hd:(kind + 1) * hd], preferred_element_type=F32)
        if gain_ref is None:
            store(kind * hd, y)
            continue
        gain = gain_ref[...]
        for h in range(hd // HEAD_DIM):
            yn = _rms_norm(y[:, h * HEAD_DIM:(h + 1) * HEAD_DIM], gain)
            store(kind * hd + h * HEAD_DIM, yn * cos + pltpu.roll(yn, HEAD_DIM // 2, axis=1) * sin)

    @pl.when(pl.program_id(0) % tiles >= tail_tile)
    def _():
        rows = kv_ref.shape[0]
        if dil == 1:
            kv_ref[...] = kvf_ref[0, tm - rows:, :]
            return
        slabs = xs_ref.shape[0]
        for half in range(2 * hd // (slabs * V7X_LANES)):
            col0 = half * slabs * V7X_LANES
            for res in range(dil):
                for c in range(slabs):
                    xs_ref[c, pl.ds(res, per_res, stride=dil), :] = (
                        kvf_ref[res, :, col0 + c * V7X_LANES:col0 + (c + 1) * V7X_LANES])
            for c in range(slabs):
                kv_ref[:, col0 + c * V7X_LANES:col0 + (c + 1) * V7X_LANES] = xs_ref[c]


def qkv_project(x, g, w_qkv, q_gain, k_gain, pos, group, *, dil, batch, tm, keep):
    n, d = x.shape
    seq = n // batch
    hd3 = w_qkv.shape[1] // N_GROUPS
    tiles = seq // tm
    per_res = tm // dil
    assert per_res % (2 * V7X_SUBLANES) == 0
    kv_rows = min(keep, tm)
    assert keep % kv_rows == 0 and (dil == 1 or kv_rows == tm)
    tail_tile = tiles - keep // kv_rows
    out_specs = [pl.BlockSpec((None, dil, per_res, hd3), lambda i: (i // tiles, 0, i % tiles, 0)),
                 pl.BlockSpec((None, kv_rows, 2 * hd3 // 3),
                              lambda i: (i // tiles, jnp.maximum(i % tiles - tail_tile, 0), 0))]
    out_shape = [jax.ShapeDtypeStruct((batch, dil, seq // dil, hd3), BF16),
                 jax.ShapeDtypeStruct((batch, keep, 2 * hd3 // 3), F32)]
    order = (np.arange(tiles)[:, None] * tm + _residue_order(tm, dil)[None, :]).reshape(seq)
    cos, sin = _rope_tables(jnp.asarray(np.asarray(pos)[order]))
    gain = pl.BlockSpec((None, 1, HEAD_DIM), lambda i: (group, 0, 0))
    rope = pl.BlockSpec((tm, HEAD_DIM), lambda i: (i % tiles, 0))
    return pl.pallas_call(
        functools.partial(_qkv_kernel, dil=dil, tiles=tiles, tail_tile=tail_tile),
        grid=(n // tm,),
        in_specs=[
            pl.BlockSpec((tm, d), lambda i: (i, 0)),
            pl.BlockSpec((1, d), lambda i: (0, 0)),
            pl.BlockSpec((d, hd3), lambda i: (0, group), pipeline_mode=pl.Buffered(1)),
            gain, gain, rope, rope,
        ],
        out_specs=out_specs,
        out_shape=out_shape,
        scratch_shapes=[pltpu.VMEM((d // V7X_LANES, tm, V7X_LANES), F32), pltpu.VMEM((tm, d), F32),
                        pltpu.VMEM((dil, per_res, 2 * hd3 // 3), F32)],
        compiler_params=_params("arbitrary"),
        name=f"qkv_project_g{group}_d{dil}",
    )(x, g, w_qkv, q_gain.reshape(N_GROUPS, 1, HEAD_DIM), k_gain.reshape(N_GROUPS, 1, HEAD_DIM), cos, sin)


def _spread_heads(cols_list, width):
    rows = cols_list[0].shape[0]
    lane = lax.broadcasted_iota(jnp.int32, (rows, width), 1)
    out = jnp.zeros((rows, width), F32)
    for h, col in enumerate(cols_list):
        out = jnp.where(lane == h, col, out)
    return out


def _attn_prompt_kernel(q_ref, kc_ref, kp_ref, vc_ref, vp_ref, o_ref, lse_ref, *, window_blocks):
    i = pl.program_id(2)
    qb = q_ref.shape[0]
    qi = lax.broadcasted_iota(jnp.int32, (qb, 2 * qb), 0)
    kj = lax.broadcasted_iota(jnp.int32, (qb, 2 * qb), 1)
    live = (kj <= qi + qb) & (kj >= qi + qb - window_blocks) & ((i > 0) | (kj >= qb))
    lses = []
    for h in range(q_ref.shape[1] // HEAD_DIM):
        cols = slice(h * HEAD_DIM, (h + 1) * HEAD_DIM)
        q = q_ref[:, cols].astype(BF16)
        k = jnp.concatenate([kp_ref[:, cols], kc_ref[:, cols]], axis=0).astype(BF16)
        v = jnp.concatenate([vp_ref[:, cols], vc_ref[:, cols]], axis=0).astype(BF16)
        s = lax.dot_general(q, k, NT_DIMS, preferred_element_type=F32) * ATTN_SCALE
        s = jnp.where(live, s, NEG_INF)
        m = jnp.max(s, axis=-1, keepdims=True)
        p = jnp.exp(s - m)
        l = jnp.sum(p, axis=-1, keepdims=True)
        o_ref[:, cols] = jnp.dot(p.astype(BF16), v, preferred_element_type=F32) / l
        lses.append(m + jnp.log(l))
    lse_ref[...] = _spread_heads(lses, lse_ref.shape[1])


def attn_prompt_group(qkv, group):
    batch, dil, n_strided, hd3 = qkv.shape
    hd = hd3 // 3
    win = WINDOWS[group]
    assert dil == DILATIONS[group] and n_strided % ATTN_BLOCK == 0 and win // dil <= ATTN_BLOCK
    blk = (None, None, ATTN_BLOCK, hd)

    def cur(kind):
        return pl.BlockSpec(blk, lambda b, r, i: (b, r, i, kind))

    def prev(kind):
        return pl.BlockSpec(blk, lambda b, r, i: (b, r, jnp.maximum(i - 1, 0), kind))

    return pl.pallas_call(
        functools.partial(_attn_prompt_kernel, window_blocks=win // dil),
        grid=(batch, dil, n_strided // ATTN_BLOCK),
        in_specs=[cur(0), cur(1), prev(1), cur(2), prev(2)],
        out_specs=[pl.BlockSpec(blk, lambda b, r, i: (b, r, i, 0)),
                   pl.BlockSpec((None, None, ATTN_BLOCK, V7X_LANES), lambda b, r, i: (b, r, i, 0))],
        out_shape=[jax.ShapeDtypeStruct((batch, dil, n_strided, hd), F32),
                   jax.ShapeDtypeStruct((batch, dil, n_strided, V7X_LANES), F32)],
        compiler_params=_params("arbitrary", "arbitrary", "arbitrary"),
        name=f"attn_prompt_g{group}",
    )(qkv, qkv, qkv, qkv, qkv)


def _head_segments(n_heads, transpose):
    shape = (V7X_LANES, n_heads * HEAD_DIM) if transpose else (n_heads * HEAD_DIM, V7X_LANES)
    c = lax.broadcasted_iota(jnp.int32, shape, 1 if transpose else 0)
    h = lax.broadcasted_iota(jnp.int32, shape, 0 if transpose else 1)
    return jnp.where((c >= h * HEAD_DIM) & (c < (h + 1) * HEAD_DIM), 1.0, 0.0).astype(BF16)


def _attn_sample_kernel(q0_ref, q1_ref, q2_ref, c0_ref, c1_ref, c2_ref, o_ref, lse_ref):
    t = pl.program_id(1)
    hd = o_ref.shape[2]
    n_heads = hd // HEAD_DIM
    t_new = q0_ref.shape[0]
    seg = _head_segments(n_heads, transpose=False)
    seg_t = _head_segments(n_heads, transpose=True)
    row_c = lax.broadcasted_iota(jnp.int32, (c0_ref.shape[0], V7X_LANES), 0)
    row_n = lax.broadcasted_iota(jnp.int32, (t_new, V7X_LANES), 0)
    for g, (qkv_ref, cache_ref) in enumerate(zip((q0_ref, q1_ref, q2_ref), (c0_ref, c1_ref, c2_ref))):
        q = qkv_ref[pl.ds(t, 1), 0:hd]
        k_new = qkv_ref[:, hd:2 * hd]
        v_new = qkv_ref[:, 2 * hd:3 * hd]
        k_old, v_old = cache_ref[:, 0:hd], cache_ref[:, hd:2 * hd]
        s_old = _split_dot(k_old * q, seg) * ATTN_SCALE
        s_new = _split_dot(k_new * q, seg) * ATTN_SCALE
        if DILATIONS[g] == 1:
            s_old = jnp.where(row_c >= t, s_old, NEG_INF)
            s_new = jnp.where(row_n <= t, s_new, NEG_INF)
        else:
            s_new = jnp.where(row_n == t, s_new, NEG_INF)
        m = jnp.maximum(jnp.max(s_old, axis=0, keepdims=True), jnp.max(s_new, axis=0, keepdims=True))
        p_old, p_new = jnp.exp(s_old - m), jnp.exp(s_new - m)
        l = jnp.sum(p_old, axis=0, keepdims=True) + jnp.sum(p_new, axis=0, keepdims=True)
        w_old = _split_dot(p_old / l, seg_t)
        w_new = _split_dot(p_new / l, seg_t)
        o = jnp.sum(w_old * v_old, axis=0, keepdims=True) + jnp.sum(w_new * v_new, axis=0, keepdims=True)
        o_ref[g, pl.ds(t, 1), :] = o
        lse_ref[g, pl.ds(t, 1), :] = m + jnp.log(l)


def attn_sample(qkv, caches, *, batch, t_new):
    cols = qkv[0].shape[1]
    hd = cols // 3
    views, specs = [], []
    for g, cache in enumerate(caches):
        dil, keep = DILATIONS[g], cache.shape[1]
        assert keep == WINDOWS[g] and keep // dil == ATTN_BLOCK and (dil == 1 or t_new <= dil)
        if dil == 1:
            views.append(cache.reshape(batch, keep, 2 * hd))
            specs.append(pl.BlockSpec((None, ATTN_BLOCK, 2 * hd), lambda b, t: (b, 0, 0)))
        else:
            by_res = jnp.stack([cache[:, t::dil] for t in range(t_new)], axis=1)
            views.append(by_res.reshape(batch, t_new, keep // dil, 2 * hd))
            specs.append(pl.BlockSpec((None, None, ATTN_BLOCK, 2 * hd), lambda b, t: (b, t, 0, 0)))
    rows_new = -(-t_new // V7X_SUBLANES) * V7X_SUBLANES
    qkv_rows = [jnp.pad(a.reshape(batch, t_new, cols), ((0, 0), (0, rows_new - t_new), (0, 0))) for a in qkv]
    o, lse = pl.pallas_call(
        _attn_sample_kernel,
        grid=(batch, t_new),
        in_specs=[pl.BlockSpec((None, rows_new, cols), lambda b, t: (b, 0, 0))] * N_GROUPS + specs,
        out_specs=[pl.BlockSpec((N_GROUPS, None, t_new, hd), lambda b, t: (0, b, 0, 0)),
                   pl.BlockSpec((N_GROUPS, None, t_new, V7X_LANES), lambda b, t: (0, b, 0, 0))],
        out_shape=[jax.ShapeDtypeStruct((N_GROUPS, batch, t_new, hd), F32),
                   jax.ShapeDtypeStruct((N_GROUPS, batch, t_new, V7X_LANES), F32)],
        compiler_params=_params("arbitrary", "arbitrary"),
        name="attn_sample",
    )(*qkv_rows, *views)
    return o.reshape(N_GROUPS, batch * t_new, hd), lse.reshape(N_GROUPS, batch * t_new, V7X_LANES)


def _attn_out_kernel(x_ref, o0_ref, o1_ref, o2_ref, l0_ref, l1_ref, l2_ref, wo_ref, y_ref,
                     og_ref, lg_ref, mix_ref):
    tm = x_ref.shape[0]
    hd = wo_ref.shape[0]
    heads = hd // HEAD_DIM
    for g, (o_ref, l_ref) in enumerate(zip((o0_ref, o1_ref, o2_ref), (l0_ref, l1_ref, l2_ref))):
        dil = o_ref.shape[0]
        for res in range(dil):
            rows = pl.ds(res, tm // dil, stride=dil) if dil > 1 else slice(0, tm)
            lg_ref[g, rows, :] = l_ref[res]
            for h in range(heads):
                og_ref[g, h, rows, :] = o_ref[res, :, h * HEAD_DIM:(h + 1) * HEAD_DIM]
    lse = [lg_ref[g] for g in range(N_GROUPS)]
    m = functools.reduce(jnp.maximum, lse)
    e = [jnp.exp(l - m) for l in lse]
    tot = functools.reduce(jnp.add, e)
    seg_t = _head_segments(heads, transpose=True)
    w = [_split_dot(e[g] / tot, seg_t) for g in range(N_GROUPS)]
    for h in range(heads):
        cols = slice(h * HEAD_DIM, (h + 1) * HEAD_DIM)
        mix = functools.reduce(jnp.add, [w[g][:, cols] * og_ref[g, h] for g in range(N_GROUPS)])
        mix_ref[:, cols] = mix.astype(BF16)
    y_ref[...] = x_ref[...] + jnp.dot(mix_ref[...], wo_ref[...], preferred_element_type=F32)


def attn_out(x, outs, lses, w_o, *, tm):
    n, d = x.shape
    batch = outs[0].shape[0]
    hd = outs[0].shape[3]
    tiles = n // batch // tm

    def by_residue(a):
        dil = a.shape[1]
        return pl.BlockSpec((None, dil, tm // dil, a.shape[3]), lambda i: (i // tiles, 0, i % tiles, 0))

    return pl.pallas_call(
        _attn_out_kernel,
        grid=(n // tm,),
        in_specs=[pl.BlockSpec((tm, d), lambda i: (i, 0))] + [by_residue(a) for a in (*outs, *lses)]
        + [pl.BlockSpec((hd, d), lambda i: (0, 0))],
        out_specs=pl.BlockSpec((tm, d), lambda i: (i, 0)),
        out_shape=jax.ShapeDtypeStruct((n, d), F32),
        scratch_shapes=[pltpu.VMEM((N_GROUPS, hd // HEAD_DIM, tm, HEAD_DIM), F32),
                        pltpu.VMEM((N_GROUPS, tm, V7X_LANES), F32), pltpu.VMEM((tm, hd), BF16)],
        compiler_params=_params("arbitrary"),
        name="attn_out",
    )(x, *outs, *lses, w_o)


def _kv_state_copies(refs, sems):
    n = len(refs) // 3
    copies = []
    for g in range(n):
        cache, new, out = refs[g], refs[n + g], refs[2 * n + g]
        keep, t_new = cache.shape[1], new.shape[1]
        copies.append(pltpu.make_async_copy(cache.at[:, pl.ds(t_new, keep - t_new)],
                                            out.at[:, pl.ds(0, keep - t_new)], sems.at[2 * g]))
        copies.append(pltpu.make_async_copy(new, out.at[:, pl.ds(keep - t_new, t_new)], sems.at[2 * g + 1]))
    return copies


def _kv_state_kernel(*refs):
    copies = _kv_state_copies(refs[:-1], refs[-1])
    for c in copies:
        c.start()
    for c in copies:
        c.wait()


def kv_state_update(caches, new_rows):
    n = len(caches)
    assert all(c.shape[1] > r.shape[1] and c.shape[2:] == r.shape[2:] for c, r in zip(caches, new_rows))
    any_space = pl.BlockSpec(memory_space=pl.ANY)
    return pl.pallas_call(
        _kv_state_kernel,
        in_specs=[any_space] * (2 * n),
        out_specs=[any_space] * n,
        out_shape=[jax.ShapeDtypeStruct(c.shape, c.dtype) for c in caches],
        scratch_shapes=[pltpu.SemaphoreType.DMA((2 * n,))],
        name="kv_state_update",
    )(*caches, *new_rows)


def _rope_tables(pos):
    half = HEAD_DIM // 2
    inv_freq = ROPE_THETA ** (-jnp.arange(half, dtype=F32) / half)
    ang = pos.astype(F32)[:, None] * inv_freq
    cos, sin = jnp.cos(ang), jnp.sin(ang)
    return jnp.concatenate([cos, cos], axis=-1), jnp.concatenate([-sin, sin], axis=-1)


def _pad_rows(a, rows):
    return jnp.pad(a, ((0, rows - a.shape[0]),) + ((0, 0),) * (a.ndim - 1))


PROMPT_TM = 512
ATTN_OUT_TM = 256
QKV_TM = 256
PEER_TE = 1024
CONV_TN = 512
SAMPLE_PEER_ROWS = V7X_LANES


def kernel(x_prompt, x_sample, state_conv, cache_kv_w128, cache_kv_w512, cache_kv_w2048,
           norm_mix, norm_ffn, conv_w_in, conv_w, conv_w_out,
           attn_w_qkv, attn_q_norm, attn_k_norm, attn_w_o,
           peer_w_q, peer_sub_keys, peer_u, peer_v):
    batch, seq, d = x_prompt.shape
    dec_batch, dec_seq, _ = x_sample.shape
    n_p, n_s = batch * seq, dec_batch * dec_seq
    caches = (cache_kv_w128, cache_kv_w512, cache_kv_w2048)
    n_heads = caches[0].shape[3]
    hd = n_heads * HEAD_DIM

    w_in, w_out = conv_w_in.astype(BF16), conv_w_out.astype(BF16)
    w_qkv, w_o = attn_w_qkv.astype(BF16), attn_w_o.astype(BF16)
    w_q = peer_w_q.astype(BF16)
    keys = peer_sub_keys.astype(BF16).reshape(peer_sub_keys.shape[0], 2 * PEER_HEADS, PEER_N_KEYS, PEER_D_HALF)
    pu, pv = peer_u.astype(BF16), peer_v.astype(BF16)
    g_mix = norm_mix.reshape(-1, 1, d)
    g_ffn = norm_ffn.reshape(-1, 1, d)

    xp = x_prompt.reshape(n_p, d)
    xs = x_sample.reshape(n_s, d)

    def peer(x, layer, tm):
        return peer_layer(x, g_ffn[layer], w_q, keys, pu, pv, layer, tm=tm, te=PEER_TE)

    def peer_sample(x, layer):
        return peer(_pad_rows(x, SAMPLE_PEER_ROWS), layer, SAMPLE_PEER_ROWS)[:n_s]

    bp, up = conv_in(xp, g_mix[0], w_in, tm=PROMPT_TM, tn=CONV_TN)
    xp = conv_out_prompt(xp, bp, up, jnp.zeros((batch, CONV_WIDTH - 1, d), F32), conv_w, w_out,
                         seq_len=seq, tm=PROMPT_TM)
    conv_state_prompt = up.reshape(batch, seq, d)[:, seq - (CONV_WIDTH - 1):]

    bs, us = conv_in(xs, g_mix[0], w_in, tm=n_s, tn=CONV_TN)
    u_ext = jnp.concatenate([state_conv, us.reshape(dec_batch, dec_seq, d)], axis=1)
    shifted = [u_ext[:, k:k + dec_seq].reshape(n_s, d) for k in range(CONV_WIDTH)]
    xs = conv_out_sample(xs, bs, shifted[0], shifted[1], shifted[2], conv_w, w_out)
    conv_state_sample = u_ext[:, dec_seq:]

    xp = peer(xp, 0, PROMPT_TM)
    xs = peer_sample(xs, 0)

    qkv_p, kv_prompt = zip(*[
        qkv_project(xp, g_mix[1], w_qkv, attn_q_norm, attn_k_norm, np.arange(seq), g,
                    dil=DILATIONS[g], batch=batch, tm=QKV_TM, keep=min(WINDOWS[g], seq)) for g in range(N_GROUPS)])
    outs, lses = zip(*[attn_prompt_group(qkv_p[g], g) for g in range(N_GROUPS)])
    xp = attn_out(xp, outs, lses, w_o, tm=ATTN_OUT_TM)

    pos_s = np.tile(PAST_LEN + np.arange(dec_seq), dec_batch)
    q_s, kv_s = zip(*[qkv_project(xs, g_mix[1], w_qkv, attn_q_norm, attn_k_norm, pos_s, g,
                                  dil=1, batch=1, tm=n_s, keep=n_s) for g in range(N_GROUPS)])
    kv_s = [kv.reshape(n_s, 2 * hd) for kv in kv_s]
    qkv_s = [jnp.concatenate([q.reshape(n_s, 3 * hd)[:, :hd].astype(F32), kv], axis=1) for q, kv in zip(q_s, kv_s)]
    o_s, lse_s = attn_sample(qkv_s, caches, batch=dec_batch, t_new=dec_seq)
    xs = attn_out(xs, [o_s[g].reshape(1, 1, n_s, hd) for g in range(N_GROUPS)],
                  [lse_s[g].reshape(1, 1, n_s, V7X_LANES) for g in range(N_GROUPS)], w_o, tm=n_s)

    xp = peer(xp, 1, PROMPT_TM)
    xs = peer_sample(xs, 1)

    kv_prompt = [kv.reshape(batch, kv.shape[1], 2, n_heads, HEAD_DIM) for kv in kv_prompt]
    kv_sample = kv_state_update(caches, [kv.reshape(dec_batch, dec_seq, 2, n_heads, HEAD_DIM) for kv in kv_s])

    return (xp.reshape(batch, seq, d), xs.reshape(dec_batch, dec_seq, d),
            conv_state_prompt, conv_state_sample, *kv_prompt, *kv_sample)
```

```python
import functools
import math

import jax
import jax.numpy as jnp
import numpy as np
from jax import lax
from jax.experimental import pallas as pl
from jax.experimental.pallas import tpu as pltpu

F32 = jnp.float32
BF16 = jnp.bfloat16

V7X_LANES = 128
V7X_SUBLANES = 8
V7X_VMEM_BYTES = 64 * 1024 * 1024
VMEM_LIMIT_BYTES = V7X_VMEM_BYTES - 8 * 1024 * 1024

NORM_EPS = 1e-6
CONV_WIDTH = 3
WINDOWS = (128, 512, 2048)
DILATIONS = (1, 4, 16)
N_GROUPS = 3
HEAD_DIM = 128
ATTN_BLOCK = 128
ATTN_SCALE = HEAD_DIM ** -0.5
ROPE_THETA = 10000.0
PAST_LEN = 16384
assert PAST_LEN >= max(WINDOWS)
PEER_HEADS = 8
PEER_N_KEYS = 128
PEER_TOPK = 16
PEER_D_HALF = 128
SQRT_HALF = math.sqrt(0.5)
RANK_MASK = 31
NOT_SELECTED = float(RANK_MASK)
NEG_INF = float("-inf")

NT_DIMS = (((1,), (1,)), ((), ()))
TN_DIMS = (((0,), (0,)), ((), ()))


def _params(*semantics):
    return pltpu.CompilerParams(dimension_semantics=semantics, vmem_limit_bytes=VMEM_LIMIT_BYTES)


def _rms_norm(x, g):
    return x * lax.rsqrt(jnp.mean(x * x, axis=-1, keepdims=True) + NORM_EPS) * g


def _split_dot(a, b01):
    hi = a.astype(BF16)
    lo = (a - hi.astype(F32)).astype(BF16)
    return (jnp.dot(hi, b01, preferred_element_type=F32)
            + jnp.dot(lo, b01, preferred_element_type=F32))


def _conv_in_kernel(x_ref, g_ref, wb_ref, wc_ref, wh_ref, b_ref, u_ref, xn_ref):
    @pl.when(pl.program_id(1) == 0)
    def _():
        xn_ref[...] = _rms_norm(x_ref[...], g_ref[...]).astype(BF16)

    xn = xn_ref[...]
    b_ref[...] = jnp.dot(xn, wb_ref[...], preferred_element_type=F32)
    c = jnp.dot(xn, wc_ref[...], preferred_element_type=F32)
    h = jnp.dot(xn, wh_ref[...], preferred_element_type=F32)
    u_ref[...] = c * h


def conv_in(x, g, w_in, *, tm, tn):
    n, d = x.shape
    nj = d // tn
    return pl.pallas_call(
        _conv_in_kernel,
        grid=(n // tm, nj),
        in_specs=[
            pl.BlockSpec((tm, d), lambda i, j: (i, 0)),
            pl.BlockSpec((1, d), lambda i, j: (0, 0)),
            pl.BlockSpec((d, tn), lambda i, j: (0, j)),
            pl.BlockSpec((d, tn), lambda i, j: (0, j + nj)),
            pl.BlockSpec((d, tn), lambda i, j: (0, j + 2 * nj)),
        ],
        out_specs=[pl.BlockSpec((tm, tn), lambda i, j: (i, j)),
                   pl.BlockSpec((tm, tn), lambda i, j: (i, j))],
        out_shape=[jax.ShapeDtypeStruct((n, d), F32), jax.ShapeDtypeStruct((n, d), F32)],
        scratch_shapes=[pltpu.VMEM((tm, d), BF16)],
        compiler_params=_params("arbitrary", "arbitrary"),
        name="conv_in",
    )(x, g, w_in, w_in, w_in)


def _conv_mix(cw_ref, um2, um1, u0):
    return cw_ref[0:1, :] * um2 + cw_ref[1:2, :] * um1 + cw_ref[2:3, :] * u0


def _conv_out_prompt_kernel(x_ref, b_ref, u_ref, up_ref, st_ref, cw_ref, wo_ref, o_ref, *, tiles_per_seq):
    i = pl.program_id(0)
    u = u_ref[...]
    first = (i % tiles_per_seq) == 0
    prev = jnp.where(first, st_ref[...], up_ref[V7X_SUBLANES - 2:V7X_SUBLANES, :])
    row = lax.broadcasted_iota(jnp.int32, (u.shape[0], 1), 0)
    um1 = jnp.where(row == 0, prev[1:2, :], pltpu.roll(u, 1, axis=0))
    um2 = jnp.where(row == 0, prev[0:1, :], jnp.where(row == 1, prev[1:2, :], pltpu.roll(u, 2, axis=0)))
    a = (b_ref[...] * _conv_mix(cw_ref, um2, um1, u)).astype(BF16)
    o_ref[...] = x_ref[...] + jnp.dot(a, wo_ref[...], preferred_element_type=F32)


def conv_out_prompt(x, bgate, u, state, conv_w, w_out, *, seq_len, tm):
    n, d = x.shape
    tps = seq_len // tm
    sub_per_tile = tm // V7X_SUBLANES
    tile = pl.BlockSpec((tm, d), lambda i: (i, 0))
    return pl.pallas_call(
        functools.partial(_conv_out_prompt_kernel, tiles_per_seq=tps),
        grid=(n // tm,),
        in_specs=[
            tile, tile, tile,
            pl.BlockSpec((V7X_SUBLANES, d), lambda i: (jnp.maximum(i * sub_per_tile - 1, 0), 0)),
            pl.BlockSpec((None, CONV_WIDTH - 1, d), lambda i: (i // tps, 0, 0)),
            pl.BlockSpec((CONV_WIDTH, d), lambda i: (0, 0)),
            pl.BlockSpec((d, d), lambda i: (0, 0)),
        ],
        out_specs=tile,
        out_shape=jax.ShapeDtypeStruct((n, d), F32),
        compiler_params=_params("arbitrary"),
        name="conv_out_prompt",
    )(x, bgate, u, u, state, conv_w, w_out)


def _conv_out_sample_kernel(x_ref, b_ref, um2_ref, um1_ref, u_ref, cw_ref, wo_ref, o_ref):
    a = (b_ref[...] * _conv_mix(cw_ref, um2_ref[...], um1_ref[...], u_ref[...])).astype(BF16)
    o_ref[...] = x_ref[...] + jnp.dot(a, wo_ref[...], preferred_element_type=F32)


def conv_out_sample(x, bgate, um2, um1, u0, conv_w, w_out):
    n, d = x.shape
    full = pl.BlockSpec((n, d), lambda i: (0, 0))
    return pl.pallas_call(
        _conv_out_sample_kernel,
        grid=(1,),
        in_specs=[full, full, full, full, full,
                  pl.BlockSpec((CONV_WIDTH, d), lambda i: (0, 0)),
                  pl.BlockSpec((d, d), lambda i: (0, 0))],
        out_specs=full,
        out_shape=jax.ShapeDtypeStruct((n, d), F32),
        compiler_params=_params("arbitrary"),
        name="conv_out_sample",
    )(x, bgate, um2, um1, u0, conv_w, w_out)


CAND_ROWS = 72


def _cand_flat_index():
    pairs = []
    for b in range(4):
        pairs += [(a, b) for a in range(16 if b == 0 else 8)]
    pairs += [(0, b) if b >= 4 else None for b in range(16)]
    pairs += [(1, b) if b >= 4 else None for b in range(8)]
    pairs += [(2, b) if b == 4 else None for b in range(8)]
    rows = [a * PEER_TOPK + b if pair is not None and (a + 1) * (b + 1) <= PEER_TOPK else 256 + i
            for i, pair in enumerate(pairs) for a, b in [pair or (99, 99)]]
    assert len(rows) == CAND_ROWS and len({r for r in rows if r < 256}) == 50
    return jnp.broadcast_to(jnp.asarray(rows, F32).reshape(CAND_ROWS, 1), (CAND_ROWS, V7X_LANES))


def _top16(s, key_iota, exact_ties):
    work = s
    rank = jnp.full(s.shape, NOT_SELECTED, F32)
    val_row = lax.broadcasted_iota(jnp.int32, (PEER_TOPK, s.shape[1]), 0)
    vals = jnp.zeros((PEER_TOPK, s.shape[1]), F32)
    for a in range(PEER_TOPK):
        m = jnp.max(work, axis=0, keepdims=True)
        sel = work == m
        if exact_ties:
            first = jnp.min(jnp.where(sel, key_iota, float(PEER_N_KEYS)), axis=0, keepdims=True)
            sel = key_iota == first
        rank = jnp.where(sel, float(a), rank)
        work = jnp.where(sel, NEG_INF, work)
        vals = jnp.where(val_row == a, m, vals)
    return vals, rank


def _ranked_rows(rank):
    return jnp.sum(jnp.where(rank < NOT_SELECTED, 1.0, 0.0), axis=0, keepdims=True)


def _peer_select(s1, s2, fidx, exact_ties):
    lanes = s1.shape[1]
    key_iota = lax.broadcasted_iota(jnp.int32, (PEER_N_KEYS, lanes), 0).astype(F32)
    v1, rank1 = _top16(s1, key_iota, exact_ties)
    v2, rank2 = _top16(s2, key_iota, exact_ties)

    r16 = lax.broadcasted_iota(jnp.int32, (16, lanes), 0)
    r8 = lax.broadcasted_iota(jnp.int32, (8, lanes), 0)
    v1_lo, v2_lo = v1[0:8, :], v2[0:8, :]
    cand = jnp.concatenate([
        v1 + v2[0:1, :],
        v1_lo + v2[1:2, :],
        jnp.where(r8 < 5, v1_lo + v2[2:3, :], NEG_INF),
        jnp.where(r8 < 4, v1_lo + v2[3:4, :], NEG_INF),
        jnp.where(r16 >= 4, v1[0:1, :] + v2, NEG_INF),
        jnp.where(r8 >= 4, v1[1:2, :] + v2_lo, NEG_INF),
        jnp.where(r8 == 4, v1[2:3, :] + v2_lo, NEG_INF),
    ], axis=0)
    c_top = v1[0:1, :] + v2[0:1, :]
    taken = jnp.zeros(cand.shape, F32)
    z = jnp.zeros((1, lanes), F32)
    for _ in range(PEER_TOPK):
        m = jnp.max(cand, axis=0, keepdims=True)
        sel = cand == m
        if exact_ties:
            first = jnp.min(jnp.where(sel, fidx, 1024.0), axis=0, keepdims=True)
            sel = fidx == first
        taken = jnp.where(sel, 1.0, taken)
        cand = jnp.where(sel, NEG_INF, cand)
        z = z + jnp.exp(m - c_top)

    extra = jnp.where(r8 == 0, jnp.sum(taken[40:56, :], axis=0, keepdims=True),
                      jnp.where(r8 == 1, jnp.sum(taken[56:64, :], axis=0, keepdims=True),
                                jnp.where(r8 == 2, jnp.sum(taken[64:72, :], axis=0, keepdims=True), 0.0)))
    cnt_lo = taken[0:8, :] + taken[16:24, :] + taken[24:32, :] + taken[32:40, :] + extra
    cnt_vec = jnp.concatenate([cnt_lo, taken[8:16, :]], axis=0)
    cnt = jnp.zeros(s1.shape, F32)
    for a in range(PEER_TOPK):
        cnt = jnp.where(rank1 == float(a), cnt_vec[a:a + 1, :], cnt)

    e1n = jnp.exp(s1 - v1[0:1, :]) / z
    e2 = jnp.exp(s2 - v2[0:1, :])
    ranked = jnp.maximum(jnp.maximum(_ranked_rows(rank1), _ranked_rows(rank2)),
                         jnp.sum(taken, axis=0, keepdims=True))
    return rank2, e2, cnt, e1n, ranked


def _peer_query_kernel(x_ref, g_ref, wq_ref, keys_ref, fidx_ref,
                       xn_ref, e2r_ref, cnt_ref, e1n_ref, sc_ref):
    tm = x_ref.shape[0]
    xn = _rms_norm(x_ref[...], g_ref[...]).astype(BF16)
    xn_ref[...] = xn
    q = jnp.dot(xn, wq_ref[...], preferred_element_type=F32).astype(BF16)
    for rp in range(2 * PEER_HEADS):
        q_rp = q[:, rp * PEER_D_HALF:(rp + 1) * PEER_D_HALF]
        sc_ref[rp] = lax.dot_general(keys_ref[rp], q_rp, NT_DIMS, preferred_element_type=F32)

    fidx = fidx_ref[...]

    def select_head(r, exact_ties):
        ranked = jnp.zeros((1, V7X_LANES), F32)
        for s in range(tm // V7X_LANES):
            cols = slice(s * V7X_LANES, (s + 1) * V7X_LANES)
            rank2, e2, cnt, e1n, n = _peer_select(sc_ref[2 * r, :, cols], sc_ref[2 * r + 1, :, cols], fidx,
                                                  exact_ties)
            e2_bits = pltpu.bitcast(e2, jnp.int32) & ~RANK_MASK
            e2r_ref[s, r] = pltpu.bitcast(e2_bits | rank2.astype(jnp.int32), F32)
            cnt_ref[r, :, cols] = cnt
            e1n_ref[r, :, cols] = e1n
            ranked = jnp.maximum(ranked, n)
        return ranked

    def head_body(r, carry):
        ranked = select_head(r, exact_ties=False)

        @pl.when(jnp.max(ranked) > float(PEER_TOPK))
        def _():
            select_head(r, exact_ties=True)

        return carry

    lax.fori_loop(0, PEER_HEADS, head_body, 0)


def peer_query(x, g, w_q, keys, layer, *, tm):
    n, d = x.shape
    dq = w_q.shape[2]
    groups = tm // V7X_LANES
    by_i2 = jax.ShapeDtypeStruct((n // V7X_LANES, PEER_HEADS, PEER_N_KEYS, V7X_LANES), F32)
    by_i2_spec = pl.BlockSpec((groups, PEER_HEADS, PEER_N_KEYS, V7X_LANES), lambda i: (i, 0, 0, 0))
    sel = jax.ShapeDtypeStruct((PEER_HEADS, PEER_N_KEYS, n), F32)
    sel_spec = pl.BlockSpec((PEER_HEADS, PEER_N_KEYS, tm), lambda i: (0, 0, i))
    return pl.pallas_call(
        _peer_query_kernel,
        grid=(n // tm,),
        in_specs=[
            pl.BlockSpec((tm, d), lambda i: (i, 0)),
            pl.BlockSpec((1, d), lambda i: (0, 0)),
            pl.BlockSpec((None, d, dq), lambda i: (layer, 0, 0)),
            pl.BlockSpec((None, 2 * PEER_HEADS, PEER_N_KEYS, PEER_D_HALF), lambda i: (layer, 0, 0, 0)),
            pl.BlockSpec((CAND_ROWS, V7X_LANES), lambda i: (0, 0)),
        ],
        out_specs=[pl.BlockSpec((tm, d), lambda i: (i, 0)), by_i2_spec, sel_spec, sel_spec],
        out_shape=[jax.ShapeDtypeStruct((n, d), BF16), by_i2, sel, sel],
        scratch_shapes=[pltpu.VMEM((2 * PEER_HEADS, PEER_N_KEYS, tm), F32)],
        compiler_params=_params("arbitrary"),
        name="peer_query",
    )(x, g, w_q, keys, _cand_flat_index())


GATE_ROWS = 32


def _gate_chunk(h, a_ref, e2r_ref, cnt_ref, e1n_ref):
    te, tm = h.shape
    for k in range(te // PEER_N_KEYS):
        for s in range(tm // V7X_LANES):
            cols = slice(s * V7X_LANES, (s + 1) * V7X_LANES)
            cnt = [cnt_ref[r, k:k + 1, cols].astype(jnp.int32) for r in range(PEER_HEADS)]
            e1n = [e1n_ref[r, k:k + 1, cols] for r in range(PEER_HEADS)]
            for i2 in range(0, PEER_N_KEYS, GATE_ROWS):
                gate = jnp.zeros((GATE_ROWS, V7X_LANES), F32)
                for r in range(PEER_HEADS):
                    e2r = e2r_ref[s, r, i2:i2 + GATE_ROWS, :]
                    live = (pltpu.bitcast(e2r, jnp.int32) & RANK_MASK) < cnt[r]
                    gate = gate + jnp.where(live, e2r, 0.0) * e1n[r]
                row0 = k * PEER_N_KEYS + i2
                hp = h[row0:row0 + GATE_ROWS, cols]
                act = 0.5 * hp * (1.0 + lax.erf(hp * SQRT_HALF))
                a_ref[row0:row0 + GATE_ROWS, cols] = (gate * act).astype(BF16)


def _peer_expert_kernel(xn_ref, x_ref, e2r_ref, cnt_ref, e1n_ref, u_ref, v_ref, o_ref, a_ref):
    c = pl.program_id(1)

    @pl.when(c == 0)
    def _():
        o_ref[...] = x_ref[...]

    h = lax.dot_general(u_ref[...], xn_ref[...], NT_DIMS, preferred_element_type=F32)
    _gate_chunk(h, a_ref, e2r_ref, cnt_ref, e1n_ref)
    o_ref[...] += lax.dot_general(a_ref[...], v_ref[...], TN_DIMS, preferred_element_type=F32)


def peer_experts(xn, x, e2r, cnt, e1n, u, v, layer, *, tm, te):
    n, d = x.shape
    tok = pl.BlockSpec((tm, d), lambda i, c: (i, 0))
    groups = tm // V7X_LANES
    by_i2 = pl.BlockSpec((groups, PEER_HEADS, PEER_N_KEYS, V7X_LANES), lambda i, c: (i, 0, 0, 0))
    by_i1 = pl.BlockSpec((PEER_HEADS, te // PEER_N_KEYS, tm), lambda i, c: (0, c, i))
    wts = pl.BlockSpec((None, te, d), lambda i, c: (layer, c, 0))
    return pl.pallas_call(
        _peer_expert_kernel,
        grid=(n // tm, u.shape[1] // te),
        in_specs=[tok, tok, by_i2, by_i1, by_i1, wts, wts],
        out_specs=tok,
        out_shape=jax.ShapeDtypeStruct((n, d), F32),
        scratch_shapes=[pltpu.VMEM((te, tm), BF16)],
        compiler_params=_params("arbitrary", "arbitrary"),
        name="peer_experts",
    )(xn, x, e2r, cnt, e1n, u, v)


def peer_layer(x, g, w_q, keys, u, v, layer, *, tm, te):
    xn, e2r, cnt, e1n = peer_query(x, g, w_q, keys, layer, tm=tm)
    return peer_experts(xn, x, e2r, cnt, e1n, u, v, layer, tm=tm, te=te)


def _residue_order(tm, dil):
    return np.arange(tm).reshape(tm // dil, dil).T.reshape(tm)


def _qkv_kernel(x_ref, g_ref, w_ref, qg_ref, kg_ref, cos_ref, sin_ref, o_ref, kv_ref, xs_ref, xp_ref, kvf_ref,
                *, dil, tiles, tail_tile):
    tm, d = x_ref.shape
    hd = w_ref.shape[1] // 3
    per_res = tm // dil
    if dil > 1:
        for c in range(d // V7X_LANES):
            xs_ref[c] = x_ref[:, c * V7X_LANES:(c + 1) * V7X_LANES]
        for res in range(dil):
            for c in range(d // V7X_LANES):
                xp_ref[res * per_res:(res + 1) * per_res, c * V7X_LANES:(c + 1) * V7X_LANES] = (
                    xs_ref[c, pl.ds(res, per_res, stride=dil), :])
        x = xp_ref[...]
    else:
        x = x_ref[...]
    xn = _rms_norm(x, g_ref[...]).astype(BF16)
    cos, sin = cos_ref[...], sin_ref[...]

    def store(col0, val):
        for res in range(dil):
            piece = val[res * per_res:(res + 1) * per_res, :]
            o_ref[res, :, col0:col0 + val.shape[1]] = piece.astype(BF16)
            if col0 >= hd:
                kvf_ref[res, :, col0 - hd:col0 - hd + val.shape[1]] = piece

    for kind, gain_ref in enumerate((qg_ref, kg_ref, None)):
        y = jnp.dot(xn, w_ref[:, kind * hd:(kind + 1) * hd], preferred_element_type=F32)
        if gain_ref is None:
            store(kind * hd, y)
            continue
        gain = gain_ref[...]
        for h in range(hd // HEAD_DIM):
            yn = _rms_norm(y[:, h * HEAD_DIM:(h + 1) * HEAD_DIM], gain)
            store(kind * hd + h * HEAD_DIM, yn * cos + pltpu.roll(yn, HEAD_DIM // 2, axis=1) * sin)

    @pl.when(pl.program_id(0) % tiles >= tail_tile)
    def _():
        rows = kv_ref.shape[0]
        if dil == 1:
            kv_ref[...] = kvf_ref[0, tm - rows:, :]
            return
        slabs = xs_ref.shape[0]
        for half in range(2 * hd // (slabs * V7X_LANES)):
            col0 = half * slabs * V7X_LANES
            for res in range(dil):
                for c in range(slabs):
                    xs_ref[c, pl.ds(res, per_res, stride=dil), :] = (
                        kvf_ref[res, :, col0 + c * V7X_LANES:col0 + (c + 1) * V7X_LANES])
            for c in range(slabs):
                kv_ref[:, col0 + c * V7X_LANES:col0 + (c + 1) * V7X_LANES] = xs_ref[c]


def qkv_project(x, g, w_qkv, q_gain, k_gain, pos, group, *, dil, batch, tm, keep):
    n, d = x.shape
    seq = n // batch
    hd3 = w_qkv.shape[1] // N_GROUPS
    tiles = seq // tm
    per_res = tm // dil
    assert per_res % (2 * V7X_SUBLANES) == 0
    kv_rows = min(keep, tm)
    assert keep % kv_rows == 0 and (dil == 1 or kv_rows == tm)
    tail_tile = tiles - keep // kv_rows
    out_specs = [pl.BlockSpec((None, dil, per_res, hd3), lambda i: (i // tiles, 0, i % tiles, 0)),
                 pl.BlockSpec((None, kv_rows, 2 * hd3 // 3),
                              lambda i: (i // tiles, jnp.maximum(i % tiles - tail_tile, 0), 0))]
    out_shape = [jax.ShapeDtypeStruct((batch, dil, seq // dil, hd3), BF16),
                 jax.ShapeDtypeStruct((batch, keep, 2 * hd3 // 3), F32)]
    order = (np.arange(tiles)[:, None] * tm + _residue_order(tm, dil)[None, :]).reshape(seq)
    cos, sin = _rope_tables(jnp.asarray(np.asarray(pos)[order]))
    gain = pl.BlockSpec((None, 1, HEAD_DIM), lambda i: (group, 0, 0))
    rope = pl.BlockSpec((tm, HEAD_DIM), lambda i: (i % tiles, 0))
    return pl.pallas_call(
        functools.partial(_qkv_kernel, dil=dil, tiles=tiles, tail_tile=tail_tile),
        grid=(n // tm,),
        in_specs=[
            pl.BlockSpec((tm, d), lambda i: (i, 0)),
            pl.BlockSpec((1, d), lambda i: (0, 0)),
            pl.BlockSpec((d, hd3), lambda i: (0, group), pipeline_mode=pl.Buffered(1)),
            gain, gain, rope, rope,
        ],
        out_specs=out_specs,
        out_shape=out_shape,
        scratch_shapes=[pltpu.VMEM((d // V7X_LANES, tm, V7X_LANES), F32), pltpu.VMEM((tm, d), F32),
                        pltpu.VMEM((dil, per_res, 2 * hd3 // 3), F32)],
        compiler_params=_params("arbitrary"),
        name=f"qkv_project_g{group}_d{dil}",
    )(x, g, w_qkv, q_gain.reshape(N_GROUPS, 1, HEAD_DIM), k_gain.reshape(N_GROUPS, 1, HEAD_DIM), cos, sin)


def _spread_heads(cols_list, width):
    rows = cols_list[0].shape[0]
    lane = lax.broadcasted_iota(jnp.int32, (rows, width), 1)
    out = jnp.zeros((rows, width), F32)
    for h, col in enumerate(cols_list):
        out = jnp.where(lane == h, col, out)
    return out


def _attn_prompt_kernel(q_ref, kc_ref, kp_ref, vc_ref, vp_ref, o_ref, lse_ref, *, window_blocks):
    i = pl.program_id(2)
    qb = q_ref.shape[0]
    qi = lax.broadcasted_iota(jnp.int32, (qb, 2 * qb), 0)
    kj = lax.broadcasted_iota(jnp.int32, (qb, 2 * qb), 1)
    live = (kj <= qi + qb) & (kj >= qi + qb - window_blocks) & ((i > 0) | (kj >= qb))
    lses = []
    for h in range(q_ref.shape[1] // HEAD_DIM):
        cols = slice(h * HEAD_DIM, (h + 1) * HEAD_DIM)
        q = q_ref[:, cols].astype(BF16)
        k = jnp.concatenate([kp_ref[:, cols], kc_ref[:, cols]], axis=0).astype(BF16)
        v = jnp.concatenate([vp_ref[:, cols], vc_ref[:, cols]], axis=0).astype(BF16)
        s = lax.dot_general(q, k, NT_DIMS, preferred_element_type=F32) * ATTN_SCALE
        s = jnp.where(live, s, NEG_INF)
        m = jnp.max(s, axis=-1, keepdims=True)
        p = jnp.exp(s - m)
        l = jnp.sum(p, axis=-1, keepdims=True)
        o_ref[:, cols] = jnp.dot(p.astype(BF16), v, preferred_element_type=F32) / l
        lses.append(m + jnp.log(l))
    lse_ref[...] = _spread_heads(lses, lse_ref.shape[1])


def attn_prompt_group(qkv, group):
    batch, dil, n_strided, hd3 = qkv.shape
    hd = hd3 // 3
    win = WINDOWS[group]
    assert dil == DILATIONS[group] and n_strided % ATTN_BLOCK == 0 and win // dil <= ATTN_BLOCK
    blk = (None, None, ATTN_BLOCK, hd)

    def cur(kind):
        return pl.BlockSpec(blk, lambda b, r, i: (b, r, i, kind))

    def prev(kind):
        return pl.BlockSpec(blk, lambda b, r, i: (b, r, jnp.maximum(i - 1, 0), kind))

    return pl.pallas_call(
        functools.partial(_attn_prompt_kernel, window_blocks=win // dil),
        grid=(batch, dil, n_strided // ATTN_BLOCK),
        in_specs=[cur(0), cur(1), prev(1), cur(2), prev(2)],
        out_specs=[pl.BlockSpec(blk, lambda b, r, i: (b, r, i, 0)),
                   pl.BlockSpec((None, None, ATTN_BLOCK, V7X_LANES), lambda b, r, i: (b, r, i, 0))],
        out_shape=[jax.ShapeDtypeStruct((batch, dil, n_strided, hd), F32),
                   jax.ShapeDtypeStruct((batch, dil, n_strided, V7X_LANES), F32)],
        compiler_params=_params("arbitrary", "arbitrary", "arbitrary"),
        name=f"attn_prompt_g{group}",
    )(qkv, qkv, qkv, qkv, qkv)


def _head_segments(n_heads, transpose):
    shape = (V7X_LANES, n_heads * HEAD_DIM) if transpose else (n_heads * HEAD_DIM, V7X_LANES)
    c = lax.broadcasted_iota(jnp.int32, shape, 1 if transpose else 0)
    h = lax.broadcasted_iota(jnp.int32, shape, 0 if transpose else 1)
    return jnp.where((c >= h * HEAD_DIM) & (c < (h + 1) * HEAD_DIM), 1.0, 0.0).astype(BF16)


def _attn_sample_kernel(q0_ref, q1_ref, q2_ref, c0_ref, c1_ref, c2_ref, o_ref, lse_ref):
    t = pl.program_id(1)
    hd = o_ref.shape[2]
    n_heads = hd // HEAD_DIM
    t_new = q0_ref.shape[0]
    seg = _head_segments(n_heads, transpose=False)
    seg_t = _head_segments(n_heads, transpose=True)
    row_c = lax.broadcasted_iota(jnp.int32, (c0_ref.shape[0], V7X_LANES), 0)
    row_n = lax.broadcasted_iota(jnp.int32, (t_new, V7X_LANES), 0)
    for g, (qkv_ref, cache_ref) in enumerate(zip((q0_ref, q1_ref, q2_ref), (c0_ref, c1_ref, c2_ref))):
        q = qkv_ref[pl.ds(t, 1), 0:hd]
        k_new = qkv_ref[:, hd:2 * hd]
        v_new = qkv_ref[:, 2 * hd:3 * hd]
        k_old, v_old = cache_ref[:, 0:hd], cache_ref[:, hd:2 * hd]
        s_old = _split_dot(k_old * q, seg) * ATTN_SCALE
        s_new = _split_dot(k_new * q, seg) * ATTN_SCALE
        if DILATIONS[g] == 1:
            s_old = jnp.where(row_c >= t, s_old, NEG_INF)
            s_new = jnp.where(row_n <= t, s_new, NEG_INF)
        else:
            s_new = jnp.where(row_n == t, s_new, NEG_INF)
        m = jnp.maximum(jnp.max(s_old, axis=0, keepdims=True), jnp.max(s_new, axis=0, keepdims=True))
        p_old, p_new = jnp.exp(s_old - m), jnp.exp(s_new - m)
        l = jnp.sum(p_old, axis=0, keepdims=True) + jnp.sum(p_new, axis=0, keepdims=True)
        w_old = _split_dot(p_old / l, seg_t)
        w_new = _split_dot(p_new / l, seg_t)
        o = jnp.sum(w_old * v_old, axis=0, keepdims=True) + jnp.sum(w_new * v_new, axis=0, keepdims=True)
        o_ref[g, pl.ds(t, 1), :] = o
        lse_ref[g, pl.ds(t, 1), :] = m + jnp.log(l)


def attn_sample(qkv, caches, *, batch, t_new):
    cols = qkv[0].shape[1]
    hd = cols // 3
    views, specs = [], []
    for g, cache in enumerate(caches):
        dil, keep = DILATIONS[g], cache.shape[1]
        assert keep == WINDOWS[g] and keep // dil == ATTN_BLOCK and (dil == 1 or t_new <= dil)
        if dil == 1:
            views.append(cache.reshape(batch, keep, 2 * hd))
            specs.append(pl.BlockSpec((None, ATTN_BLOCK, 2 * hd), lambda b, t: (b, 0, 0)))
        else:
            by_res = jnp.stack([cache[:, t::dil] for t in range(t_new)], axis=1)
            views.append(by_res.reshape(batch, t_new, keep // dil, 2 * hd))
            specs.append(pl.BlockSpec((None, None, ATTN_BLOCK, 2 * hd), lambda b, t: (b, t, 0, 0)))
    rows_new = -(-t_new // V7X_SUBLANES) * V7X_SUBLANES
    qkv_rows = [jnp.pad(a.reshape(batch, t_new, cols), ((0, 0), (0, rows_new - t_new), (0, 0))) for a in qkv]
    o, lse = pl.pallas_call(
        _attn_sample_kernel,
        grid=(batch, t_new),
        in_specs=[pl.BlockSpec((None, rows_new, cols), lambda b, t: (b, 0, 0))] * N_GROUPS + specs,
        out_specs=[pl.BlockSpec((N_GROUPS, None, t_new, hd), lambda b, t: (0, b, 0, 0)),
                   pl.BlockSpec((N_GROUPS, None, t_new, V7X_LANES), lambda b, t: (0, b, 0, 0))],
        out_shape=[jax.ShapeDtypeStruct((N_GROUPS, batch, t_new, hd), F32),
                   jax.ShapeDtypeStruct((N_GROUPS, batch, t_new, V7X_LANES), F32)],
        compiler_params=_params("arbitrary", "arbitrary"),
        name="attn_sample",
    )(*qkv_rows, *views)
    return o.reshape(N_GROUPS, batch * t_new, hd), lse.reshape(N_GROUPS, batch * t_new, V7X_LANES)


def _attn_out_kernel(x_ref, o0_ref, o1_ref, o2_ref, l0_ref, l1_ref, l2_ref, wo_ref, y_ref,
                     og_ref, lg_ref, mix_ref):
    tm = x_ref.shape[0]
    hd = wo_ref.shape[0]
    heads = hd // HEAD_DIM
    for g, (o_ref, l_ref) in enumerate(zip((o0_ref, o1_ref, o2_ref), (l0_ref, l1_ref, l2_ref))):
        dil = o_ref.shape[0]
        for res in range(dil):
            rows = pl.ds(res, tm // dil, stride=dil) if dil > 1 else slice(0, tm)
            lg_ref[g, rows, :] = l_ref[res]
            for h in range(heads):
                og_ref[g, h, rows, :] = o_ref[res, :, h * HEAD_DIM:(h + 1) * HEAD_DIM]
    lse = [lg_ref[g] for g in range(N_GROUPS)]
    m = functools.reduce(jnp.maximum, lse)
    e = [jnp.exp(l - m) for l in lse]
    tot = functools.reduce(jnp.add, e)
    seg_t = _head_segments(heads, transpose=True)
    w = [_split_dot(e[g] / tot, seg_t) for g in range(N_GROUPS)]
    for h in range(heads):
        cols = slice(h * HEAD_DIM, (h + 1) * HEAD_DIM)
        mix = functools.reduce(jnp.add, [w[g][:, cols] * og_ref[g, h] for g in range(N_GROUPS)])
        mix_ref[:, cols] = mix.astype(BF16)
    y_ref[...] = x_ref[...] + jnp.dot(mix_ref[...], wo_ref[...], preferred_element_type=F32)


def attn_out(x, outs, lses, w_o, *, tm):
    n, d = x.shape
    batch = outs[0].shape[0]
    hd = outs[0].shape[3]
    tiles = n // batch // tm

    def by_residue(a):
        dil = a.shape[1]
        return pl.BlockSpec((None, dil, tm // dil, a.shape[3]), lambda i: (i // tiles, 0, i % tiles, 0))

    return pl.pallas_call(
        _attn_out_kernel,
        grid=(n // tm,),
        in_specs=[pl.BlockSpec((tm, d), lambda i: (i, 0))] + [by_residue(a) for a in (*outs, *lses)]
        + [pl.BlockSpec((hd, d), lambda i: (0, 0))],
        out_specs=pl.BlockSpec((tm, d), lambda i: (i, 0)),
        out_shape=jax.ShapeDtypeStruct((n, d), F32),
        scratch_shapes=[pltpu.VMEM((N_GROUPS, hd // HEAD_DIM, tm, HEAD_DIM), F32),
                        pltpu.VMEM((N_GROUPS, tm, V7X_LANES), F32), pltpu.VMEM((tm, hd), BF16)],
        compiler_params=_params("arbitrary"),
        name="attn_out",
    )(x, *outs, *lses, w_o)


def _kv_state_kernel(c_ref, n_ref, o_ref):
    i = pl.program_id(1)
    last = pl.num_programs(1) - 1
    rows, t_new = o_ref.shape[0], n_ref.shape[0]

    @pl.when(i < last)
    def _():
        o_ref[...] = c_ref[0]

    @pl.when(i == last)
    def _():
        o_ref[0:rows - t_new] = c_ref[0, t_new:rows]
        o_ref[rows - t_new:rows] = n_ref[...]


def kv_state_update(cache, new, *, rows):
    batch, keep = cache.shape[:2]
    t_new = new.shape[1]
    tail = cache.shape[2:]
    blocks = keep // rows
    assert keep % rows == 0 and rows > t_new and new.shape[2:] == tail
    zeros = (0,) * len(tail)

    def shifted(b, i):
        return (b, jnp.where(i < blocks - 1, i * rows + t_new, keep - rows)) + zeros

    return pl.pallas_call(
        _kv_state_kernel,
        grid=(batch, blocks),
        in_specs=[pl.BlockSpec(tuple(pl.Element(size) for size in (1, rows) + tail), shifted),
                  pl.BlockSpec((None, t_new) + tail, lambda b, i: (b, 0) + zeros)],
        out_specs=pl.BlockSpec((None, rows) + tail, lambda b, i: (b, i) + zeros),
        out_shape=jax.ShapeDtypeStruct(cache.shape, cache.dtype),
        compiler_params=_params("arbitrary", "arbitrary"),
        name="kv_state_update",
    )(cache, new)


def _rope_tables(pos):
    half = HEAD_DIM // 2
    inv_freq = ROPE_THETA ** (-jnp.arange(half, dtype=F32) / half)
    ang = pos.astype(F32)[:, None] * inv_freq
    cos, sin = jnp.cos(ang), jnp.sin(ang)
    return jnp.concatenate([cos, cos], axis=-1), jnp.concatenate([-sin, sin], axis=-1)


def _pad_rows(a, rows):
    return jnp.pad(a, ((0, rows - a.shape[0]),) + ((0, 0),) * (a.ndim - 1))


PROMPT_TM = 512
ATTN_OUT_TM = 256
QKV_TM = 256
PEER_TE = 1024
CONV_TN = 512
SAMPLE_PEER_ROWS = V7X_LANES


def kernel(x_prompt, x_sample, state_conv, cache_kv_w128, cache_kv_w512, cache_kv_w2048,
           norm_mix, norm_ffn, conv_w_in, conv_w, conv_w_out,
           attn_w_qkv, attn_q_norm, attn_k_norm, attn_w_o,
           peer_w_q, peer_sub_keys, peer_u, peer_v):
    batch, seq, d = x_prompt.shape
    dec_batch, dec_seq, _ = x_sample.shape
    n_p, n_s = batch * seq, dec_batch * dec_seq
    caches = (cache_kv_w128, cache_kv_w512, cache_kv_w2048)
    n_heads = caches[0].shape[3]
    hd = n_heads * HEAD_DIM

    w_in, w_out = conv_w_in.astype(BF16), conv_w_out.astype(BF16)
    w_qkv, w_o = attn_w_qkv.astype(BF16), attn_w_o.astype(BF16)
    w_q = peer_w_q.astype(BF16)
    keys = peer_sub_keys.astype(BF16).reshape(peer_sub_keys.shape[0], 2 * PEER_HEADS, PEER_N_KEYS, PEER_D_HALF)
    pu, pv = peer_u.astype(BF16), peer_v.astype(BF16)
    g_mix = norm_mix.reshape(-1, 1, d)
    g_ffn = norm_ffn.reshape(-1, 1, d)

    xp = x_prompt.reshape(n_p, d)
    xs = x_sample.reshape(n_s, d)

    def peer(x, layer, tm):
        return peer_layer(x, g_ffn[layer], w_q, keys, pu, pv, layer, tm=tm, te=PEER_TE)

    def peer_sample(x, layer):
        return peer(_pad_rows(x, SAMPLE_PEER_ROWS), layer, SAMPLE_PEER_ROWS)[:n_s]

    bp, up = conv_in(xp, g_mix[0], w_in, tm=PROMPT_TM, tn=CONV_TN)
    xp = conv_out_prompt(xp, bp, up, jnp.zeros((batch, CONV_WIDTH - 1, d), F32), conv_w, w_out,
                         seq_len=seq, tm=PROMPT_TM)
    conv_state_prompt = up.reshape(batch, seq, d)[:, seq - (CONV_WIDTH - 1):]

    bs, us = conv_in(xs, g_mix[0], w_in, tm=n_s, tn=CONV_TN)
    u_ext = jnp.concatenate([state_conv, us.reshape(dec_batch, dec_seq, d)], axis=1)
    shifted = [u_ext[:, k:k + dec_seq].reshape(n_s, d) for k in range(CONV_WIDTH)]
    xs = conv_out_sample(xs, bs, shifted[0], shifted[1], shifted[2], conv_w, w_out)
    conv_state_sample = u_ext[:, dec_seq:]

    xp = peer(xp, 0, PROMPT_TM)
    xs = peer_sample(xs, 0)

    qkv_p, kv_prompt = zip(*[
        qkv_project(xp, g_mix[1], w_qkv, attn_q_norm, attn_k_norm, np.arange(seq), g,
                    dil=DILATIONS[g], batch=batch, tm=QKV_TM, keep=min(WINDOWS[g], seq)) for g in range(N_GROUPS)])
    outs, lses = zip(*[attn_prompt_group(qkv_p[g], g) for g in range(N_GROUPS)])
    xp = attn_out(xp, outs, lses, w_o, tm=ATTN_OUT_TM)

    pos_s = np.tile(PAST_LEN + np.arange(dec_seq), dec_batch)
    q_s, kv_s = zip(*[qkv_project(xs, g_mix[1], w_qkv, attn_q_norm, attn_k_norm, pos_s, g,
                                  dil=1, batch=1, tm=n_s, keep=n_s) for g in range(N_GROUPS)])
    kv_s = [kv.reshape(n_s, 2 * hd) for kv in kv_s]
    qkv_s = [jnp.concatenate([q.reshape(n_s, 3 * hd)[:, :hd].astype(F32), kv], axis=1) for q, kv in zip(q_s, kv_s)]
    o_s, lse_s = attn_sample(qkv_s, caches, batch=dec_batch, t_new=dec_seq)
    xs = attn_out(xs, [o_s[g].reshape(1, 1, n_s, hd) for g in range(N_GROUPS)],
                  [lse_s[g].reshape(1, 1, n_s, V7X_LANES) for g in range(N_GROUPS)], w_o, tm=n_s)

    xp = peer(xp, 1, PROMPT_TM)
    xs = peer_sample(xs, 1)

    kv_prompt = [kv.reshape(batch, kv.shape[1], 2, n_heads, HEAD_DIM) for kv in kv_prompt]
    kv_sample = [kv_state_update(cache, kv.reshape(dec_batch, dec_seq, 2, n_heads, HEAD_DIM), rows=ATTN_BLOCK)
                 for cache, kv in zip(caches, kv_s)]

    return (xp.reshape(batch, seq, d), xs.reshape(dec_batch, dec_seq, d),
            conv_state_prompt, conv_state_sample, *kv_prompt, *kv_sample)
```

```python
import functools
import math

import jax
import jax.numpy as jnp
import numpy as np
from jax import lax
from jax.experimental import pallas as pl
from jax.experimental.pallas import tpu as pltpu

F32 = jnp.float32
BF16 = jnp.bfloat16

V7X_LANES = 128
V7X_SUBLANES = 8
V7X_VMEM_BYTES = 64 * 1024 * 1024
VMEM_LIMIT_BYTES = V7X_VMEM_BYTES - 8 * 1024 * 1024

NORM_EPS = 1e-6
CONV_WIDTH = 3
WINDOWS = (128, 512, 2048)
DILATIONS = (1, 4, 16)
N_GROUPS = 3
HEAD_DIM = 128
ATTN_BLOCK = 128
ATTN_SCALE = HEAD_DIM ** -0.5
ROPE_THETA = 10000.0
PAST_LEN = 16384
assert PAST_LEN >= max(WINDOWS)
PEER_HEADS = 8
PEER_N_KEYS = 128
PEER_TOPK = 16
PEER_D_HALF = 128
SQRT_HALF = math.sqrt(0.5)
RANK_MASK = 31
NOT_SELECTED = float(RANK_MASK)
NEG_INF = float("-inf")

NT_DIMS = (((1,), (1,)), ((), ()))
TN_DIMS = (((0,), (0,)), ((), ()))


def _params(*semantics):
    return pltpu.CompilerParams(dimension_semantics=semantics, vmem_limit_bytes=VMEM_LIMIT_BYTES)


def _rms_norm(x, g):
    return x * lax.rsqrt(jnp.mean(x * x, axis=-1, keepdims=True) + NORM_EPS) * g


def _split_dot(a, b01):
    hi = a.astype(BF16)
    lo = (a - hi.astype(F32)).astype(BF16)
    return (jnp.dot(hi, b01, preferred_element_type=F32)
            + jnp.dot(lo, b01, preferred_element_type=F32))


def _conv_in_kernel(x_ref, g_ref, wb_ref, wc_ref, wh_ref, b_ref, u_ref, xn_ref):
    @pl.when(pl.program_id(1) == 0)
    def _():
        xn_ref[...] = _rms_norm(x_ref[...], g_ref[...]).astype(BF16)

    xn = xn_ref[...]
    b_ref[...] = jnp.dot(xn, wb_ref[...], preferred_element_type=F32)
    c = jnp.dot(xn, wc_ref[...], preferred_element_type=F32)
    h = jnp.dot(xn, wh_ref[...], preferred_element_type=F32)
    u_ref[...] = c * h


def conv_in(x, g, w_in, *, tm, tn):
    n, d = x.shape
    nj = d // tn
    return pl.pallas_call(
        _conv_in_kernel,
        grid=(n // tm, nj),
        in_specs=[
            pl.BlockSpec((tm, d), lambda i, j: (i, 0)),
            pl.BlockSpec((1, d), lambda i, j: (0, 0)),
            pl.BlockSpec((d, tn), lambda i, j: (0, j)),
            pl.BlockSpec((d, tn), lambda i, j: (0, j + nj)),
            pl.BlockSpec((d, tn), lambda i, j: (0, j + 2 * nj)),
        ],
        out_specs=[pl.BlockSpec((tm, tn), lambda i, j: (i, j)),
                   pl.BlockSpec((tm, tn), lambda i, j: (i, j))],
        out_shape=[jax.ShapeDtypeStruct((n, d), F32), jax.ShapeDtypeStruct((n, d), F32)],
        scratch_shapes=[pltpu.VMEM((tm, d), BF16)],
        compiler_params=_params("arbitrary", "arbitrary"),
        name="conv_in",
    )(x, g, w_in, w_in, w_in)


def _conv_mix(cw_ref, um2, um1, u0):
    return cw_ref[0:1, :] * um2 + cw_ref[1:2, :] * um1 + cw_ref[2:3, :] * u0


def _conv_out_prompt_kernel(x_ref, b_ref, u_ref, up_ref, st_ref, cw_ref, wo_ref, o_ref, *, tiles_per_seq):
    i = pl.program_id(0)
    u = u_ref[...]
    first = (i % tiles_per_seq) == 0
    prev = jnp.where(first, st_ref[...], up_ref[V7X_SUBLANES - 2:V7X_SUBLANES, :])
    row = lax.broadcasted_iota(jnp.int32, (u.shape[0], 1), 0)
    um1 = jnp.where(row == 0, prev[1:2, :], pltpu.roll(u, 1, axis=0))
    um2 = jnp.where(row == 0, prev[0:1, :], jnp.where(row == 1, prev[1:2, :], pltpu.roll(u, 2, axis=0)))
    a = (b_ref[...] * _conv_mix(cw_ref, um2, um1, u)).astype(BF16)
    o_ref[...] = x_ref[...] + jnp.dot(a, wo_ref[...], preferred_element_type=F32)


def conv_out_prompt(x, bgate, u, state, conv_w, w_out, *, seq_len, tm):
    n, d = x.shape
    tps = seq_len // tm
    sub_per_tile = tm // V7X_SUBLANES
    tile = pl.BlockSpec((tm, d), lambda i: (i, 0))
    return pl.pallas_call(
        functools.partial(_conv_out_prompt_kernel, tiles_per_seq=tps),
        grid=(n // tm,),
        in_specs=[
            tile, tile, tile,
            pl.BlockSpec((V7X_SUBLANES, d), lambda i: (jnp.maximum(i * sub_per_tile - 1, 0), 0)),
            pl.BlockSpec((None, CONV_WIDTH - 1, d), lambda i: (i // tps, 0, 0)),
            pl.BlockSpec((CONV_WIDTH, d), lambda i: (0, 0)),
            pl.BlockSpec((d, d), lambda i: (0, 0)),
        ],
        out_specs=tile,
        out_shape=jax.ShapeDtypeStruct((n, d), F32),
        compiler_params=_params("arbitrary"),
        name="conv_out_prompt",
    )(x, bgate, u, u, state, conv_w, w_out)


def _conv_out_sample_kernel(x_ref, b_ref, um2_ref, um1_ref, u_ref, cw_ref, wo_ref, o_ref):
    a = (b_ref[...] * _conv_mix(cw_ref, um2_ref[...], um1_ref[...], u_ref[...])).astype(BF16)
    o_ref[...] = x_ref[...] + jnp.dot(a, wo_ref[...], preferred_element_type=F32)


def conv_out_sample(x, bgate, um2, um1, u0, conv_w, w_out):
    n, d = x.shape
    full = pl.BlockSpec((n, d), lambda i: (0, 0))
    return pl.pallas_call(
        _conv_out_sample_kernel,
        grid=(1,),
        in_specs=[full, full, full, full, full,
                  pl.BlockSpec((CONV_WIDTH, d), lambda i: (0, 0)),
                  pl.BlockSpec((d, d), lambda i: (0, 0))],
        out_specs=full,
        out_shape=jax.ShapeDtypeStruct((n, d), F32),
        compiler_params=_params("arbitrary"),
        name="conv_out_sample",
    )(x, bgate, um2, um1, u0, conv_w, w_out)


CAND_ROWS = 72


def _cand_flat_index():
    pairs = []
    for b in range(4):
        pairs += [(a, b) for a in range(16 if b == 0 else 8)]
    pairs += [(0, b) if b >= 4 else None for b in range(16)]
    pairs += [(1, b) if b >= 4 else None for b in range(8)]
    pairs += [(2, b) if b == 4 else None for b in range(8)]
    rows = [a * PEER_TOPK + b if pair is not None and (a + 1) * (b + 1) <= PEER_TOPK else 256 + i
            for i, pair in enumerate(pairs) for a, b in [pair or (99, 99)]]
    assert len(rows) == CAND_ROWS and len({r for r in rows if r < 256}) == 50
    return jnp.broadcast_to(jnp.asarray(rows, F32).reshape(CAND_ROWS, 1), (CAND_ROWS, V7X_LANES))


def _top16(s, key_iota, exact_ties):
    work = s
    rank = jnp.full(s.shape, NOT_SELECTED, F32)
    val_row = lax.broadcasted_iota(jnp.int32, (PEER_TOPK, s.shape[1]), 0)
    vals = jnp.zeros((PEER_TOPK, s.shape[1]), F32)
    for a in range(PEER_TOPK):
        m = jnp.max(work, axis=0, keepdims=True)
        sel = work == m
        if exact_ties:
            first = jnp.min(jnp.where(sel, key_iota, float(PEER_N_KEYS)), axis=0, keepdims=True)
            sel = key_iota == first
        rank = jnp.where(sel, float(a), rank)
        work = jnp.where(sel, NEG_INF, work)
        vals = jnp.where(val_row == a, m, vals)
    return vals, rank


def _ranked_rows(rank):
    return jnp.sum(jnp.where(rank < NOT_SELECTED, 1.0, 0.0), axis=0, keepdims=True)


def _peer_select(s1, s2, fidx, exact_ties):
    lanes = s1.shape[1]
    key_iota = lax.broadcasted_iota(jnp.int32, (PEER_N_KEYS, lanes), 0).astype(F32)
    v1, rank1 = _top16(s1, key_iota, exact_ties)
    v2, rank2 = _top16(s2, key_iota, exact_ties)

    r16 = lax.broadcasted_iota(jnp.int32, (16, lanes), 0)
    r8 = lax.broadcasted_iota(jnp.int32, (8, lanes), 0)
    v1_lo, v2_lo = v1[0:8, :], v2[0:8, :]
    cand = jnp.concatenate([
        v1 + v2[0:1, :],
        v1_lo + v2[1:2, :],
        jnp.where(r8 < 5, v1_lo + v2[2:3, :], NEG_INF),
        jnp.where(r8 < 4, v1_lo + v2[3:4, :], NEG_INF),
        jnp.where(r16 >= 4, v1[0:1, :] + v2, NEG_INF),
        jnp.where(r8 >= 4, v1[1:2, :] + v2_lo, NEG_INF),
        jnp.where(r8 == 4, v1[2:3, :] + v2_lo, NEG_INF),
    ], axis=0)
    c_top = v1[0:1, :] + v2[0:1, :]
    taken = jnp.zeros(cand.shape, F32)
    z = jnp.zeros((1, lanes), F32)
    for _ in range(PEER_TOPK):
        m = jnp.max(cand, axis=0, keepdims=True)
        sel = cand == m
        if exact_ties:
            first = jnp.min(jnp.where(sel, fidx, 1024.0), axis=0, keepdims=True)
            sel = fidx == first
        taken = jnp.where(sel, 1.0, taken)
        cand = jnp.where(sel, NEG_INF, cand)
        z = z + jnp.exp(m - c_top)

    extra = jnp.where(r8 == 0, jnp.sum(taken[40:56, :], axis=0, keepdims=True),
                      jnp.where(r8 == 1, jnp.sum(taken[56:64, :], axis=0, keepdims=True),
                                jnp.where(r8 == 2, jnp.sum(taken[64:72, :], axis=0, keepdims=True), 0.0)))
    cnt_lo = taken[0:8, :] + taken[16:24, :] + taken[24:32, :] + taken[32:40, :] + extra
    cnt_vec = jnp.concatenate([cnt_lo, taken[8:16, :]], axis=0)
    cnt = jnp.zeros(s1.shape, F32)
    for a in range(PEER_TOPK):
        cnt = jnp.where(rank1 == float(a), cnt_vec[a:a + 1, :], cnt)

    e1n = jnp.exp(s1 - v1[0:1, :]) / z
    e2 = jnp.exp(s2 - v2[0:1, :])
    ranked = jnp.maximum(jnp.maximum(_ranked_rows(rank1), _ranked_rows(rank2)),
                         jnp.sum(taken, axis=0, keepdims=True))
    return rank2, e2, cnt, e1n, ranked


def _peer_query_kernel(x_ref, g_ref, wq_ref, keys_ref, fidx_ref,
                       xn_ref, e2r_ref, cnt_ref, e1n_ref, sc_ref):
    tm = x_ref.shape[0]
    xn = _rms_norm(x_ref[...], g_ref[...]).astype(BF16)
    xn_ref[...] = xn
    q = jnp.dot(xn, wq_ref[...], preferred_element_type=F32).astype(BF16)
    for rp in range(2 * PEER_HEADS):
        q_rp = q[:, rp * PEER_D_HALF:(rp + 1) * PEER_D_HALF]
        sc_ref[rp] = lax.dot_general(keys_ref[rp], q_rp, NT_DIMS, preferred_element_type=F32)

    fidx = fidx_ref[...]

    def select_head(r, exact_ties):
        ranked = jnp.zeros((1, V7X_LANES), F32)
        for s in range(tm // V7X_LANES):
            cols = slice(s * V7X_LANES, (s + 1) * V7X_LANES)
            rank2, e2, cnt, e1n, n = _peer_select(sc_ref[2 * r, :, cols], sc_ref[2 * r + 1, :, cols], fidx,
                                                  exact_ties)
            e2_bits = pltpu.bitcast(e2, jnp.int32) & ~RANK_MASK
            e2r_ref[s, r] = pltpu.bitcast(e2_bits | rank2.astype(jnp.int32), F32)
            cnt_ref[r, :, cols] = cnt
            e1n_ref[r, :, cols] = e1n
            ranked = jnp.maximum(ranked, n)
        return ranked

    def head_body(r, carry):
        ranked = select_head(r, exact_ties=False)

        @pl.when(jnp.max(ranked) > float(PEER_TOPK))
        def _():
            select_head(r, exact_ties=True)

        return carry

    lax.fori_loop(0, PEER_HEADS, head_body, 0)


def peer_query(x, g, w_q, keys, layer, *, tm):
    n, d = x.shape
    dq = w_q.shape[2]
    groups = tm // V7X_LANES
    by_i2 = jax.ShapeDtypeStruct((n // V7X_LANES, PEER_HEADS, PEER_N_KEYS, V7X_LANES), F32)
    by_i2_spec = pl.BlockSpec((groups, PEER_HEADS, PEER_N_KEYS, V7X_LANES), lambda i: (i, 0, 0, 0))
    sel = jax.ShapeDtypeStruct((PEER_HEADS, PEER_N_KEYS, n), F32)
    sel_spec = pl.BlockSpec((PEER_HEADS, PEER_N_KEYS, tm), lambda i: (0, 0, i))
    return pl.pallas_call(
        _peer_query_kernel,
        grid=(n // tm,),
        in_specs=[
            pl.BlockSpec((tm, d), lambda i: (i, 0)),
            pl.BlockSpec((1, d), lambda i: (0, 0)),
            pl.BlockSpec((None, d, dq), lambda i: (layer, 0, 0)),
            pl.BlockSpec((None, 2 * PEER_HEADS, PEER_N_KEYS, PEER_D_HALF), lambda i: (layer, 0, 0, 0)),
            pl.BlockSpec((CAND_ROWS, V7X_LANES), lambda i: (0, 0)),
        ],
        out_specs=[pl.BlockSpec((tm, d), lambda i: (i, 0)), by_i2_spec, sel_spec, sel_spec],
        out_shape=[jax.ShapeDtypeStruct((n, d), BF16), by_i2, sel, sel],
        scratch_shapes=[pltpu.VMEM((2 * PEER_HEADS, PEER_N_KEYS, tm), F32)],
        compiler_params=_params("arbitrary"),
        name="peer_query",
    )(x, g, w_q, keys, _cand_flat_index())


GATE_ROWS = 32


def _gate_chunk(h, a_ref, e2r_ref, cnt_ref, e1n_ref):
    te, tm = h.shape
    for k in range(te // PEER_N_KEYS):
        for s in range(tm // V7X_LANES):
            cols = slice(s * V7X_LANES, (s + 1) * V7X_LANES)
            cnt = [cnt_ref[r, k:k + 1, cols].astype(jnp.int32) for r in range(PEER_HEADS)]
            e1n = [e1n_ref[r, k:k + 1, cols] for r in range(PEER_HEADS)]
            for i2 in range(0, PEER_N_KEYS, GATE_ROWS):
                gate = jnp.zeros((GATE_ROWS, V7X_LANES), F32)
                for r in range(PEER_HEADS):
                    e2r = e2r_ref[s, r, i2:i2 + GATE_ROWS, :]
                    live = (pltpu.bitcast(e2r, jnp.int32) & RANK_MASK) < cnt[r]
                    gate = gate + jnp.where(live, e2r, 0.0) * e1n[r]
                row0 = k * PEER_N_KEYS + i2
                hp = h[row0:row0 + GATE_ROWS, cols]
                act = 0.5 * hp * (1.0 + lax.erf(hp * SQRT_HALF))
                a_ref[row0:row0 + GATE_ROWS, cols] = (gate * act).astype(BF16)


def _peer_expert_kernel(xn_ref, x_ref, e2r_ref, cnt_ref, e1n_ref, u_ref, v_ref, o_ref, a_ref):
    c = pl.program_id(1)

    @pl.when(c == 0)
    def _():
        o_ref[...] = x_ref[...]

    h = lax.dot_general(u_ref[...], xn_ref[...], NT_DIMS, preferred_element_type=F32)
    _gate_chunk(h, a_ref, e2r_ref, cnt_ref, e1n_ref)
    o_ref[...] += lax.dot_general(a_ref[...], v_ref[...], TN_DIMS, preferred_element_type=F32)


def peer_experts(xn, x, e2r, cnt, e1n, u, v, layer, *, tm, te):
    n, d = x.shape
    tok = pl.BlockSpec((tm, d), lambda i, c: (i, 0))
    groups = tm // V7X_LANES
    by_i2 = pl.BlockSpec((groups, PEER_HEADS, PEER_N_KEYS, V7X_LANES), lambda i, c: (i, 0, 0, 0))
    by_i1 = pl.BlockSpec((PEER_HEADS, te // PEER_N_KEYS, tm), lambda i, c: (0, c, i))
    wts = pl.BlockSpec((None, te, d), lambda i, c: (layer, c, 0))
    return pl.pallas_call(
        _peer_expert_kernel,
        grid=(n // tm, u.shape[1] // te),
        in_specs=[tok, tok, by_i2, by_i1, by_i1, wts, wts],
        out_specs=tok,
        out_shape=jax.ShapeDtypeStruct((n, d), F32),
        scratch_shapes=[pltpu.VMEM((te, tm), BF16)],
        compiler_params=_params("arbitrary", "arbitrary"),
        name="peer_experts",
    )(xn, x, e2r, cnt, e1n, u, v)


def peer_layer(x, g, w_q, keys, u, v, layer, *, tm, te):
    xn, e2r, cnt, e1n = peer_query(x, g, w_q, keys, layer, tm=tm)
    return peer_experts(xn, x, e2r, cnt, e1n, u, v, layer, tm=tm, te=te)


def _residue_order(tm, dil):
    return np.arange(tm).reshape(tm // dil, dil).T.reshape(tm)


def _qkv_kernel(x_ref, g_ref, w_ref, qg_ref, kg_ref, cos_ref, sin_ref, o_ref, kv_ref, xs_ref, xp_ref, kvf_ref,
                *, dil, tiles, tail_tile):
    tm, d = x_ref.shape
    hd = w_ref.shape[1] // 3
    per_res = tm // dil
    if dil > 1:
        for c in range(d // V7X_LANES):
            xs_ref[c] = x_ref[:, c * V7X_LANES:(c + 1) * V7X_LANES]
        for res in range(dil):
            for c in range(d // V7X_LANES):
                xp_ref[res * per_res:(res + 1) * per_res, c * V7X_LANES:(c + 1) * V7X_LANES] = (
                    xs_ref[c, pl.ds(res, per_res, stride=dil), :])
        x = xp_ref[...]
    else:
        x = x_ref[...]
    xn = _rms_norm(x, g_ref[...]).astype(BF16)
    cos, sin = cos_ref[...], sin_ref[...]

    def store(col0, val):
        for res in range(dil):
            piece = val[res * per_res:(res + 1) * per_res, :]
            o_ref[res, :, col0:col0 + val.shape[1]] = piece.astype(BF16)
            if col0 >= hd:
                kvf_ref[res, :, col0 - hd:col0 - hd + val.shape[1]] = piece

    for kind, gain_ref in enumerate((qg_ref, kg_ref, None)):
        y = jnp.dot(xn, w_ref[:, kind * hd:(kind + 1) * hd], preferred_element_type=F32)
        if gain_ref is None:
            store(kind * hd, y)
            continue
        gain = gain_ref[...]
        for h in range(hd // HEAD_DIM):
            yn = _rms_norm(y[:, h * HEAD_DIM:(h + 1) * HEAD_DIM], gain)
            store(kind * hd + h * HEAD_DIM, yn * cos + pltpu.roll(yn, HEAD_DIM // 2, axis=1) * sin)

    @pl.when(pl.program_id(0) % tiles >= tail_tile)
    def _():
        rows = kv_ref.shape[0]
        if dil == 1:
            kv_ref[...] = kvf_ref[0, tm - rows:, :]
            return
        slabs = xs_ref.shape[0]
        for half in range(2 * hd // (slabs * V7X_LANES)):
            col0 = half * slabs * V7X_LANES
            for res in range(dil):
                for c in range(slabs):
                    xs_ref[c, pl.ds(res, per_res, stride=dil), :] = (
                        kvf_ref[res, :, col0 + c * V7X_LANES:col0 + (c + 1) * V7X_LANES])
            for c in range(slabs):
                kv_ref[:, col0 + c * V7X_LANES:col0 + (c + 1) * V7X_LANES] = xs_ref[c]


def qkv_project(x, g, w_qkv, q_gain, k_gain, pos, group, *, dil, batch, tm, keep):
    n, d = x.shape
    seq = n // batch
    hd3 = w_qkv.shape[1] // N_GROUPS
    tiles = seq // tm
    per_res = tm // dil
    assert per_res % (2 * V7X_SUBLANES) == 0
    kv_rows = min(keep, tm)
    assert keep % kv_rows == 0 and (dil == 1 or kv_rows == tm)
    tail_tile = tiles - keep // kv_rows
    out_specs = [pl.BlockSpec((None, dil, per_res, hd3), lambda i: (i // tiles, 0, i % tiles, 0)),
                 pl.BlockSpec((None, kv_rows, 2 * hd3 // 3),
                              lambda i: (i // tiles, jnp.maximum(i % tiles - tail_tile, 0), 0))]
    out_shape = [jax.ShapeDtypeStruct((batch, dil, seq // dil, hd3), BF16),
                 jax.ShapeDtypeStruct((batch, keep, 2 * hd3 // 3), F32)]
    order = (np.arange(tiles)[:, None] * tm + _residue_order(tm, dil)[None, :]).reshape(seq)
    cos, sin = _rope_tables(jnp.asarray(np.asarray(pos)[order]))
    gain = pl.BlockSpec((None, 1, HEAD_DIM), lambda i: (group, 0, 0))
    rope = pl.BlockSpec((tm, HEAD_DIM), lambda i: (i % tiles, 0))
    return pl.pallas_call(
        functools.partial(_qkv_kernel, dil=dil, tiles=tiles, tail_tile=tail_tile),
        grid=(n // tm,),
        in_specs=[
            pl.BlockSpec((tm, d), lambda i: (i, 0)),
            pl.BlockSpec((1, d), lambda i: (0, 0)),
            pl.BlockSpec((d, hd3), lambda i: (0, group), pipeline_mode=pl.Buffered(1)),
            gain, gain, rope, rope,
        ],
        out_specs=out_specs,
        out_shape=out_shape,
        scratch_shapes=[pltpu.VMEM((d // V7X_LANES, tm, V7X_LANES), F32), pltpu.VMEM((tm, d), F32),
                        pltpu.VMEM((dil, per_res, 2 * hd3 // 3), F32)],
        compiler_params=_params("arbitrary"),
        name=f"qkv_project_g{group}_d{dil}",
    )(x, g, w_qkv, q_gain.reshape(N_GROUPS, 1, HEAD_DIM), k_gain.reshape(N_GROUPS, 1, HEAD_DIM), cos, sin)


def _spread_heads(cols_list, width):
    rows = cols_list[0].shape[0]
    lane = lax.broadcasted_iota(jnp.int32, (rows, width), 1)
    out = jnp.zeros((rows, width), F32)
    for h, col in enumerate(cols_list):
        out = jnp.where(lane == h, col, out)
    return out


def _attn_prompt_kernel(q_ref, kc_ref, kp_ref, vc_ref, vp_ref, o_ref, lse_ref, *, window_blocks):
    i = pl.program_id(2)
    qb = q_ref.shape[0]
    qi = lax.broadcasted_iota(jnp.int32, (qb, 2 * qb), 0)
    kj = lax.broadcasted_iota(jnp.int32, (qb, 2 * qb), 1)
    live = (kj <= qi + qb) & (kj >= qi + qb - window_blocks) & ((i > 0) | (kj >= qb))
    lses = []
    for h in range(q_ref.shape[1] // HEAD_DIM):
        cols = slice(h * HEAD_DIM, (h + 1) * HEAD_DIM)
        q = q_ref[:, cols].astype(BF16)
        k = jnp.concatenate([kp_ref[:, cols], kc_ref[:, cols]], axis=0).astype(BF16)
        v = jnp.concatenate([vp_ref[:, cols], vc_ref[:, cols]], axis=0).astype(BF16)
        s = lax.dot_general(q, k, NT_DIMS, preferred_element_type=F32) * ATTN_SCALE
        s = jnp.where(live, s, NEG_INF)
        m = jnp.max(s, axis=-1, keepdims=True)
        p = jnp.exp(s - m)
        l = jnp.sum(p, axis=-1, keepdims=True)
        o_ref[:, cols] = jnp.dot(p.astype(BF16), v, preferred_element_type=F32) / l
        lses.append(m + jnp.log(l))
    lse_ref[...] = _spread_heads(lses, lse_ref.shape[1])


def attn_prompt_group(qkv, group):
    batch, dil, n_strided, hd3 = qkv.shape
    hd = hd3 // 3
    win = WINDOWS[group]
    assert dil == DILATIONS[group] and n_strided % ATTN_BLOCK == 0 and win // dil <= ATTN_BLOCK
    blk = (None, None, ATTN_BLOCK, hd)

    def cur(kind):
        return pl.BlockSpec(blk, lambda b, r, i: (b, r, i, kind))

    def prev(kind):
        return pl.BlockSpec(blk, lambda b, r, i: (b, r, jnp.maximum(i - 1, 0), kind))

    return pl.pallas_call(
        functools.partial(_attn_prompt_kernel, window_blocks=win // dil),
        grid=(batch, dil, n_strided // ATTN_BLOCK),
        in_specs=[cur(0), cur(1), prev(1), cur(2), prev(2)],
        out_specs=[pl.BlockSpec(blk, lambda b, r, i: (b, r, i, 0)),
                   pl.BlockSpec((None, None, ATTN_BLOCK, V7X_LANES), lambda b, r, i: (b, r, i, 0))],
        out_shape=[jax.ShapeDtypeStruct((batch, dil, n_strided, hd), F32),
                   jax.ShapeDtypeStruct((batch, dil, n_strided, V7X_LANES), F32)],
        compiler_params=_params("arbitrary", "arbitrary", "arbitrary"),
        name=f"attn_prompt_g{group}",
    )(qkv, qkv, qkv, qkv, qkv)


def _head_segments(n_heads, transpose):
    shape = (V7X_LANES, n_heads * HEAD_DIM) if transpose else (n_heads * HEAD_DIM, V7X_LANES)
    c = lax.broadcasted_iota(jnp.int32, shape, 1 if transpose else 0)
    h = lax.broadcasted_iota(jnp.int32, shape, 0 if transpose else 1)
    return jnp.where((c >= h * HEAD_DIM) & (c < (h + 1) * HEAD_DIM), 1.0, 0.0).astype(BF16)


def _attn_sample_kernel(n0_ref, n1_ref, n2_ref, c0_ref, c1_ref, c2_ref, o_ref, lse_ref):
    t = pl.program_id(1)
    for g, (new_ref, cache_ref) in enumerate(zip((n0_ref, n1_ref, n2_ref), (c0_ref, c1_ref, c2_ref))):
        q = new_ref[pl.ds(t, 1), 0]
        k_new, v_new = new_ref[:, 1], new_ref[:, 2]
        k_old, v_old = cache_ref[:, 0], cache_ref[:, 1]
        s_old = jnp.sum(k_old * q, axis=-1, keepdims=True) * ATTN_SCALE
        s_new = jnp.sum(k_new * q, axis=-1, keepdims=True) * ATTN_SCALE
        row_n = lax.broadcasted_iota(jnp.int32, s_new.shape, 0)
        if DILATIONS[g] == 1:
            row_c = lax.broadcasted_iota(jnp.int32, s_old.shape, 0)
            s_old = jnp.where(row_c >= t, s_old, NEG_INF)
            s_new = jnp.where(row_n <= t, s_new, NEG_INF)
        else:
            s_new = jnp.where(row_n == t, s_new, NEG_INF)
        m = jnp.maximum(jnp.max(s_old, axis=0, keepdims=True), jnp.max(s_new, axis=0, keepdims=True))
        p_old, p_new = jnp.exp(s_old - m), jnp.exp(s_new - m)
        l = jnp.sum(p_old, axis=0, keepdims=True) + jnp.sum(p_new, axis=0, keepdims=True)
        o = jnp.sum(p_old * v_old, axis=0, keepdims=True) + jnp.sum(p_new * v_new, axis=0, keepdims=True)
        o_ref[g, pl.ds(t, 1)] = o / l
        lse_ref[g, pl.ds(t, 1)] = m + jnp.log(l)


def attn_sample(qkv, caches, *, batch, t_new):
    n_heads = caches[0].shape[3]
    views, specs = [], []
    for g, cache in enumerate(caches):
        dil, keep = DILATIONS[g], cache.shape[1]
        assert keep == WINDOWS[g] and keep // dil == ATTN_BLOCK and (dil == 1 or t_new <= dil)
        views.append(cache.reshape(batch, keep // dil, dil, 2, n_heads, HEAD_DIM))
        specs.append(pl.BlockSpec((None, ATTN_BLOCK, None, 2, n_heads, HEAD_DIM),
                                  (lambda b, t: (b, 0, 0, 0, 0, 0)) if dil == 1 else
                                  (lambda b, t: (b, 0, t, 0, 0, 0))))
    rows_new = -(-t_new // V7X_SUBLANES) * V7X_SUBLANES
    new_rows = [jnp.pad(a.reshape(batch, t_new, 3, n_heads, HEAD_DIM),
                        ((0, 0), (0, rows_new - t_new), (0, 0), (0, 0), (0, 0))) for a in qkv]
    o, lse = pl.pallas_call(
        _attn_sample_kernel,
        grid=(batch, t_new),
        in_specs=[pl.BlockSpec((None, rows_new, 3, n_heads, HEAD_DIM), lambda b, t: (b, 0, 0, 0, 0))] * N_GROUPS
        + specs,
        out_specs=[pl.BlockSpec((N_GROUPS, None, t_new, n_heads, HEAD_DIM), lambda b, t: (0, b, 0, 0, 0)),
                   pl.BlockSpec((N_GROUPS, None, t_new, n_heads, 1), lambda b, t: (0, b, 0, 0, 0))],
        out_shape=[jax.ShapeDtypeStruct((N_GROUPS, batch, t_new, n_heads, HEAD_DIM), F32),
                   jax.ShapeDtypeStruct((N_GROUPS, batch, t_new, n_heads, 1), F32)],
        compiler_params=_params("arbitrary", "arbitrary"),
        name="attn_sample",
    )(*new_rows, *views)
    lse = jnp.pad(lse.reshape(N_GROUPS, batch * t_new, n_heads), ((0, 0), (0, 0), (0, V7X_LANES - n_heads)))
    return o.reshape(N_GROUPS, batch * t_new, n_heads * HEAD_DIM), lse


def _attn_out_kernel(x_ref, o0_ref, o1_ref, o2_ref, l0_ref, l1_ref, l2_ref, wo_ref, y_ref,
                     og_ref, lg_ref, mix_ref):
    tm = x_ref.shape[0]
    hd = wo_ref.shape[0]
    heads = hd // HEAD_DIM
    for g, (o_ref, l_ref) in enumerate(zip((o0_ref, o1_ref, o2_ref), (l0_ref, l1_ref, l2_ref))):
        dil = o_ref.shape[0]
        for res in range(dil):
            rows = pl.ds(res, tm // dil, stride=dil) if dil > 1 else slice(0, tm)
            lg_ref[g, rows, :] = l_ref[res]
            for h in range(heads):
                og_ref[g, h, rows, :] = o_ref[res, :, h * HEAD_DIM:(h + 1) * HEAD_DIM]
    lse = [lg_ref[g] for g in range(N_GROUPS)]
    m = functools.reduce(jnp.maximum, lse)
    e = [jnp.exp(l - m) for l in lse]
    tot = functools.reduce(jnp.add, e)
    seg_t = _head_segments(heads, transpose=True)
    w = [_split_dot(e[g] / tot, seg_t) for g in range(N_GROUPS)]
    for h in range(heads):
        cols = slice(h * HEAD_DIM, (h + 1) * HEAD_DIM)
        mix = functools.reduce(jnp.add, [w[g][:, cols] * og_ref[g, h] for g in range(N_GROUPS)])
        mix_ref[:, cols] = mix.astype(BF16)
    y_ref[...] = x_ref[...] + jnp.dot(mix_ref[...], wo_ref[...], preferred_element_type=F32)


def attn_out(x, outs, lses, w_o, *, tm):
    n, d = x.shape
    batch = outs[0].shape[0]
    hd = outs[0].shape[3]
    tiles = n // batch // tm

    def by_residue(a):
        dil = a.shape[1]
        return pl.BlockSpec((None, dil, tm // dil, a.shape[3]), lambda i: (i // tiles, 0, i % tiles, 0))

    return pl.pallas_call(
        _attn_out_kernel,
        grid=(n // tm,),
        in_specs=[pl.BlockSpec((tm, d), lambda i: (i, 0))] + [by_residue(a) for a in (*outs, *lses)]
        + [pl.BlockSpec((hd, d), lambda i: (0, 0))],
        out_specs=pl.BlockSpec((tm, d), lambda i: (i, 0)),
        out_shape=jax.ShapeDtypeStruct((n, d), F32),
        scratch_shapes=[pltpu.VMEM((N_GROUPS, hd // HEAD_DIM, tm, HEAD_DIM), F32),
                        pltpu.VMEM((N_GROUPS, tm, V7X_LANES), F32), pltpu.VMEM((tm, hd), BF16)],
        compiler_params=_params("arbitrary"),
        name="attn_out",
    )(x, *outs, *lses, w_o)


def _kv_state_kernel(c_ref, n_ref, o_ref):
    i = pl.program_id(1)
    last = pl.num_programs(1) - 1
    rows, t_new = o_ref.shape[0], n_ref.shape[0]

    @pl.when(i < last)
    def _():
        o_ref[...] = c_ref[0]

    @pl.when(i == last)
    def _():
        o_ref[0:rows - t_new] = c_ref[0, t_new:rows]
        o_ref[rows - t_new:rows] = n_ref[...]


def kv_state_update(cache, new, *, rows):
    batch, keep = cache.shape[:2]
    t_new = new.shape[1]
    tail = cache.shape[2:]
    blocks = keep // rows
    assert keep % rows == 0 and rows > t_new and new.shape[2:] == tail
    zeros = (0,) * len(tail)

    def shifted(b, i):
        return (b, jnp.where(i < blocks - 1, i * rows + t_new, keep - rows)) + zeros

    return pl.pallas_call(
        _kv_state_kernel,
        grid=(batch, blocks),
        in_specs=[pl.BlockSpec(tuple(pl.Element(size) for size in (1, rows) + tail), shifted),
                  pl.BlockSpec((None, t_new) + tail, lambda b, i: (b, 0) + zeros)],
        out_specs=pl.BlockSpec((None, rows) + tail, lambda b, i: (b, i) + zeros),
        out_shape=jax.ShapeDtypeStruct(cache.shape, cache.dtype),
        compiler_params=_params("arbitrary", "arbitrary"),
        name="kv_state_update",
    )(cache, new)


def _rope_tables(pos):
    half = HEAD_DIM // 2
    inv_freq = ROPE_THETA ** (-jnp.arange(half, dtype=F32) / half)
    ang = pos.astype(F32)[:, None] * inv_freq
    cos, sin = jnp.cos(ang), jnp.sin(ang)
    return jnp.concatenate([cos, cos], axis=-1), jnp.concatenate([-sin, sin], axis=-1)


def _pad_rows(a, rows):
    return jnp.pad(a, ((0, rows - a.shape[0]),) + ((0, 0),) * (a.ndim - 1))


PROMPT_TM = 512
ATTN_OUT_TM = 256
QKV_TM = 256
PEER_TE = 1024
CONV_TN = 1024
SAMPLE_PEER_ROWS = V7X_LANES


def kernel(x_prompt, x_sample, state_conv, cache_kv_w128, cache_kv_w512, cache_kv_w2048,
           norm_mix, norm_ffn, conv_w_in, conv_w, conv_w_out,
           attn_w_qkv, attn_q_norm, attn_k_norm, attn_w_o,
           peer_w_q, peer_sub_keys, peer_u, peer_v):
    batch, seq, d = x_prompt.shape
    dec_batch, dec_seq, _ = x_sample.shape
    n_p, n_s = batch * seq, dec_batch * dec_seq
    caches = (cache_kv_w128, cache_kv_w512, cache_kv_w2048)
    n_heads = caches[0].shape[3]
    hd = n_heads * HEAD_DIM

    w_in, w_out = conv_w_in.astype(BF16), conv_w_out.astype(BF16)
    w_qkv, w_o = attn_w_qkv.astype(BF16), attn_w_o.astype(BF16)
    w_q = peer_w_q.astype(BF16)
    keys = peer_sub_keys.astype(BF16).reshape(peer_sub_keys.shape[0], 2 * PEER_HEADS, PEER_N_KEYS, PEER_D_HALF)
    pu, pv = peer_u.astype(BF16), peer_v.astype(BF16)
    g_mix = norm_mix.reshape(-1, 1, d)
    g_ffn = norm_ffn.reshape(-1, 1, d)

    xp = x_prompt.reshape(n_p, d)
    xs = x_sample.reshape(n_s, d)

    def peer(x, layer, tm):
        return peer_layer(x, g_ffn[layer], w_q, keys, pu, pv, layer, tm=tm, te=PEER_TE)

    def peer_sample(x, layer):
        return peer(_pad_rows(x, SAMPLE_PEER_ROWS), layer, SAMPLE_PEER_ROWS)[:n_s]

    bp, up = conv_in(xp, g_mix[0], w_in, tm=PROMPT_TM, tn=CONV_TN)
    xp = conv_out_prompt(xp, bp, up, jnp.zeros((batch, CONV_WIDTH - 1, d), F32), conv_w, w_out,
                         seq_len=seq, tm=PROMPT_TM)
    conv_state_prompt = up.reshape(batch, seq, d)[:, seq - (CONV_WIDTH - 1):]

    bs, us = conv_in(xs, g_mix[0], w_in, tm=n_s, tn=CONV_TN)
    u_ext = jnp.concatenate([state_conv, us.reshape(dec_batch, dec_seq, d)], axis=1)
    shifted = [u_ext[:, k:k + dec_seq].reshape(n_s, d) for k in range(CONV_WIDTH)]
    xs = conv_out_sample(xs, bs, shifted[0], shifted[1], shifted[2], conv_w, w_out)
    conv_state_sample = u_ext[:, dec_seq:]

    xp = peer(xp, 0, PROMPT_TM)
    xs = peer_sample(xs, 0)

    qkv_p, kv_prompt = zip(*[
        qkv_project(xp, g_mix[1], w_qkv, attn_q_norm, attn_k_norm, np.arange(seq), g,
                    dil=DILATIONS[g], batch=batch, tm=QKV_TM, keep=min(WINDOWS[g], seq)) for g in range(N_GROUPS)])
    outs, lses = zip(*[attn_prompt_group(qkv_p[g], g) for g in range(N_GROUPS)])
    xp = attn_out(xp, outs, lses, w_o, tm=ATTN_OUT_TM)

    pos_s = np.tile(PAST_LEN + np.arange(dec_seq), dec_batch)
    q_s, kv_s = zip(*[qkv_project(xs, g_mix[1], w_qkv, attn_q_norm, attn_k_norm, pos_s, g,
                                  dil=1, batch=1, tm=n_s, keep=n_s) for g in range(N_GROUPS)])
    kv_s = [kv.reshape(n_s, 2 * hd) for kv in kv_s]
    qkv_s = [jnp.concatenate([q.reshape(n_s, 3 * hd)[:, :hd].astype(F32), kv], axis=1) for q, kv in zip(q_s, kv_s)]
    o_s, lse_s = attn_sample(qkv_s, caches, batch=dec_batch, t_new=dec_seq)
    xs = attn_out(xs, [o_s[g].reshape(1, 1, n_s, hd) for g in range(N_GROUPS)],
                  [lse_s[g].reshape(1, 1, n_s, V7X_LANES) for g in range(N_GROUPS)], w_o, tm=n_s)

    xp = peer(xp, 1, PROMPT_TM)
    xs = peer_sample(xs, 1)

    kv_prompt = [kv.reshape(batch, kv.shape[1], 2, n_heads, HEAD_DIM) for kv in kv_prompt]
    kv_sample = [kv_state_update(cache, kv.reshape(dec_batch, dec_seq, 2, n_heads, HEAD_DIM), rows=ATTN_BLOCK)
                 for cache, kv in zip(caches, kv_s)]

    return (xp.reshape(batch, seq, d), xs.reshape(dec_batch, dec_seq, d),
            conv_state_prompt, conv_state_sample, *kv_prompt, *kv_sample)
```

```python
import functools
import math

import jax
import jax.numpy as jnp
import numpy as np
from jax import lax
from jax.experimental import pallas as pl
from jax.experimental.pallas import tpu as pltpu

F32 = jnp.float32
BF16 = jnp.bfloat16

V7X_LANES = 128
V7X_SUBLANES = 8
V7X_VMEM_BYTES = 64 * 1024 * 1024
VMEM_LIMIT_BYTES = V7X_VMEM_BYTES - 8 * 1024 * 1024

NORM_EPS = 1e-6
CONV_WIDTH = 3
WINDOWS = (128, 512, 2048)
DILATIONS = (1, 4, 16)
N_GROUPS = 3
HEAD_DIM = 128
ATTN_BLOCK = 128
ATTN_SCALE = HEAD_DIM ** -0.5
ROPE_THETA = 10000.0
PAST_LEN = 16384
assert PAST_LEN >= max(WINDOWS)
PEER_HEADS = 8
PEER_N_KEYS = 128
PEER_TOPK = 16
PEER_D_HALF = 128
SQRT_HALF = math.sqrt(0.5)
RANK_MASK = 31
NOT_SELECTED = float(RANK_MASK)
NEG_INF = float("-inf")

NT_DIMS = (((1,), (1,)), ((), ()))
TN_DIMS = (((0,), (0,)), ((), ()))


def _params(*semantics):
    return pltpu.CompilerParams(dimension_semantics=semantics, vmem_limit_bytes=VMEM_LIMIT_BYTES)


def _rms_norm(x, g):
    return x * lax.rsqrt(jnp.mean(x * x, axis=-1, keepdims=True) + NORM_EPS) * g


def _split_dot(a, b01):
    hi = a.astype(BF16)
    lo = (a - hi.astype(F32)).astype(BF16)
    return (jnp.dot(hi, b01, preferred_element_type=F32)
            + jnp.dot(lo, b01, preferred_element_type=F32))


def _conv_in_kernel(x_ref, g_ref, wb_ref, wc_ref, wh_ref, b_ref, u_ref, xn_ref):
    @pl.when(pl.program_id(1) == 0)
    def _():
        xn_ref[...] = _rms_norm(x_ref[...], g_ref[...]).astype(BF16)

    xn = xn_ref[...]
    b_ref[...] = jnp.dot(xn, wb_ref[...], preferred_element_type=F32)
    c = jnp.dot(xn, wc_ref[...], preferred_element_type=F32)
    h = jnp.dot(xn, wh_ref[...], preferred_element_type=F32)
    u_ref[...] = c * h


def conv_in(x, g, w_in, *, tm, tn):
    n, d = x.shape
    nj = d // tn
    return pl.pallas_call(
        _conv_in_kernel,
        grid=(n // tm, nj),
        in_specs=[
            pl.BlockSpec((tm, d), lambda i, j: (i, 0)),
            pl.BlockSpec((1, d), lambda i, j: (0, 0)),
            pl.BlockSpec((d, tn), lambda i, j: (0, j)),
            pl.BlockSpec((d, tn), lambda i, j: (0, j + nj)),
            pl.BlockSpec((d, tn), lambda i, j: (0, j + 2 * nj)),
        ],
        out_specs=[pl.BlockSpec((tm, tn), lambda i, j: (i, j)),
                   pl.BlockSpec((tm, tn), lambda i, j: (i, j))],
        out_shape=[jax.ShapeDtypeStruct((n, d), F32), jax.ShapeDtypeStruct((n, d), F32)],
        scratch_shapes=[pltpu.VMEM((tm, d), BF16)],
        compiler_params=_params("arbitrary", "arbitrary"),
        name="conv_in",
    )(x, g, w_in, w_in, w_in)


def _conv_mix(cw_ref, um2, um1, u0):
    return cw_ref[0:1, :] * um2 + cw_ref[1:2, :] * um1 + cw_ref[2:3, :] * u0


def _conv_out_prompt_kernel(x_ref, b_ref, u_ref, up_ref, st_ref, cw_ref, wo_ref, o_ref, *, tiles_per_seq):
    i = pl.program_id(0)
    u = u_ref[...]
    first = (i % tiles_per_seq) == 0
    prev = jnp.where(first, st_ref[...], up_ref[V7X_SUBLANES - 2:V7X_SUBLANES, :])
    row = lax.broadcasted_iota(jnp.int32, (u.shape[0], 1), 0)
    um1 = jnp.where(row == 0, prev[1:2, :], pltpu.roll(u, 1, axis=0))
    um2 = jnp.where(row == 0, prev[0:1, :], jnp.where(row == 1, prev[1:2, :], pltpu.roll(u, 2, axis=0)))
    a = (b_ref[...] * _conv_mix(cw_ref, um2, um1, u)).astype(BF16)
    o_ref[...] = x_ref[...] + jnp.dot(a, wo_ref[...], preferred_element_type=F32)


def conv_out_prompt(x, bgate, u, state, conv_w, w_out, *, seq_len, tm):
    n, d = x.shape
    tps = seq_len // tm
    sub_per_tile = tm // V7X_SUBLANES
    tile = pl.BlockSpec((tm, d), lambda i: (i, 0))
    return pl.pallas_call(
        functools.partial(_conv_out_prompt_kernel, tiles_per_seq=tps),
        grid=(n // tm,),
        in_specs=[
            tile, tile, tile,
            pl.BlockSpec((V7X_SUBLANES, d), lambda i: (jnp.maximum(i * sub_per_tile - 1, 0), 0)),
            pl.BlockSpec((None, CONV_WIDTH - 1, d), lambda i: (i // tps, 0, 0)),
            pl.BlockSpec((CONV_WIDTH, d), lambda i: (0, 0)),
            pl.BlockSpec((d, d), lambda i: (0, 0)),
        ],
        out_specs=tile,
        out_shape=jax.ShapeDtypeStruct((n, d), F32),
        compiler_params=_params("arbitrary"),
        name="conv_out_prompt",
    )(x, bgate, u, u, state, conv_w, w_out)


def _conv_out_sample_kernel(x_ref, b_ref, um2_ref, um1_ref, u_ref, cw_ref, wo_ref, o_ref):
    a = (b_ref[...] * _conv_mix(cw_ref, um2_ref[...], um1_ref[...], u_ref[...])).astype(BF16)
    o_ref[...] = x_ref[...] + jnp.dot(a, wo_ref[...], preferred_element_type=F32)


def conv_out_sample(x, bgate, um2, um1, u0, conv_w, w_out):
    n, d = x.shape
    full = pl.BlockSpec((n, d), lambda i: (0, 0))
    return pl.pallas_call(
        _conv_out_sample_kernel,
        grid=(1,),
        in_specs=[full, full, full, full, full,
                  pl.BlockSpec((CONV_WIDTH, d), lambda i: (0, 0)),
                  pl.BlockSpec((d, d), lambda i: (0, 0))],
        out_specs=full,
        out_shape=jax.ShapeDtypeStruct((n, d), F32),
        compiler_params=_params("arbitrary"),
        name="conv_out_sample",
    )(x, bgate, um2, um1, u0, conv_w, w_out)


CAND_ROWS = 72


def _cand_flat_index():
    pairs = []
    for b in range(4):
        pairs += [(a, b) for a in range(16 if b == 0 else 8)]
    pairs += [(0, b) if b >= 4 else None for b in range(16)]
    pairs += [(1, b) if b >= 4 else None for b in range(8)]
    pairs += [(2, b) if b == 4 else None for b in range(8)]
    rows = [a * PEER_TOPK + b if pair is not None and (a + 1) * (b + 1) <= PEER_TOPK else 256 + i
            for i, pair in enumerate(pairs) for a, b in [pair or (99, 99)]]
    assert len(rows) == CAND_ROWS and len({r for r in rows if r < 256}) == 50
    return jnp.broadcast_to(jnp.asarray(rows, F32).reshape(CAND_ROWS, 1), (CAND_ROWS, V7X_LANES))


def _top16(s, key_iota, exact_ties):
    work = s
    rank = jnp.full(s.shape, NOT_SELECTED, F32)
    val_row = lax.broadcasted_iota(jnp.int32, (PEER_TOPK, s.shape[1]), 0)
    vals = jnp.zeros((PEER_TOPK, s.shape[1]), F32)
    for a in range(PEER_TOPK):
        m = jnp.max(work, axis=0, keepdims=True)
        sel = work == m
        if exact_ties:
            first = jnp.min(jnp.where(sel, key_iota, float(PEER_N_KEYS)), axis=0, keepdims=True)
            sel = key_iota == first
        rank = jnp.where(sel, float(a), rank)
        work = jnp.where(sel, NEG_INF, work)
        vals = jnp.where(val_row == a, m, vals)
    return vals, rank


def _ranked_rows(rank):
    return jnp.sum(jnp.where(rank < NOT_SELECTED, 1.0, 0.0), axis=0, keepdims=True)


def _peer_select(s1, s2, fidx, exact_ties):
    lanes = s1.shape[1]
    key_iota = lax.broadcasted_iota(jnp.int32, (PEER_N_KEYS, lanes), 0).astype(F32)
    v1, rank1 = _top16(s1, key_iota, exact_ties)
    v2, rank2 = _top16(s2, key_iota, exact_ties)

    r16 = lax.broadcasted_iota(jnp.int32, (16, lanes), 0)
    r8 = lax.broadcasted_iota(jnp.int32, (8, lanes), 0)
    v1_lo, v2_lo = v1[0:8, :], v2[0:8, :]
    cand = jnp.concatenate([
        v1 + v2[0:1, :],
        v1_lo + v2[1:2, :],
        jnp.where(r8 < 5, v1_lo + v2[2:3, :], NEG_INF),
        jnp.where(r8 < 4, v1_lo + v2[3:4, :], NEG_INF),
        jnp.where(r16 >= 4, v1[0:1, :] + v2, NEG_INF),
        jnp.where(r8 >= 4, v1[1:2, :] + v2_lo, NEG_INF),
        jnp.where(r8 == 4, v1[2:3, :] + v2_lo, NEG_INF),
    ], axis=0)
    c_top = v1[0:1, :] + v2[0:1, :]
    taken = jnp.zeros(cand.shape, F32)
    z = jnp.zeros((1, lanes), F32)
    for _ in range(PEER_TOPK):
        m = jnp.max(cand, axis=0, keepdims=True)
        sel = cand == m
        if exact_ties:
            first = jnp.min(jnp.where(sel, fidx, 1024.0), axis=0, keepdims=True)
            sel = fidx == first
        taken = jnp.where(sel, 1.0, taken)
        cand = jnp.where(sel, NEG_INF, cand)
        z = z + jnp.exp(m - c_top)

    extra = jnp.where(r8 == 0, jnp.sum(taken[40:56, :], axis=0, keepdims=True),
                      jnp.where(r8 == 1, jnp.sum(taken[56:64, :], axis=0, keepdims=True),
                                jnp.where(r8 == 2, jnp.sum(taken[64:72, :], axis=0, keepdims=True), 0.0)))
    cnt_lo = taken[0:8, :] + taken[16:24, :] + taken[24:32, :] + taken[32:40, :] + extra
    cnt_vec = jnp.concatenate([cnt_lo, taken[8:16, :]], axis=0)
    cnt = jnp.zeros(s1.shape, F32)
    for a in range(PEER_TOPK):
        cnt = jnp.where(rank1 == float(a), cnt_vec[a:a + 1, :], cnt)

    e1n = jnp.exp(s1 - v1[0:1, :]) / z
    e2 = jnp.exp(s2 - v2[0:1, :])
    ranked = jnp.maximum(jnp.maximum(_ranked_rows(rank1), _ranked_rows(rank2)),
                         jnp.sum(taken, axis=0, keepdims=True))
    return rank2, e2, cnt, e1n, ranked


def _peer_query_kernel(x_ref, g_ref, wq_ref, keys_ref, fidx_ref,
                       xn_ref, e2r_ref, cnt_ref, e1n_ref, sc_ref):
    tm = x_ref.shape[0]
    xn = _rms_norm(x_ref[...], g_ref[...]).astype(BF16)
    xn_ref[...] = xn
    q = jnp.dot(xn, wq_ref[...], preferred_element_type=F32).astype(BF16)
    for rp in range(2 * PEER_HEADS):
        q_rp = q[:, rp * PEER_D_HALF:(rp + 1) * PEER_D_HALF]
        sc_ref[rp] = lax.dot_general(keys_ref[rp], q_rp, NT_DIMS, preferred_element_type=F32)

    fidx = fidx_ref[...]

    def select_head(r, exact_ties):
        ranked = jnp.zeros((1, V7X_LANES), F32)
        for s in range(tm // V7X_LANES):
            cols = slice(s * V7X_LANES, (s + 1) * V7X_LANES)
            rank2, e2, cnt, e1n, n = _peer_select(sc_ref[2 * r, :, cols], sc_ref[2 * r + 1, :, cols], fidx,
                                                  exact_ties)
            e2_bits = pltpu.bitcast(e2, jnp.int32) & ~RANK_MASK
            e2r_ref[s, r] = pltpu.bitcast(e2_bits | rank2.astype(jnp.int32), F32)
            cnt_ref[r, :, cols] = cnt
            e1n_ref[r, :, cols] = e1n
            ranked = jnp.maximum(ranked, n)
        return ranked

    def head_body(r, carry):
        ranked = select_head(r, exact_ties=False)

        @pl.when(jnp.max(ranked) > float(PEER_TOPK))
        def _():
            select_head(r, exact_ties=True)

        return carry

    lax.fori_loop(0, PEER_HEADS, head_body, 0)


def peer_query(x, g, w_q, keys, layer, *, tm):
    n, d = x.shape
    dq = w_q.shape[2]
    groups = tm // V7X_LANES
    by_i2 = jax.ShapeDtypeStruct((n // V7X_LANES, PEER_HEADS, PEER_N_KEYS, V7X_LANES), F32)
    by_i2_spec = pl.BlockSpec((groups, PEER_HEADS, PEER_N_KEYS, V7X_LANES), lambda i: (i, 0, 0, 0))
    sel = jax.ShapeDtypeStruct((PEER_HEADS, PEER_N_KEYS, n), F32)
    sel_spec = pl.BlockSpec((PEER_HEADS, PEER_N_KEYS, tm), lambda i: (0, 0, i))
    return pl.pallas_call(
        _peer_query_kernel,
        grid=(n // tm,),
        in_specs=[
            pl.BlockSpec((tm, d), lambda i: (i, 0)),
            pl.BlockSpec((1, d), lambda i: (0, 0)),
            pl.BlockSpec((None, d, dq), lambda i: (layer, 0, 0)),
            pl.BlockSpec((None, 2 * PEER_HEADS, PEER_N_KEYS, PEER_D_HALF), lambda i: (layer, 0, 0, 0)),
            pl.BlockSpec((CAND_ROWS, V7X_LANES), lambda i: (0, 0)),
        ],
        out_specs=[pl.BlockSpec((tm, d), lambda i: (i, 0)), by_i2_spec, sel_spec, sel_spec],
        out_shape=[jax.ShapeDtypeStruct((n, d), BF16), by_i2, sel, sel],
        scratch_shapes=[pltpu.VMEM((2 * PEER_HEADS, PEER_N_KEYS, tm), F32)],
        compiler_params=_params("arbitrary"),
        name="peer_query",
    )(x, g, w_q, keys, _cand_flat_index())


GATE_ROWS = 32


def _gate_chunk(h, a_ref, e2r_ref, cnt_ref, e1n_ref):
    te, tm = h.shape
    for k in range(te // PEER_N_KEYS):
        for s in range(tm // V7X_LANES):
            cols = slice(s * V7X_LANES, (s + 1) * V7X_LANES)
            cnt = [cnt_ref[r, k:k + 1, cols].astype(jnp.int32) for r in range(PEER_HEADS)]
            e1n = [e1n_ref[r, k:k + 1, cols] for r in range(PEER_HEADS)]
            for i2 in range(0, PEER_N_KEYS, GATE_ROWS):
                gate = jnp.zeros((GATE_ROWS, V7X_LANES), F32)
                for r in range(PEER_HEADS):
                    e2r = e2r_ref[s, r, i2:i2 + GATE_ROWS, :]
                    live = (pltpu.bitcast(e2r, jnp.int32) & RANK_MASK) < cnt[r]
                    gate = gate + jnp.where(live, e2r, 0.0) * e1n[r]
                row0 = k * PEER_N_KEYS + i2
                hp = h[row0:row0 + GATE_ROWS, cols]
                act = 0.5 * hp * (1.0 + lax.erf(hp * SQRT_HALF))
                a_ref[row0:row0 + GATE_ROWS, cols] = (gate * act).astype(BF16)


def _peer_expert_kernel(xn_ref, x_ref, e2r_ref, cnt_ref, e1n_ref, u_ref, v_ref, o_ref, a_ref):
    c = pl.program_id(1)

    @pl.when(c == 0)
    def _():
        o_ref[...] = x_ref[...]

    h = lax.dot_general(u_ref[...], xn_ref[...], NT_DIMS, preferred_element_type=F32)
    _gate_chunk(h, a_ref, e2r_ref, cnt_ref, e1n_ref)
    o_ref[...] += lax.dot_general(a_ref[...], v_ref[...], TN_DIMS, preferred_element_type=F32)


def peer_experts(xn, x, e2r, cnt, e1n, u, v, layer, *, tm, te):
    n, d = x.shape
    tok = pl.BlockSpec((tm, d), lambda i, c: (i, 0))
    groups = tm // V7X_LANES
    by_i2 = pl.BlockSpec((groups, PEER_HEADS, PEER_N_KEYS, V7X_LANES), lambda i, c: (i, 0, 0, 0))
    by_i1 = pl.BlockSpec((PEER_HEADS, te // PEER_N_KEYS, tm), lambda i, c: (0, c, i))
    wts = pl.BlockSpec((None, te, d), lambda i, c: (layer, c, 0))
    return pl.pallas_call(
        _peer_expert_kernel,
        grid=(n // tm, u.shape[1] // te),
        in_specs=[tok, tok, by_i2, by_i1, by_i1, wts, wts],
        out_specs=tok,
        out_shape=jax.ShapeDtypeStruct((n, d), F32),
        scratch_shapes=[pltpu.VMEM((te, tm), BF16)],
        compiler_params=_params("arbitrary", "arbitrary"),
        name="peer_experts",
    )(xn, x, e2r, cnt, e1n, u, v)


def peer_layer(x, g, w_q, keys, u, v, layer, *, tm, te):
    xn, e2r, cnt, e1n = peer_query(x, g, w_q, keys, layer, tm=tm)
    return peer_experts(xn, x, e2r, cnt, e1n, u, v, layer, tm=tm, te=te)


def _residue_order(tm, dil):
    return np.arange(tm).reshape(tm // dil, dil).T.reshape(tm)


def _qkv_kernel(x_ref, g_ref, w_ref, qg_ref, kg_ref, cos_ref, sin_ref, o_ref, kv_ref, xs_ref, xp_ref, kvf_ref,
                *, dil, tiles, tail_tile):
    tm, d = x_ref.shape
    hd = w_ref.shape[1] // 3
    per_res = tm // dil
    if dil > 1:
        for c in range(d // V7X_LANES):
            xs_ref[c] = x_ref[:, c * V7X_LANES:(c + 1) * V7X_LANES]
        for res in range(dil):
            for c in range(d // V7X_LANES):
                xp_ref[res * per_res:(res + 1) * per_res, c * V7X_LANES:(c + 1) * V7X_LANES] = (
                    xs_ref[c, pl.ds(res, per_res, stride=dil), :])
        x = xp_ref[...]
    else:
        x = x_ref[...]
    xn = _rms_norm(x, g_ref[...]).astype(BF16)
    cos, sin = cos_ref[...], sin_ref[...]

    def store(col0, val):
        for res in range(dil):
            piece = val[res * per_res:(res + 1) * per_res, :]
            o_ref[res, :, col0:col0 + val.shape[1]] = piece.astype(BF16)
            if col0 >= hd:
                kvf_ref[res, :, col0 - hd:col0 - hd + val.shape[1]] = piece

    for kind, gain_ref in enumerate((qg_ref, kg_ref, None)):
        y = jnp.dot(xn, w_ref[:, kind * hd:(kind + 1) * hd], preferred_element_type=F32)
        if gain_ref is None:
            store(kind * hd, y)
            continue
        gain = gain_ref[...]
        for h in range(hd // HEAD_DIM):
            yn = _rms_norm(y[:, h * HEAD_DIM:(h + 1) * HEAD_DIM], gain)
            store(kind * hd + h * HEAD_DIM, yn * cos + pltpu.roll(yn, HEAD_DIM // 2, axis=1) * sin)

    @pl.when(pl.program_id(0) % tiles >= tail_tile)
    def _():
        rows = kv_ref.shape[0]
        if dil == 1:
            kv_ref[...] = kvf_ref[0, tm - rows:, :]
            return
        slabs = xs_ref.shape[0]
        for half in range(2 * hd // (slabs * V7X_LANES)):
            col0 = half * slabs * V7X_LANES
            for res in range(dil):
                for c in range(slabs):
                    xs_ref[c, pl.ds(res, per_res, stride=dil), :] = (
                        kvf_ref[res, :, col0 + c * V7X_LANES:col0 + (c + 1) * V7X_LANES])
            for c in range(slabs):
                kv_ref[:, col0 + c * V7X_LANES:col0 + (c + 1) * V7X_LANES] = xs_ref[c]


def qkv_project(x, g, w_qkv, q_gain, k_gain, pos, group, *, dil, batch, tm, keep):
    n, d = x.shape
    seq = n // batch
    hd3 = w_qkv.shape[1] // N_GROUPS
    tiles = seq // tm
    per_res = tm // dil
    assert per_res % (2 * V7X_SUBLANES) == 0
    kv_rows = min(keep, tm)
    assert keep % kv_rows == 0 and (dil == 1 or kv_rows == tm)
    tail_tile = tiles - keep // kv_rows
    out_specs = [pl.BlockSpec((None, dil, per_res, hd3), lambda i: (i // tiles, 0, i % tiles, 0)),
                 pl.BlockSpec((None, kv_rows, 2 * hd3 // 3),
                              lambda i: (i // tiles, jnp.maximum(i % tiles - tail_tile, 0), 0))]
    out_shape = [jax.ShapeDtypeStruct((batch, dil, seq // dil, hd3), BF16),
                 jax.ShapeDtypeStruct((batch, keep, 2 * hd3 // 3), F32)]
    order = (np.arange(tiles)[:, None] * tm + _residue_order(tm, dil)[None, :]).reshape(seq)
    cos, sin = _rope_tables(jnp.asarray(np.asarray(pos)[order]))
    gain = pl.BlockSpec((None, 1, HEAD_DIM), lambda i: (group, 0, 0))
    rope = pl.BlockSpec((tm, HEAD_DIM), lambda i: (i % tiles, 0))
    return pl.pallas_call(
        functools.partial(_qkv_kernel, dil=dil, tiles=tiles, tail_tile=tail_tile),
        grid=(n // tm,),
        in_specs=[
            pl.BlockSpec((tm, d), lambda i: (i, 0)),
            pl.BlockSpec((1, d), lambda i: (0, 0)),
            pl.BlockSpec((d, hd3), lambda i: (0, group), pipeline_mode=pl.Buffered(1)),
            gain, gain, rope, rope,
        ],
        out_specs=out_specs,
        out_shape=out_shape,
        scratch_shapes=[pltpu.VMEM((d // V7X_LANES, tm, V7X_LANES), F32), pltpu.VMEM((tm, d), F32),
                        pltpu.VMEM((dil, per_res, 2 * hd3 // 3), F32)],
        compiler_params=_params("arbitrary"),
        name=f"qkv_project_g{group}_d{dil}",
    )(x, g, w_qkv, q_gain.reshape(N_GROUPS, 1, HEAD_DIM), k_gain.reshape(N_GROUPS, 1, HEAD_DIM), cos, sin)


def _spread_heads(cols_list, width):
    rows = cols_list[0].shape[0]
    lane = lax.broadcasted_iota(jnp.int32, (rows, width), 1)
    out = jnp.zeros((rows, width), F32)
    for h, col in enumerate(cols_list):
        out = jnp.where(lane == h, col, out)
    return out


def _attn_prompt_kernel(q_ref, kc_ref, kp_ref, vc_ref, vp_ref, o_ref, lse_ref, *, window_blocks):
    i = pl.program_id(2)
    qb = kp_ref.shape[0]
    blocks = q_ref.shape[0] // qb
    qi = lax.broadcasted_iota(jnp.int32, (qb, 2 * qb), 0)
    kj = lax.broadcasted_iota(jnp.int32, (qb, 2 * qb), 1)
    band = (kj <= qi + qb) & (kj >= qi + qb - window_blocks)
    for j in range(blocks):
        rows = slice(j * qb, (j + 1) * qb)
        live = band & ((i > 0) | (kj >= qb)) if j == 0 else band
        lses = []
        for h in range(q_ref.shape[1] // HEAD_DIM):
            cols = slice(h * HEAD_DIM, (h + 1) * HEAD_DIM)
            k_prev = kp_ref[:, cols] if j == 0 else kc_ref[(j - 1) * qb:j * qb, cols]
            v_prev = vp_ref[:, cols] if j == 0 else vc_ref[(j - 1) * qb:j * qb, cols]
            q = q_ref[rows, cols].astype(BF16)
            k = jnp.concatenate([k_prev, kc_ref[rows, cols]], axis=0).astype(BF16)
            v = jnp.concatenate([v_prev, vc_ref[rows, cols]], axis=0).astype(BF16)
            s = lax.dot_general(q, k, NT_DIMS, preferred_element_type=F32) * ATTN_SCALE
            s = jnp.where(live, s, NEG_INF)
            m = jnp.max(s, axis=-1, keepdims=True)
            p = jnp.exp(s - m)
            l = jnp.sum(p, axis=-1, keepdims=True)
            o_ref[rows, cols] = jnp.dot(p.astype(BF16), v, preferred_element_type=F32) / l
            lses.append(m + jnp.log(l))
        lse_ref[rows, :] = _spread_heads(lses, lse_ref.shape[1])


def attn_prompt_group(qkv, group):
    batch, dil, n_strided, hd3 = qkv.shape
    hd = hd3 // 3
    win = WINDOWS[group]
    step_rows = ATTN_STEP_BLOCKS * ATTN_BLOCK
    assert dil == DILATIONS[group] and n_strided % step_rows == 0 and win // dil <= ATTN_BLOCK
    blk = (None, None, step_rows, hd)

    def cur(kind):
        return pl.BlockSpec(blk, lambda b, r, i: (b, r, i, kind))

    def prev(kind):
        return pl.BlockSpec((None, None, ATTN_BLOCK, hd),
                            lambda b, r, i: (b, r, jnp.maximum(i * ATTN_STEP_BLOCKS - 1, 0), kind))

    return pl.pallas_call(
        functools.partial(_attn_prompt_kernel, window_blocks=win // dil),
        grid=(batch, dil, n_strided // step_rows),
        in_specs=[cur(0), cur(1), prev(1), cur(2), prev(2)],
        out_specs=[pl.BlockSpec(blk, lambda b, r, i: (b, r, i, 0)),
                   pl.BlockSpec((None, None, step_rows, V7X_LANES), lambda b, r, i: (b, r, i, 0))],
        out_shape=[jax.ShapeDtypeStruct((batch, dil, n_strided, hd), F32),
                   jax.ShapeDtypeStruct((batch, dil, n_strided, V7X_LANES), F32)],
        compiler_params=_params("arbitrary", "arbitrary", "arbitrary"),
        name=f"attn_prompt_g{group}",
    )(qkv, qkv, qkv, qkv, qkv)


def _head_segments(n_heads, transpose):
    shape = (V7X_LANES, n_heads * HEAD_DIM) if transpose else (n_heads * HEAD_DIM, V7X_LANES)
    c = lax.broadcasted_iota(jnp.int32, shape, 1 if transpose else 0)
    h = lax.broadcasted_iota(jnp.int32, shape, 0 if transpose else 1)
    return jnp.where((c >= h * HEAD_DIM) & (c < (h + 1) * HEAD_DIM), 1.0, 0.0).astype(BF16)


def _attn_sample_kernel(n0_ref, n1_ref, n2_ref, c0_ref, c1_ref, c2_ref, o_ref, lse_ref):
    t = pl.program_id(1)
    for g, (new_ref, cache_ref) in enumerate(zip((n0_ref, n1_ref, n2_ref), (c0_ref, c1_ref, c2_ref))):
        q = new_ref[pl.ds(t, 1), 0]
        k_new, v_new = new_ref[:, 1], new_ref[:, 2]
        k_old, v_old = cache_ref[:, 0], cache_ref[:, 1]
        s_old = jnp.sum(k_old * q, axis=-1, keepdims=True) * ATTN_SCALE
        s_new = jnp.sum(k_new * q, axis=-1, keepdims=True) * ATTN_SCALE
        row_n = lax.broadcasted_iota(jnp.int32, s_new.shape, 0)
        if DILATIONS[g] == 1:
            row_c = lax.broadcasted_iota(jnp.int32, s_old.shape, 0)
            s_old = jnp.where(row_c >= t, s_old, NEG_INF)
            s_new = jnp.where(row_n <= t, s_new, NEG_INF)
        else:
            s_new = jnp.where(row_n == t, s_new, NEG_INF)
        m = jnp.maximum(jnp.max(s_old, axis=0, keepdims=True), jnp.max(s_new, axis=0, keepdims=True))
        p_old, p_new = jnp.exp(s_old - m), jnp.exp(s_new - m)
        l = jnp.sum(p_old, axis=0, keepdims=True) + jnp.sum(p_new, axis=0, keepdims=True)
        o = jnp.sum(p_old * v_old, axis=0, keepdims=True) + jnp.sum(p_new * v_new, axis=0, keepdims=True)
        o_ref[g, pl.ds(t, 1)] = o / l
        lse_ref[g, pl.ds(t, 1)] = m + jnp.log(l)


def attn_sample(qkv, caches, *, batch, t_new):
    n_heads = caches[0].shape[3]
    views, specs = [], []
    for g, cache in enumerate(caches):
        dil, keep = DILATIONS[g], cache.shape[1]
        assert keep == WINDOWS[g] and keep // dil == ATTN_BLOCK and (dil == 1 or t_new <= dil)
        views.append(cache.reshape(batch, keep // dil, dil, 2, n_heads, HEAD_DIM))
        specs.append(pl.BlockSpec((None, ATTN_BLOCK, None, 2, n_heads, HEAD_DIM),
                                  (lambda b, t: (b, 0, 0, 0, 0, 0)) if dil == 1 else
                                  (lambda b, t: (b, 0, t, 0, 0, 0))))
    rows_new = -(-t_new // V7X_SUBLANES) * V7X_SUBLANES
    new_rows = [jnp.pad(a.reshape(batch, t_new, 3, n_heads, HEAD_DIM),
                        ((0, 0), (0, rows_new - t_new), (0, 0), (0, 0), (0, 0))) for a in qkv]
    o, lse = pl.pallas_call(
        _attn_sample_kernel,
        grid=(batch, t_new),
        in_specs=[pl.BlockSpec((None, rows_new, 3, n_heads, HEAD_DIM), lambda b, t: (b, 0, 0, 0, 0))] * N_GROUPS
        + specs,
        out_specs=[pl.BlockSpec((N_GROUPS, None, t_new, n_heads, HEAD_DIM), lambda b, t: (0, b, 0, 0, 0)),
                   pl.BlockSpec((N_GROUPS, None, t_new, n_heads, 1), lambda b, t: (0, b, 0, 0, 0))],
        out_shape=[jax.ShapeDtypeStruct((N_GROUPS, batch, t_new, n_heads, HEAD_DIM), F32),
                   jax.ShapeDtypeStruct((N_GROUPS, batch, t_new, n_heads, 1), F32)],
        compiler_params=_params("arbitrary", "arbitrary"),
        name="attn_sample",
    )(*new_rows, *views)
    lse = jnp.pad(lse.reshape(N_GROUPS, batch * t_new, n_heads), ((0, 0), (0, 0), (0, V7X_LANES - n_heads)))
    return o.reshape(N_GROUPS, batch * t_new, n_heads * HEAD_DIM), lse


def _attn_out_kernel(x_ref, o0_ref, o1_ref, o2_ref, l0_ref, l1_ref, l2_ref, wo_ref, y_ref,
                     og_ref, lg_ref, mix_ref):
    tm = x_ref.shape[0]
    hd = wo_ref.shape[0]
    heads = hd // HEAD_DIM
    for g, (o_ref, l_ref) in enumerate(zip((o0_ref, o1_ref, o2_ref), (l0_ref, l1_ref, l2_ref))):
        dil = o_ref.shape[0]
        for res in range(dil):
            rows = pl.ds(res, tm // dil, stride=dil) if dil > 1 else slice(0, tm)
            lg_ref[g, rows, :] = l_ref[res]
            for h in range(heads):
                og_ref[g, h, rows, :] = o_ref[res, :, h * HEAD_DIM:(h + 1) * HEAD_DIM]
    lse = [lg_ref[g] for g in range(N_GROUPS)]
    m = functools.reduce(jnp.maximum, lse)
    e = [jnp.exp(l - m) for l in lse]
    tot = functools.reduce(jnp.add, e)
    seg_t = _head_segments(heads, transpose=True)
    w = [_split_dot(e[g] / tot, seg_t) for g in range(N_GROUPS)]
    for h in range(heads):
        cols = slice(h * HEAD_DIM, (h + 1) * HEAD_DIM)
        mix = functools.reduce(jnp.add, [w[g][:, cols] * og_ref[g, h] for g in range(N_GROUPS)])
        mix_ref[:, cols] = mix.astype(BF16)
    y_ref[...] = x_ref[...] + jnp.dot(mix_ref[...], wo_ref[...], preferred_element_type=F32)


def attn_out(x, outs, lses, w_o, *, tm):
    n, d = x.shape
    batch = outs[0].shape[0]
    hd = outs[0].shape[3]
    tiles = n // batch // tm

    def by_residue(a):
        dil = a.shape[1]
        return pl.BlockSpec((None, dil, tm // dil, a.shape[3]), lambda i: (i // tiles, 0, i % tiles, 0))

    return pl.pallas_call(
        _attn_out_kernel,
        grid=(n // tm,),
        in_specs=[pl.BlockSpec((tm, d), lambda i: (i, 0))] + [by_residue(a) for a in (*outs, *lses)]
        + [pl.BlockSpec((hd, d), lambda i: (0, 0))],
        out_specs=pl.BlockSpec((tm, d), lambda i: (i, 0)),
        out_shape=jax.ShapeDtypeStruct((n, d), F32),
        scratch_shapes=[pltpu.VMEM((N_GROUPS, hd // HEAD_DIM, tm, HEAD_DIM), F32),
                        pltpu.VMEM((N_GROUPS, tm, V7X_LANES), F32), pltpu.VMEM((tm, hd), BF16)],
        compiler_params=_params("arbitrary"),
        name="attn_out",
    )(x, *outs, *lses, w_o)


def _kv_state_kernel(c_ref, n_ref, o_ref):
    i = pl.program_id(1)
    last = pl.num_programs(1) - 1
    rows, t_new = o_ref.shape[0], n_ref.shape[0]

    @pl.when(i < last)
    def _():
        o_ref[...] = c_ref[0]

    @pl.when(i == last)
    def _():
        o_ref[0:rows - t_new] = c_ref[0, t_new:rows]
        o_ref[rows - t_new:rows] = n_ref[...]


def kv_state_update(cache, new, *, rows):
    batch, keep = cache.shape[:2]
    t_new = new.shape[1]
    tail = cache.shape[2:]
    blocks = keep // rows
    assert keep % rows == 0 and rows > t_new and new.shape[2:] == tail
    zeros = (0,) * len(tail)

    def shifted(b, i):
        return (b, jnp.where(i < blocks - 1, i * rows + t_new, keep - rows)) + zeros

    return pl.pallas_call(
        _kv_state_kernel,
        grid=(batch, blocks),
        in_specs=[pl.BlockSpec(tuple(pl.Element(size) for size in (1, rows) + tail), shifted),
                  pl.BlockSpec((None, t_new) + tail, lambda b, i: (b, 0) + zeros)],
        out_specs=pl.BlockSpec((None, rows) + tail, lambda b, i: (b, i) + zeros),
        out_shape=jax.ShapeDtypeStruct(cache.shape, cache.dtype),
        compiler_params=_params("arbitrary", "arbitrary"),
        name="kv_state_update",
    )(cache, new)


def _rope_tables(pos):
    half = HEAD_DIM // 2
    inv_freq = ROPE_THETA ** (-jnp.arange(half, dtype=F32) / half)
    ang = pos.astype(F32)[:, None] * inv_freq
    cos, sin = jnp.cos(ang), jnp.sin(ang)
    return jnp.concatenate([cos, cos], axis=-1), jnp.concatenate([-sin, sin], axis=-1)


def _pad_rows(a, rows):
    return jnp.pad(a, ((0, rows - a.shape[0]),) + ((0, 0),) * (a.ndim - 1))


PROMPT_TM = 512
ATTN_STEP_BLOCKS = 2
ATTN_OUT_TM = 256
QKV_TM = 256
PEER_TE = 1024
CONV_TN = 1024
SAMPLE_PEER_ROWS = V7X_LANES


def kernel(x_prompt, x_sample, state_conv, cache_kv_w128, cache_kv_w512, cache_kv_w2048,
           norm_mix, norm_ffn, conv_w_in, conv_w, conv_w_out,
           attn_w_qkv, attn_q_norm, attn_k_norm, attn_w_o,
           peer_w_q, peer_sub_keys, peer_u, peer_v):
    batch, seq, d = x_prompt.shape
    dec_batch, dec_seq, _ = x_sample.shape
    n_p, n_s = batch * seq, dec_batch * dec_seq
    caches = (cache_kv_w128, cache_kv_w512, cache_kv_w2048)
    n_heads = caches[0].shape[3]
    hd = n_heads * HEAD_DIM

    w_in, w_out = conv_w_in.astype(BF16), conv_w_out.astype(BF16)
    w_qkv, w_o = attn_w_qkv.astype(BF16), attn_w_o.astype(BF16)
    w_q = peer_w_q.astype(BF16)
    keys = peer_sub_keys.astype(BF16).reshape(peer_sub_keys.shape[0], 2 * PEER_HEADS, PEER_N_KEYS, PEER_D_HALF)
    pu, pv = peer_u.astype(BF16), peer_v.astype(BF16)
    g_mix = norm_mix.reshape(-1, 1, d)
    g_ffn = norm_ffn.reshape(-1, 1, d)

    xp = x_prompt.reshape(n_p, d)
    xs = x_sample.reshape(n_s, d)

    def peer(x, layer, tm):
        return peer_layer(x, g_ffn[layer], w_q, keys, pu, pv, layer, tm=tm, te=PEER_TE)

    def peer_sample(x, layer):
        return peer(_pad_rows(x, SAMPLE_PEER_ROWS), layer, SAMPLE_PEER_ROWS)[:n_s]

    bp, up = conv_in(xp, g_mix[0], w_in, tm=PROMPT_TM, tn=CONV_TN)
    xp = conv_out_prompt(xp, bp, up, jnp.zeros((batch, CONV_WIDTH - 1, d), F32), conv_w, w_out,
                         seq_len=seq, tm=PROMPT_TM)
    conv_state_prompt = up.reshape(batch, seq, d)[:, seq - (CONV_WIDTH - 1):]

    bs, us = conv_in(xs, g_mix[0], w_in, tm=n_s, tn=CONV_TN)
    u_ext = jnp.concatenate([state_conv, us.reshape(dec_batch, dec_seq, d)], axis=1)
    shifted = [u_ext[:, k:k + dec_seq].reshape(n_s, d) for k in range(CONV_WIDTH)]
    xs = conv_out_sample(xs, bs, shifted[0], shifted[1], shifted[2], conv_w, w_out)
    conv_state_sample = u_ext[:, dec_seq:]

    xp = peer(xp, 0, PROMPT_TM)
    xs = peer_sample(xs, 0)

    qkv_p, kv_prompt = zip(*[
        qkv_project(xp, g_mix[1], w_qkv, attn_q_norm, attn_k_norm, np.arange(seq), g,
                    dil=DILATIONS[g], batch=batch, tm=QKV_TM, keep=min(WINDOWS[g], seq)) for g in range(N_GROUPS)])
    outs, lses = zip(*[attn_prompt_group(qkv_p[g], g) for g in range(N_GROUPS)])
    xp = attn_out(xp, outs, lses, w_o, tm=ATTN_OUT_TM)

    pos_s = np.tile(PAST_LEN + np.arange(dec_seq), dec_batch)
    q_s, kv_s = zip(*[qkv_project(xs, g_mix[1], w_qkv, attn_q_norm, attn_k_norm, pos_s, g,
                                  dil=1, batch=1, tm=n_s, keep=n_s) for g in range(N_GROUPS)])
    kv_s = [kv.reshape(n_s, 2 * hd) for kv in kv_s]
    qkv_s = [jnp.concatenate([q.reshape(n_s, 3 * hd)[:, :hd].astype(F32), kv], axis=1) for q, kv in zip(q_s, kv_s)]
    o_s, lse_s = attn_sample(qkv_s, caches, batch=dec_batch, t_new=dec_seq)
    xs = attn_out(xs, [o_s[g].reshape(1, 1, n_s, hd) for g in range(N_GROUPS)],
                  [lse_s[g].reshape(1, 1, n_s, V7X_LANES) for g in range(N_GROUPS)], w_o, tm=n_s)

    xp = peer(xp, 1, PROMPT_TM)
    xs = peer_sample(xs, 1)

    kv_prompt = [kv.reshape(batch, kv.shape[1], 2, n_heads, HEAD_DIM) for kv in kv_prompt]
    kv_sample = [kv_state_update(cache, kv.reshape(dec_batch, dec_seq, 2, n_heads, HEAD_DIM), rows=ATTN_BLOCK)
                 for cache, kv in zip(caches, kv_s)]

    return (xp.reshape(batch, seq, d), xs.reshape(dec_batch, dec_seq, d),
            conv_state_prompt, conv_state_sample, *kv_prompt, *kv_sample)
```

```python
import functools
import math

import jax
import jax.numpy as jnp
import numpy as np
from jax import lax
from jax.experimental import pallas as pl
from jax.experimental.pallas import tpu as pltpu

F32 = jnp.float32
BF16 = jnp.bfloat16

V7X_LANES = 128
V7X_SUBLANES = 8
V7X_VMEM_BYTES = 64 * 1024 * 1024
VMEM_LIMIT_BYTES = V7X_VMEM_BYTES - 8 * 1024 * 1024

NORM_EPS = 1e-6
CONV_WIDTH = 3
WINDOWS = (128, 512, 2048)
DILATIONS = (1, 4, 16)
N_GROUPS = 3
HEAD_DIM = 128
ATTN_BLOCK = 128
ATTN_SCALE = HEAD_DIM ** -0.5
ROPE_THETA = 10000.0
PAST_LEN = 16384
assert PAST_LEN >= max(WINDOWS)
PEER_HEADS = 8
PEER_N_KEYS = 128
PEER_TOPK = 16
PEER_D_HALF = 128
SQRT_HALF = math.sqrt(0.5)
NOT_SELECTED = 31.0
NEG_INF = float("-inf")

NT_DIMS = (((1,), (1,)), ((), ()))
TN_DIMS = (((0,), (0,)), ((), ()))


def _params(*semantics):
    return pltpu.CompilerParams(dimension_semantics=semantics, vmem_limit_bytes=VMEM_LIMIT_BYTES)


def _rms_norm(x, g):
    return x * lax.rsqrt(jnp.mean(x * x, axis=-1, keepdims=True) + NORM_EPS) * g


def _split_dot(a, b01):
    hi = a.astype(BF16)
    lo = (a - hi.astype(F32)).astype(BF16)
    return (jnp.dot(hi, b01, preferred_element_type=F32)
            + jnp.dot(lo, b01, preferred_element_type=F32))


def _conv_in_kernel(x_ref, g_ref, wb_ref, wc_ref, wh_ref, b_ref, u_ref, xn_ref):
    @pl.when(pl.program_id(1) == 0)
    def _():
        xn_ref[...] = _rms_norm(x_ref[...], g_ref[...]).astype(BF16)

    xn = xn_ref[...]
    b_ref[...] = jnp.dot(xn, wb_ref[...], preferred_element_type=F32)
    c = jnp.dot(xn, wc_ref[...], preferred_element_type=F32)
    h = jnp.dot(xn, wh_ref[...], preferred_element_type=F32)
    u_ref[...] = c * h


def conv_in(x, g, w_in, *, tm, tn):
    n, d = x.shape
    nj = d // tn
    return pl.pallas_call(
        _conv_in_kernel,
        grid=(n // tm, nj),
        in_specs=[
            pl.BlockSpec((tm, d), lambda i, j: (i, 0)),
            pl.BlockSpec((1, d), lambda i, j: (0, 0)),
            pl.BlockSpec((d, tn), lambda i, j: (0, j)),
            pl.BlockSpec((d, tn), lambda i, j: (0, j + nj)),
            pl.BlockSpec((d, tn), lambda i, j: (0, j + 2 * nj)),
        ],
        out_specs=[pl.BlockSpec((tm, tn), lambda i, j: (i, j)),
                   pl.BlockSpec((tm, tn), lambda i, j: (i, j))],
        out_shape=[jax.ShapeDtypeStruct((n, d), F32), jax.ShapeDtypeStruct((n, d), F32)],
        scratch_shapes=[pltpu.VMEM((tm, d), BF16)],
        compiler_params=_params("arbitrary", "arbitrary"),
        name="conv_in",
    )(x, g, w_in, w_in, w_in)


def _conv_mix(cw_ref, um2, um1, u0):
    return cw_ref[0:1, :] * um2 + cw_ref[1:2, :] * um1 + cw_ref[2:3, :] * u0


def _conv_out_prompt_kernel(x_ref, b_ref, u_ref, up_ref, st_ref, cw_ref, wo_ref, o_ref, *, tiles_per_seq):
    i = pl.program_id(0)
    u = u_ref[...]
    first = (i % tiles_per_seq) == 0
    prev = jnp.where(first, st_ref[...], up_ref[V7X_SUBLANES - 2:V7X_SUBLANES, :])
    row = lax.broadcasted_iota(jnp.int32, (u.shape[0], 1), 0)
    um1 = jnp.where(row == 0, prev[1:2, :], pltpu.roll(u, 1, axis=0))
    um2 = jnp.where(row == 0, prev[0:1, :], jnp.where(row == 1, prev[1:2, :], pltpu.roll(u, 2, axis=0)))
    a = (b_ref[...] * _conv_mix(cw_ref, um2, um1, u)).astype(BF16)
    o_ref[...] = x_ref[...] + jnp.dot(a, wo_ref[...], preferred_element_type=F32)


def conv_out_prompt(x, bgate, u, state, conv_w, w_out, *, seq_len, tm):
    n, d = x.shape
    tps = seq_len // tm
    sub_per_tile = tm // V7X_SUBLANES
    tile = pl.BlockSpec((tm, d), lambda i: (i, 0))
    return pl.pallas_call(
        functools.partial(_conv_out_prompt_kernel, tiles_per_seq=tps),
        grid=(n // tm,),
        in_specs=[
            tile, tile, tile,
            pl.BlockSpec((V7X_SUBLANES, d), lambda i: (jnp.maximum(i * sub_per_tile - 1, 0), 0)),
            pl.BlockSpec((None, CONV_WIDTH - 1, d), lambda i: (i // tps, 0, 0)),
            pl.BlockSpec((CONV_WIDTH, d), lambda i: (0, 0)),
            pl.BlockSpec((d, d), lambda i: (0, 0)),
        ],
        out_specs=tile,
        out_shape=jax.ShapeDtypeStruct((n, d), F32),
        compiler_params=_params("arbitrary"),
        name="conv_out_prompt",
    )(x, bgate, u, u, state, conv_w, w_out)


def _conv_out_sample_kernel(x_ref, b_ref, um2_ref, um1_ref, u_ref, cw_ref, wo_ref, o_ref):
    a = (b_ref[...] * _conv_mix(cw_ref, um2_ref[...], um1_ref[...], u_ref[...])).astype(BF16)
    o_ref[...] = x_ref[...] + jnp.dot(a, wo_ref[...], preferred_element_type=F32)


def conv_out_sample(x, bgate, um2, um1, u0, conv_w, w_out):
    n, d = x.shape
    full = pl.BlockSpec((n, d), lambda i: (0, 0))
    return pl.pallas_call(
        _conv_out_sample_kernel,
        grid=(1,),
        in_specs=[full, full, full, full, full,
                  pl.BlockSpec((CONV_WIDTH, d), lambda i: (0, 0)),
                  pl.BlockSpec((d, d), lambda i: (0, 0))],
        out_specs=full,
        out_shape=jax.ShapeDtypeStruct((n, d), F32),
        compiler_params=_params("arbitrary"),
        name="conv_out_sample",
    )(x, bgate, um2, um1, u0, conv_w, w_out)


CAND_ROWS = 72


def _cand_flat_index():
    pairs = []
    for b in range(4):
        pairs += [(a, b) for a in range(16 if b == 0 else 8)]
    pairs += [(0, b) if b >= 4 else None for b in range(16)]
    pairs += [(1, b) if b >= 4 else None for b in range(8)]
    pairs += [(2, b) if b == 4 else None for b in range(8)]
    rows = [a * PEER_TOPK + b if pair is not None and (a + 1) * (b + 1) <= PEER_TOPK else 256 + i
            for i, pair in enumerate(pairs) for a, b in [pair or (99, 99)]]
    assert len(rows) == CAND_ROWS and len({r for r in rows if r < 256}) == 50
    return jnp.broadcast_to(jnp.asarray(rows, F32).reshape(CAND_ROWS, 1), (CAND_ROWS, V7X_LANES))


def _top16(s, key_iota, exact_ties):
    work = s
    rank = jnp.full(s.shape, NOT_SELECTED, F32)
    val_row = lax.broadcasted_iota(jnp.int32, (PEER_TOPK, s.shape[1]), 0)
    vals = jnp.zeros((PEER_TOPK, s.shape[1]), F32)
    for a in range(PEER_TOPK):
        m = jnp.max(work, axis=0, keepdims=True)
        sel = work == m
        if exact_ties:
            first = jnp.min(jnp.where(sel, key_iota, float(PEER_N_KEYS)), axis=0, keepdims=True)
            sel = key_iota == first
        rank = jnp.where(sel, float(a), rank)
        work = jnp.where(sel, NEG_INF, work)
        vals = jnp.where(val_row == a, m, vals)
    return vals, rank


def _ranked_rows(rank):
    return jnp.sum(jnp.where(rank < NOT_SELECTED, 1.0, 0.0), axis=0, keepdims=True)


def _peer_select(s1, s2, fidx, exact_ties):
    lanes = s1.shape[1]
    key_iota = lax.broadcasted_iota(jnp.int32, (PEER_N_KEYS, lanes), 0).astype(F32)
    v1, rank1 = _top16(s1, key_iota, exact_ties)
    v2, rank2 = _top16(s2, key_iota, exact_ties)

    r16 = lax.broadcasted_iota(jnp.int32, (16, lanes), 0)
    r8 = lax.broadcasted_iota(jnp.int32, (8, lanes), 0)
    v1_lo, v2_lo = v1[0:8, :], v2[0:8, :]
    cand = jnp.concatenate([
        v1 + v2[0:1, :],
        v1_lo + v2[1:2, :],
        jnp.where(r8 < 5, v1_lo + v2[2:3, :], NEG_INF),
        jnp.where(r8 < 4, v1_lo + v2[3:4, :], NEG_INF),
        jnp.where(r16 >= 4, v1[0:1, :] + v2, NEG_INF),
        jnp.where(r8 >= 4, v1[1:2, :] + v2_lo, NEG_INF),
        jnp.where(r8 == 4, v1[2:3, :] + v2_lo, NEG_INF),
    ], axis=0)
    c_top = v1[0:1, :] + v2[0:1, :]
    taken = jnp.zeros(cand.shape, F32)
    z = jnp.zeros((1, lanes), F32)
    for _ in range(PEER_TOPK):
        m = jnp.max(cand, axis=0, keepdims=True)
        sel = cand == m
        if exact_ties:
            first = jnp.min(jnp.where(sel, fidx, 1024.0), axis=0, keepdims=True)
            sel = fidx == first
        taken = jnp.where(sel, 1.0, taken)
        cand = jnp.where(sel, NEG_INF, cand)
        z = z + jnp.exp(m - c_top)

    extra = jnp.where(r8 == 0, jnp.sum(taken[40:56, :], axis=0, keepdims=True),
                      jnp.where(r8 == 1, jnp.sum(taken[56:64, :], axis=0, keepdims=True),
                                jnp.where(r8 == 2, jnp.sum(taken[64:72, :], axis=0, keepdims=True), 0.0)))
    cnt_lo = taken[0:8, :] + taken[16:24, :] + taken[24:32, :] + taken[32:40, :] + extra
    cnt_vec = jnp.concatenate([cnt_lo, taken[8:16, :]], axis=0)
    cnt = jnp.zeros(s1.shape, F32)
    for a in range(PEER_TOPK):
        cnt = jnp.where(rank1 == float(a), cnt_vec[a:a + 1, :], cnt)

    e1n = jnp.exp(s1 - v1[0:1, :]) / z
    e2 = jnp.exp(s2 - v2[0:1, :])
    ranked = jnp.maximum(jnp.maximum(_ranked_rows(rank1), _ranked_rows(rank2)),
                         jnp.sum(taken, axis=0, keepdims=True))
    return rank2, e2, cnt, e1n, ranked


def _peer_query_kernel(x_ref, g_ref, wq_ref, keys_ref, fidx_ref,
                       xn_ref, rank2_ref, e2_ref, cnt_ref, e1n_ref, sc_ref):
    tm = x_ref.shape[0]
    xn = _rms_norm(x_ref[...], g_ref[...]).astype(BF16)
    xn_ref[...] = xn
    q = jnp.dot(xn, wq_ref[...], preferred_element_type=F32).astype(BF16)
    for rp in range(2 * PEER_HEADS):
        q_rp = q[:, rp * PEER_D_HALF:(rp + 1) * PEER_D_HALF]
        sc_ref[rp] = lax.dot_general(keys_ref[rp], q_rp, NT_DIMS, preferred_element_type=F32)

    fidx = fidx_ref[...]

    def select_head(r, exact_ties):
        ranked = jnp.zeros((1, V7X_LANES), F32)
        for s in range(tm // V7X_LANES):
            cols = slice(s * V7X_LANES, (s + 1) * V7X_LANES)
            rank2, e2, cnt, e1n, n = _peer_select(sc_ref[2 * r, :, cols], sc_ref[2 * r + 1, :, cols], fidx,
                                                  exact_ties)
            rank2_ref[s, r] = rank2
            e2_ref[s, r] = e2
            cnt_ref[r, :, cols] = cnt
            e1n_ref[r, :, cols] = e1n
            ranked = jnp.maximum(ranked, n)
        return ranked

    def head_body(r, carry):
        ranked = select_head(r, exact_ties=False)

        @pl.when(jnp.max(ranked) > float(PEER_TOPK))
        def _():
            select_head(r, exact_ties=True)

        return carry

    lax.fori_loop(0, PEER_HEADS, head_body, 0)


def peer_query(x, g, w_q, keys, layer, *, tm):
    n, d = x.shape
    dq = w_q.shape[2]
    groups = tm // V7X_LANES
    by_i2 = jax.ShapeDtypeStruct((n // V7X_LANES, PEER_HEADS, PEER_N_KEYS, V7X_LANES), F32)
    by_i2_spec = pl.BlockSpec((groups, PEER_HEADS, PEER_N_KEYS, V7X_LANES), lambda i: (i, 0, 0, 0))
    sel = jax.ShapeDtypeStruct((PEER_HEADS, PEER_N_KEYS, n), F32)
    sel_spec = pl.BlockSpec((PEER_HEADS, PEER_N_KEYS, tm), lambda i: (0, 0, i))
    return pl.pallas_call(
        _peer_query_kernel,
        grid=(n // tm,),
        in_specs=[
            pl.BlockSpec((tm, d), lambda i: (i, 0)),
            pl.BlockSpec((1, d), lambda i: (0, 0)),
            pl.BlockSpec((None, d, dq), lambda i: (layer, 0, 0)),
            pl.BlockSpec((None, 2 * PEER_HEADS, PEER_N_KEYS, PEER_D_HALF), lambda i: (layer, 0, 0, 0)),
            pl.BlockSpec((CAND_ROWS, V7X_LANES), lambda i: (0, 0)),
        ],
        out_specs=[pl.BlockSpec((tm, d), lambda i: (i, 0)), by_i2_spec, by_i2_spec, sel_spec, sel_spec],
        out_shape=[jax.ShapeDtypeStruct((n, d), BF16), by_i2, by_i2, sel, sel],
        scratch_shapes=[pltpu.VMEM((2 * PEER_HEADS, PEER_N_KEYS, tm), F32)],
        compiler_params=_params("arbitrary"),
        name="peer_query",
    )(x, g, w_q, keys, _cand_flat_index())


GATE_ROWS = 32


def _gate_chunk(h, a_ref, rank2_ref, e2_ref, cnt_ref, e1n_ref):
    te, tm = h.shape
    for k in range(te // PEER_N_KEYS):
        for s in range(tm // V7X_LANES):
            cols = slice(s * V7X_LANES, (s + 1) * V7X_LANES)
            cnt = [cnt_ref[r, k:k + 1, cols] for r in range(PEER_HEADS)]
            e1n = [e1n_ref[r, k:k + 1, cols] for r in range(PEER_HEADS)]
            for i2 in range(0, PEER_N_KEYS, GATE_ROWS):
                gate = jnp.zeros((GATE_ROWS, V7X_LANES), F32)
                for r in range(PEER_HEADS):
                    live = rank2_ref[s, r, i2:i2 + GATE_ROWS, :] < cnt[r]
                    gate = gate + jnp.where(live, e2_ref[s, r, i2:i2 + GATE_ROWS, :], 0.0) * e1n[r]
                row0 = k * PEER_N_KEYS + i2
                hp = h[row0:row0 + GATE_ROWS, cols]
                act = 0.5 * hp * (1.0 + lax.erf(hp * SQRT_HALF))
                a_ref[row0:row0 + GATE_ROWS, cols] = (gate * act).astype(BF16)


def _peer_expert_kernel(xn_ref, x_ref, rank2_ref, e2_ref, cnt_ref, e1n_ref, u_ref, v_ref, o_ref, a_ref):
    c = pl.program_id(1)

    @pl.when(c == 0)
    def _():
        o_ref[...] = x_ref[...]

    h = lax.dot_general(u_ref[...], xn_ref[...], NT_DIMS, preferred_element_type=F32)
    _gate_chunk(h, a_ref, rank2_ref, e2_ref, cnt_ref, e1n_ref)
    o_ref[...] += lax.dot_general(a_ref[...], v_ref[...], TN_DIMS, preferred_element_type=F32)


def peer_experts(xn, x, rank2, e2, cnt, e1n, u, v, layer, *, tm, te):
    n, d = x.shape
    tok = pl.BlockSpec((tm, d), lambda i, c: (i, 0))
    groups = tm // V7X_LANES
    by_i2 = pl.BlockSpec((groups, PEER_HEADS, PEER_N_KEYS, V7X_LANES), lambda i, c: (i, 0, 0, 0))
    by_i1 = pl.BlockSpec((PEER_HEADS, te // PEER_N_KEYS, tm), lambda i, c: (0, c, i))
    wts = pl.BlockSpec((None, te, d), lambda i, c: (layer, c, 0))
    return pl.pallas_call(
        _peer_expert_kernel,
        grid=(n // tm, u.shape[1] // te),
        in_specs=[tok, tok, by_i2, by_i2, by_i1, by_i1, wts, wts],
        out_specs=tok,
        out_shape=jax.ShapeDtypeStruct((n, d), F32),
        scratch_shapes=[pltpu.VMEM((te, tm), BF16)],
        compiler_params=_params("arbitrary", "arbitrary"),
        name="peer_experts",
    )(xn, x, rank2, e2, cnt, e1n, u, v)


def peer_layer(x, g, w_q, keys, u, v, layer, *, tm, te):
    xn, rank2, e2, cnt, e1n = peer_query(x, g, w_q, keys, layer, tm=tm)
    return peer_experts(xn, x, rank2, e2, cnt, e1n, u, v, layer, tm=tm, te=te)


def _residue_order(tm, dil):
    return np.arange(tm).reshape(tm // dil, dil).T.reshape(tm)


def _qkv_kernel(x_ref, g_ref, w_ref, qg_ref, kg_ref, cos_ref, sin_ref, o_ref, kv_ref, xs_ref, xp_ref, kvf_ref,
                *, dil, tiles, tail_tile):
    tm, d = x_ref.shape
    hd = w_ref.shape[1] // 3
    per_res = tm // dil
    if dil > 1:
        for c in range(d // V7X_LANES):
            xs_ref[c] = x_ref[:, c * V7X_LANES:(c + 1) * V7X_LANES]
        for res in range(dil):
            for c in range(d // V7X_LANES):
                xp_ref[res * per_res:(res + 1) * per_res, c * V7X_LANES:(c + 1) * V7X_LANES] = (
                    xs_ref[c, pl.ds(res, per_res, stride=dil), :])
        x = xp_ref[...]
    else:
        x = x_ref[...]
    xn = _rms_norm(x, g_ref[...]).astype(BF16)
    cos, sin = cos_ref[...], sin_ref[...]

    def store(col0, val):
        for res in range(dil):
            piece = val[res * per_res:(res + 1) * per_res, :]
            o_ref[res, :, col0:col0 + val.shape[1]] = piece.astype(BF16)
            if col0 >= hd:
                kvf_ref[res, :, col0 - hd:col0 - hd + val.shape[1]] = piece

    for kind, gain_ref in enumerate((qg_ref, kg_ref, None)):
        y = jnp.dot(xn, w_ref[:, kind * hd:(kind + 1) * hd], preferred_element_type=F32)
        if gain_ref is None:
            store(kind * hd, y)
            continue
        gain = gain_ref[...]
        for h in range(hd // HEAD_DIM):
            yn = _rms_norm(y[:, h * HEAD_DIM:(h + 1) * HEAD_DIM], gain)
            store(kind * hd + h * HEAD_DIM, yn * cos + pltpu.roll(yn, HEAD_DIM // 2, axis=1) * sin)

    @pl.when(pl.program_id(0) % tiles >= tail_tile)
    def _():
        rows = kv_ref.shape[0]
        if dil == 1:
            kv_ref[...] = kvf_ref[0, tm - rows:, :]
            return
        slabs = xs_ref.shape[0]
        for half in range(2 * hd // (slabs * V7X_LANES)):
            col0 = half * slabs * V7X_LANES
            for res in range(dil):
                for c in range(slabs):
                    xs_ref[c, pl.ds(res, per_res, stride=dil), :] = (
                        kvf_ref[res, :, col0 + c * V7X_LANES:col0 + (c + 1) * V7X_LANES])
            for c in range(slabs):
                kv_ref[:, col0 + c * V7X_LANES:col0 + (c + 1) * V7X_LANES] = xs_ref[c]


def qkv_project(x, g, w_qkv, q_gain, k_gain, pos, group, *, dil, batch, tm, keep):
    n, d = x.shape
    seq = n // batch
    hd3 = w_qkv.shape[1] // N_GROUPS
    tiles = seq // tm
    per_res = tm // dil
    assert per_res % (2 * V7X_SUBLANES) == 0
    kv_rows = min(keep, tm)
    assert keep % kv_rows == 0 and (dil == 1 or kv_rows == tm)
    tail_tile = tiles - keep // kv_rows
    out_specs = [pl.BlockSpec((None, dil, per_res, hd3), lambda i: (i // tiles, 0, i % tiles, 0)),
                 pl.BlockSpec((None, kv_rows, 2 * hd3 // 3),
                              lambda i: (i // tiles, jnp.maximum(i % tiles - tail_tile, 0), 0))]
    out_shape = [jax.ShapeDtypeStruct((batch, dil, seq // dil, hd3), BF16),
                 jax.ShapeDtypeStruct((batch, keep, 2 * hd3 // 3), F32)]
    order = (np.arange(tiles)[:, None] * tm + _residue_order(tm, dil)[None, :]).reshape(seq)
    cos, sin = _rope_tables(jnp.asarray(np.asarray(pos)[order]))
    gain = pl.BlockSpec((None, 1, HEAD_DIM), lambda i: (group, 0, 0))
    rope = pl.BlockSpec((tm, HEAD_DIM), lambda i: (i % tiles, 0))
    return pl.pallas_call(
        functools.partial(_qkv_kernel, dil=dil, tiles=tiles, tail_tile=tail_tile),
        grid=(n // tm,),
        in_specs=[
            pl.BlockSpec((tm, d), lambda i: (i, 0)),
            pl.BlockSpec((1, d), lambda i: (0, 0)),
            pl.BlockSpec((d, hd3), lambda i: (0, group), pipeline_mode=pl.Buffered(1)),
            gain, gain, rope, rope,
        ],
        out_specs=out_specs,
        out_shape=out_shape,
        scratch_shapes=[pltpu.VMEM((d // V7X_LANES, tm, V7X_LANES), F32), pltpu.VMEM((tm, d), F32),
                        pltpu.VMEM((dil, per_res, 2 * hd3 // 3), F32)],
        compiler_params=_params("arbitrary"),
        name=f"qkv_project_g{group}_d{dil}",
    )(x, g, w_qkv, q_gain.reshape(N_GROUPS, 1, HEAD_DIM), k_gain.reshape(N_GROUPS, 1, HEAD_DIM), cos, sin)


def _spread_heads(cols_list, width):
    rows = cols_list[0].shape[0]
    lane = lax.broadcasted_iota(jnp.int32, (rows, width), 1)
    out = jnp.zeros((rows, width), F32)
    for h, col in enumerate(cols_list):
        out = jnp.where(lane == h, col, out)
    return out


def _attn_prompt_kernel(q_ref, kc_ref, kp_ref, vc_ref, vp_ref, o_ref, lse_ref, *, window_blocks):
    i = pl.program_id(2)
    qb = kp_ref.shape[0]
    blocks = q_ref.shape[0] // qb
    qi = lax.broadcasted_iota(jnp.int32, (qb, 2 * qb), 0)
    kj = lax.broadcasted_iota(jnp.int32, (qb, 2 * qb), 1)
    band = (kj <= qi + qb) & (kj >= qi + qb - window_blocks)
    for j in range(blocks):
        rows = slice(j * qb, (j + 1) * qb)
        live = band & ((i > 0) | (kj >= qb)) if j == 0 else band
        lses = []
        for h in range(q_ref.shape[1] // HEAD_DIM):
            cols = slice(h * HEAD_DIM, (h + 1) * HEAD_DIM)
            k_prev = kp_ref[:, cols] if j == 0 else kc_ref[(j - 1) * qb:j * qb, cols]
            v_prev = vp_ref[:, cols] if j == 0 else vc_ref[(j - 1) * qb:j * qb, cols]
            q = q_ref[rows, cols].astype(BF16)
            k = jnp.concatenate([k_prev, kc_ref[rows, cols]], axis=0).astype(BF16)
            v = jnp.concatenate([v_prev, vc_ref[rows, cols]], axis=0).astype(BF16)
            s = lax.dot_general(q, k, NT_DIMS, preferred_element_type=F32) * ATTN_SCALE
            s = jnp.where(live, s, NEG_INF)
            m = jnp.max(s, axis=-1, keepdims=True)
            p = jnp.exp(s - m)
            l = jnp.sum(p, axis=-1, keepdims=True)
            o_ref[rows, cols] = jnp.dot(p.astype(BF16), v, preferred_element_type=F32) / l
            lses.append(m + jnp.log(l))
        lse_ref[rows, :] = _spread_heads(lses, lse_ref.shape[1])


def attn_prompt_group(qkv, group):
    batch, dil, n_strided, hd3 = qkv.shape
    hd = hd3 // 3
    win = WINDOWS[group]
    step_rows = ATTN_STEP_BLOCKS * ATTN_BLOCK
    assert dil == DILATIONS[group] and n_strided % step_rows == 0 and win // dil <= ATTN_BLOCK
    blk = (None, None, step_rows, hd)

    def cur(kind):
        return pl.BlockSpec(blk, lambda b, r, i: (b, r, i, kind))

    def prev(kind):
        return pl.BlockSpec((None, None, ATTN_BLOCK, hd),
                            lambda b, r, i: (b, r, jnp.maximum(i * ATTN_STEP_BLOCKS - 1, 0), kind))

    return pl.pallas_call(
        functools.partial(_attn_prompt_kernel, window_blocks=win // dil),
        grid=(batch, dil, n_strided // step_rows),
        in_specs=[cur(0), cur(1), prev(1), cur(2), prev(2)],
        out_specs=[pl.BlockSpec(blk, lambda b, r, i: (b, r, i, 0)),
                   pl.BlockSpec((None, None, step_rows, V7X_LANES), lambda b, r, i: (b, r, i, 0))],
        out_shape=[jax.ShapeDtypeStruct((batch, dil, n_strided, hd), F32),
                   jax.ShapeDtypeStruct((batch, dil, n_strided, V7X_LANES), F32)],
        compiler_params=_params("arbitrary", "arbitrary", "arbitrary"),
        name=f"attn_prompt_g{group}",
    )(qkv, qkv, qkv, qkv, qkv)


def _head_segments(n_heads, transpose):
    shape = (V7X_LANES, n_heads * HEAD_DIM) if transpose else (n_heads * HEAD_DIM, V7X_LANES)
    c = lax.broadcasted_iota(jnp.int32, shape, 1 if transpose else 0)
    h = lax.broadcasted_iota(jnp.int32, shape, 0 if transpose else 1)
    return jnp.where((c >= h * HEAD_DIM) & (c < (h + 1) * HEAD_DIM), 1.0, 0.0).astype(BF16)


def _attn_sample_kernel(n0_ref, n1_ref, n2_ref, c0_ref, c1_ref, c2_ref, o_ref, lse_ref):
    t = pl.program_id(1)
    for g, (new_ref, cache_ref) in enumerate(zip((n0_ref, n1_ref, n2_ref), (c0_ref, c1_ref, c2_ref))):
        q = new_ref[pl.ds(t, 1), 0]
        k_new, v_new = new_ref[:, 1], new_ref[:, 2]
        k_old, v_old = cache_ref[:, 0], cache_ref[:, 1]
        s_old = jnp.sum(k_old * q, axis=-1, keepdims=True) * ATTN_SCALE
        s_new = jnp.sum(k_new * q, axis=-1, keepdims=True) * ATTN_SCALE
        row_n = lax.broadcasted_iota(jnp.int32, s_new.shape, 0)
        if DILATIONS[g] == 1:
            row_c = lax.broadcasted_iota(jnp.int32, s_old.shape, 0)
            s_old = jnp.where(row_c >= t, s_old, NEG_INF)
            s_new = jnp.where(row_n <= t, s_new, NEG_INF)
        else:
            s_new = jnp.where(row_n == t, s_new, NEG_INF)
        m = jnp.maximum(jnp.max(s_old, axis=0, keepdims=True), jnp.max(s_new, axis=0, keepdims=True))
        p_old, p_new = jnp.exp(s_old - m), jnp.exp(s_new - m)
        l = jnp.sum(p_old, axis=0, keepdims=True) + jnp.sum(p_new, axis=0, keepdims=True)
        o = jnp.sum(p_old * v_old, axis=0, keepdims=True) + jnp.sum(p_new * v_new, axis=0, keepdims=True)
        o_ref[g, pl.ds(t, 1)] = o / l
        lse_ref[g, pl.ds(t, 1)] = m + jnp.log(l)


def attn_sample(qkv, caches, *, batch, t_new):
    n_heads = caches[0].shape[3]
    views, specs = [], []
    for g, cache in enumerate(caches):
        dil, keep = DILATIONS[g], cache.shape[1]
        assert keep == WINDOWS[g] and keep // dil == ATTN_BLOCK and (dil == 1 or t_new <= dil)
        views.append(cache.reshape(batch, keep // dil, dil, 2, n_heads, HEAD_DIM))
        specs.append(pl.BlockSpec((None, ATTN_BLOCK, None, 2, n_heads, HEAD_DIM),
                                  (lambda b, t: (b, 0, 0, 0, 0, 0)) if dil == 1 else
                                  (lambda b, t: (b, 0, t, 0, 0, 0))))
    rows_new = -(-t_new // V7X_SUBLANES) * V7X_SUBLANES
    new_rows = [jnp.pad(a.reshape(batch, t_new, 3, n_heads, HEAD_DIM),
                        ((0, 0), (0, rows_new - t_new), (0, 0), (0, 0), (0, 0))) for a in qkv]
    o, lse = pl.pallas_call(
        _attn_sample_kernel,
        grid=(batch, t_new),
        in_specs=[pl.BlockSpec((None, rows_new, 3, n_heads, HEAD_DIM), lambda b, t: (b, 0, 0, 0, 0))] * N_GROUPS
        + specs,
        out_specs=[pl.BlockSpec((N_GROUPS, None, t_new, n_heads, HEAD_DIM), lambda b, t: (0, b, 0, 0, 0)),
                   pl.BlockSpec((N_GROUPS, None, t_new, n_heads, 1), lambda b, t: (0, b, 0, 0, 0))],
        out_shape=[jax.ShapeDtypeStruct((N_GROUPS, batch, t_new, n_heads, HEAD_DIM), F32),
                   jax.ShapeDtypeStruct((N_GROUPS, batch, t_new, n_heads, 1), F32)],
        compiler_params=_params("arbitrary", "arbitrary"),
        name="attn_sample",
    )(*new_rows, *views)
    lse = jnp.pad(lse.reshape(N_GROUPS, batch * t_new, n_heads), ((0, 0), (0, 0), (0, V7X_LANES - n_heads)))
    return o.reshape(N_GROUPS, batch * t_new, n_heads * HEAD_DIM), lse


def _attn_out_kernel(x_ref, o0_ref, o1_ref, o2_ref, l0_ref, l1_ref, l2_ref, wo_ref, y_ref,
                     og_ref, lg_ref, mix_ref):
    tm = x_ref.shape[0]
    hd = wo_ref.shape[0]
    heads = hd // HEAD_DIM
    for g, (o_ref, l_ref) in enumerate(zip((o0_ref, o1_ref, o2_ref), (l0_ref, l1_ref, l2_ref))):
        dil = o_ref.shape[0]
        for res in range(dil):
            rows = pl.ds(res, tm // dil, stride=dil) if dil > 1 else slice(0, tm)
            lg_ref[g, rows, :] = l_ref[res]
            for h in range(heads):
                og_ref[g, h, rows, :] = o_ref[res, :, h * HEAD_DIM:(h + 1) * HEAD_DIM]
    lse = [lg_ref[g] for g in range(N_GROUPS)]
    m = functools.reduce(jnp.maximum, lse)
    e = [jnp.exp(l - m) for l in lse]
    tot = functools.reduce(jnp.add, e)
    seg_t = _head_segments(heads, transpose=True)
    w = [_split_dot(e[g] / tot, seg_t) for g in range(N_GROUPS)]
    for h in range(heads):
        cols = slice(h * HEAD_DIM, (h + 1) * HEAD_DIM)
        mix = functools.reduce(jnp.add, [w[g][:, cols] * og_ref[g, h] for g in range(N_GROUPS)])
        mix_ref[:, cols] = mix.astype(BF16)
    y_ref[...] = x_ref[...] + jnp.dot(mix_ref[...], wo_ref[...], preferred_element_type=F32)


def attn_out(x, outs, lses, w_o, *, tm):
    n, d = x.shape
    batch = outs[0].shape[0]
    hd = outs[0].shape[3]
    tiles = n // batch // tm

    def by_residue(a):
        dil = a.shape[1]
        return pl.BlockSpec((None, dil, tm // dil, a.shape[3]), lambda i: (i // tiles, 0, i % tiles, 0))

    return pl.pallas_call(
        _attn_out_kernel,
        grid=(n // tm,),
        in_specs=[pl.BlockSpec((tm, d), lambda i: (i, 0))] + [by_residue(a) for a in (*outs, *lses)]
        + [pl.BlockSpec((hd, d), lambda i: (0, 0))],
        out_specs=pl.BlockSpec((tm, d), lambda i: (i, 0)),
        out_shape=jax.ShapeDtypeStruct((n, d), F32),
        scratch_shapes=[pltpu.VMEM((N_GROUPS, hd // HEAD_DIM, tm, HEAD_DIM), F32),
                        pltpu.VMEM((N_GROUPS, tm, V7X_LANES), F32), pltpu.VMEM((tm, hd), BF16)],
        compiler_params=_params("arbitrary"),
        name="attn_out",
    )(x, *outs, *lses, w_o)


def _kv_state_kernel(c_ref, next_ref, n_ref, o_ref):
    i = pl.program_id(1)
    last = pl.num_programs(1) - 1
    rows, t_new = o_ref.shape[0], n_ref.shape[0]
    o_ref[0:rows - t_new] = c_ref[t_new:rows]
    o_ref[rows - t_new:rows] = jnp.where(i == last, n_ref[...], next_ref[0:t_new])


def kv_state_update(cache, new, *, rows):
    batch, keep = cache.shape[:2]
    t_new = new.shape[1]
    tail = cache.shape[2:]
    blocks = keep // rows
    head_rows = V7X_SUBLANES
    assert keep % rows == 0 and rows % head_rows == 0 and t_new <= head_rows < rows and new.shape[2:] == tail
    zeros = (0,) * len(tail)
    heads_per_block = rows // head_rows

    def next_head(b, i):
        return (b, jnp.minimum((i + 1) * heads_per_block, keep // head_rows - 1)) + zeros

    return pl.pallas_call(
        _kv_state_kernel,
        grid=(batch, blocks),
        in_specs=[pl.BlockSpec((None, rows) + tail, lambda b, i: (b, i) + zeros),
                  pl.BlockSpec((None, head_rows) + tail, next_head),
                  pl.BlockSpec((None, t_new) + tail, lambda b, i: (b, 0) + zeros)],
        out_specs=pl.BlockSpec((None, rows) + tail, lambda b, i: (b, i) + zeros),
        out_shape=jax.ShapeDtypeStruct(cache.shape, cache.dtype),
        compiler_params=_params("arbitrary", "arbitrary"),
        name="kv_state_update",
    )(cache, cache, new)


def _rope_tables(pos):
    half = HEAD_DIM // 2
    inv_freq = ROPE_THETA ** (-jnp.arange(half, dtype=F32) / half)
    ang = pos.astype(F32)[:, None] * inv_freq
    cos, sin = jnp.cos(ang), jnp.sin(ang)
    return jnp.concatenate([cos, cos], axis=-1), jnp.concatenate([-sin, sin], axis=-1)


def _pad_rows(a, rows):
    return jnp.pad(a, ((0, rows - a.shape[0]),) + ((0, 0),) * (a.ndim - 1))


PROMPT_TM = 512
ATTN_STEP_BLOCKS = 2
ATTN_OUT_TM = 256
QKV_TM = 256
PEER_TE = 1024
CONV_TN = 1024
SAMPLE_PEER_ROWS = V7X_LANES


def kernel(x_prompt, x_sample, state_conv, cache_kv_w128, cache_kv_w512, cache_kv_w2048,
           norm_mix, norm_ffn, conv_w_in, conv_w, conv_w_out,
           attn_w_qkv, attn_q_norm, attn_k_norm, attn_w_o,
           peer_w_q, peer_sub_keys, peer_u, peer_v):
    batch, seq, d = x_prompt.shape
    dec_batch, dec_seq, _ = x_sample.shape
    n_p, n_s = batch * seq, dec_batch * dec_seq
    caches = (cache_kv_w128, cache_kv_w512, cache_kv_w2048)
    n_heads = caches[0].shape[3]
    hd = n_heads * HEAD_DIM

    w_in, w_out = conv_w_in.astype(BF16), conv_w_out.astype(BF16)
    w_qkv, w_o = attn_w_qkv.astype(BF16), attn_w_o.astype(BF16)
    w_q = peer_w_q.astype(BF16)
    keys = peer_sub_keys.astype(BF16).reshape(peer_sub_keys.shape[0], 2 * PEER_HEADS, PEER_N_KEYS, PEER_D_HALF)
    pu, pv = peer_u.astype(BF16), peer_v.astype(BF16)
    g_mix = norm_mix.reshape(-1, 1, d)
    g_ffn = norm_ffn.reshape(-1, 1, d)

    xp = x_prompt.reshape(n_p, d)
    xs = x_sample.reshape(n_s, d)

    def peer(x, layer, tm):
        return peer_layer(x, g_ffn[layer], w_q, keys, pu, pv, layer, tm=tm, te=PEER_TE)

    def peer_sample(x, layer):
        return peer(_pad_rows(x, SAMPLE_PEER_ROWS), layer, SAMPLE_PEER_ROWS)[:n_s]

    bp, up = conv_in(xp, g_mix[0], w_in, tm=PROMPT_TM, tn=CONV_TN)
    xp = conv_out_prompt(xp, bp, up, jnp.zeros((batch, CONV_WIDTH - 1, d), F32), conv_w, w_out,
                         seq_len=seq, tm=PROMPT_TM)
    conv_state_prompt = up.reshape(batch, seq, d)[:, seq - (CONV_WIDTH - 1):]

    bs, us = conv_in(xs, g_mix[0], w_in, tm=n_s, tn=CONV_TN)
    u_ext = jnp.concatenate([state_conv, us.reshape(dec_batch, dec_seq, d)], axis=1)
    shifted = [u_ext[:, k:k + dec_seq].reshape(n_s, d) for k in range(CONV_WIDTH)]
    xs = conv_out_sample(xs, bs, shifted[0], shifted[1], shifted[2], conv_w, w_out)
    conv_state_sample = u_ext[:, dec_seq:]

    xp = peer(xp, 0, PROMPT_TM)
    xs = peer_sample(xs, 0)

    qkv_p, kv_prompt = zip(*[
        qkv_project(xp, g_mix[1], w_qkv, attn_q_norm, attn_k_norm, np.arange(seq), g,
                    dil=DILATIONS[g], batch=batch, tm=QKV_TM, keep=min(WINDOWS[g], seq)) for g in range(N_GROUPS)])
    outs, lses = zip(*[attn_prompt_group(qkv_p[g], g) for g in range(N_GROUPS)])
    xp = attn_out(xp, outs, lses, w_o, tm=ATTN_OUT_TM)

    pos_s = np.tile(PAST_LEN + np.arange(dec_seq), dec_batch)
    q_s, kv_s = zip(*[qkv_project(xs, g_mix[1], w_qkv, attn_q_norm, attn_k_norm, pos_s, g,
                                  dil=1, batch=1, tm=n_s, keep=n_s) for g in range(N_GROUPS)])
    kv_s = [kv.reshape(n_s, 2 * hd) for kv in kv_s]
    qkv_s = [jnp.concatenate([q.reshape(n_s, 3 * hd)[:, :hd].astype(F32), kv], axis=1) for q, kv in zip(q_s, kv_s)]
    o_s, lse_s = attn_sample(qkv_s, caches, batch=dec_batch, t_new=dec_seq)
    xs = attn_out(xs, [o_s[g].reshape(1, 1, n_s, hd) for g in range(N_GROUPS)],
                  [lse_s[g].reshape(1, 1, n_s, V7X_LANES) for g in range(N_GROUPS)], w_o, tm=n_s)

    xp = peer(xp, 1, PROMPT_TM)
    xs = peer_sample(xs, 1)

    kv_prompt = [kv.reshape(batch, kv.shape[1], 2, n_heads, HEAD_DIM) for kv in kv_prompt]
    kv_sample = [kv_state_update(cache, kv.reshape(dec_batch, dec_seq, 2, n_heads, HEAD_DIM), rows=ATTN_BLOCK)
                 for cache, kv in zip(caches, kv_s)]

    return (xp.reshape(batch, seq, d), xs.reshape(dec_batch, dec_seq, d),
            conv_state_prompt, conv_state_sample, *kv_prompt, *kv_sample)
```

```python
import functools
import math

import jax
import jax.numpy as jnp
import numpy as np
from jax import lax
from jax.experimental import pallas as pl
from jax.experimental.pallas import tpu as pltpu

F32 = jnp.float32
BF16 = jnp.bfloat16

V7X_LANES = 128
V7X_SUBLANES = 8
V7X_VMEM_BYTES = 64 * 1024 * 1024
VMEM_LIMIT_BYTES = V7X_VMEM_BYTES - 8 * 1024 * 1024

NORM_EPS = 1e-6
CONV_WIDTH = 3
WINDOWS = (128, 512, 2048)
DILATIONS = (1, 4, 16)
N_GROUPS = 3
HEAD_DIM = 128
ATTN_BLOCK = 128
ATTN_SCALE = HEAD_DIM ** -0.5
ROPE_THETA = 10000.0
PAST_LEN = 16384
assert PAST_LEN >= max(WINDOWS)
PEER_HEADS = 8
PEER_N_KEYS = 128
PEER_TOPK = 16
PEER_D_HALF = 128
SQRT_HALF = math.sqrt(0.5)
NOT_SELECTED = 31.0
NEG_INF = float("-inf")

NT_DIMS = (((1,), (1,)), ((), ()))
TN_DIMS = (((0,), (0,)), ((), ()))


def _params(*semantics):
    return pltpu.CompilerParams(dimension_semantics=semantics, vmem_limit_bytes=VMEM_LIMIT_BYTES)


def _rms_norm(x, g):
    return x * lax.rsqrt(jnp.mean(x * x, axis=-1, keepdims=True) + NORM_EPS) * g


def _split_dot(a, b01):
    hi = a.astype(BF16)
    lo = (a - hi.astype(F32)).astype(BF16)
    return (jnp.dot(hi, b01, preferred_element_type=F32)
            + jnp.dot(lo, b01, preferred_element_type=F32))


def _conv_in_kernel(x_ref, g_ref, wb_ref, wc_ref, wh_ref, b_ref, u_ref, xn_ref):
    @pl.when(pl.program_id(1) == 0)
    def _():
        xn_ref[...] = _rms_norm(x_ref[...], g_ref[...]).astype(BF16)

    xn = xn_ref[...]
    b_ref[...] = jnp.dot(xn, wb_ref[...], preferred_element_type=F32)
    c = jnp.dot(xn, wc_ref[...], preferred_element_type=F32)
    h = jnp.dot(xn, wh_ref[...], preferred_element_type=F32)
    u_ref[...] = c * h


def conv_in(x, g, w_in, *, tm, tn):
    n, d = x.shape
    nj = d // tn
    return pl.pallas_call(
        _conv_in_kernel,
        grid=(n // tm, nj),
        in_specs=[
            pl.BlockSpec((tm, d), lambda i, j: (i, 0)),
            pl.BlockSpec((1, d), lambda i, j: (0, 0)),
            pl.BlockSpec((d, tn), lambda i, j: (0, j)),
            pl.BlockSpec((d, tn), lambda i, j: (0, j + nj)),
            pl.BlockSpec((d, tn), lambda i, j: (0, j + 2 * nj)),
        ],
        out_specs=[pl.BlockSpec((tm, tn), lambda i, j: (i, j)),
                   pl.BlockSpec((tm, tn), lambda i, j: (i, j))],
        out_shape=[jax.ShapeDtypeStruct((n, d), F32), jax.ShapeDtypeStruct((n, d), F32)],
        scratch_shapes=[pltpu.VMEM((tm, d), BF16)],
        compiler_params=_params("arbitrary", "arbitrary"),
        name="conv_in",
    )(x, g, w_in, w_in, w_in)


def _conv_mix(cw_ref, um2, um1, u0):
    return cw_ref[0:1, :] * um2 + cw_ref[1:2, :] * um1 + cw_ref[2:3, :] * u0


def _conv_out_prompt_kernel(x_ref, b_ref, u_ref, up_ref, st_ref, cw_ref, wo_ref, o_ref, *, tiles_per_seq):
    i = pl.program_id(0)
    u = u_ref[...]
    first = (i % tiles_per_seq) == 0
    prev = jnp.where(first, st_ref[...], up_ref[V7X_SUBLANES - 2:V7X_SUBLANES, :])
    row = lax.broadcasted_iota(jnp.int32, (u.shape[0], 1), 0)
    um1 = jnp.where(row == 0, prev[1:2, :], pltpu.roll(u, 1, axis=0))
    um2 = jnp.where(row == 0, prev[0:1, :], jnp.where(row == 1, prev[1:2, :], pltpu.roll(u, 2, axis=0)))
    a = (b_ref[...] * _conv_mix(cw_ref, um2, um1, u)).astype(BF16)
    o_ref[...] = x_ref[...] + jnp.dot(a, wo_ref[...], preferred_element_type=F32)


def conv_out_prompt(x, bgate, u, state, conv_w, w_out, *, seq_len, tm):
    n, d = x.shape
    tps = seq_len // tm
    sub_per_tile = tm // V7X_SUBLANES
    tile = pl.BlockSpec((tm, d), lambda i: (i, 0))
    return pl.pallas_call(
        functools.partial(_conv_out_prompt_kernel, tiles_per_seq=tps),
        grid=(n // tm,),
        in_specs=[
            tile, tile, tile,
            pl.BlockSpec((V7X_SUBLANES, d), lambda i: (jnp.maximum(i * sub_per_tile - 1, 0), 0)),
            pl.BlockSpec((None, CONV_WIDTH - 1, d), lambda i: (i // tps, 0, 0)),
            pl.BlockSpec((CONV_WIDTH, d), lambda i: (0, 0)),
            pl.BlockSpec((d, d), lambda i: (0, 0)),
        ],
        out_specs=tile,
        out_shape=jax.ShapeDtypeStruct((n, d), F32),
        compiler_params=_params("arbitrary"),
        name="conv_out_prompt",
    )(x, bgate, u, u, state, conv_w, w_out)


def _conv_out_sample_kernel(x_ref, b_ref, um2_ref, um1_ref, u_ref, cw_ref, wo_ref, o_ref):
    a = (b_ref[...] * _conv_mix(cw_ref, um2_ref[...], um1_ref[...], u_ref[...])).astype(BF16)
    o_ref[...] = x_ref[...] + jnp.dot(a, wo_ref[...], preferred_element_type=F32)


def conv_out_sample(x, bgate, um2, um1, u0, conv_w, w_out):
    n, d = x.shape
    full = pl.BlockSpec((n, d), lambda i: (0, 0))
    return pl.pallas_call(
        _conv_out_sample_kernel,
        grid=(1,),
        in_specs=[full, full, full, full, full,
                  pl.BlockSpec((CONV_WIDTH, d), lambda i: (0, 0)),
                  pl.BlockSpec((d, d), lambda i: (0, 0))],
        out_specs=full,
        out_shape=jax.ShapeDtypeStruct((n, d), F32),
        compiler_params=_params("arbitrary"),
        name="conv_out_sample",
    )(x, bgate, um2, um1, u0, conv_w, w_out)


CAND_ROWS = 72


def _cand_flat_index():
    pairs = []
    for b in range(4):
        pairs += [(a, b) for a in range(16 if b == 0 else 8)]
    pairs += [(0, b) if b >= 4 else None for b in range(16)]
    pairs += [(1, b) if b >= 4 else None for b in range(8)]
    pairs += [(2, b) if b == 4 else None for b in range(8)]
    rows = [a * PEER_TOPK + b if pair is not None and (a + 1) * (b + 1) <= PEER_TOPK else 256 + i
            for i, pair in enumerate(pairs) for a, b in [pair or (99, 99)]]
    assert len(rows) == CAND_ROWS and len({r for r in rows if r < 256}) == 50
    return jnp.broadcast_to(jnp.asarray(rows, F32).reshape(CAND_ROWS, 1), (CAND_ROWS, V7X_LANES))


def _top16(s, key_iota, exact_ties):
    work = s
    rank = jnp.full(s.shape, NOT_SELECTED, F32)
    val_row = lax.broadcasted_iota(jnp.int32, (PEER_TOPK, s.shape[1]), 0)
    vals = jnp.zeros((PEER_TOPK, s.shape[1]), F32)
    for a in range(PEER_TOPK):
        m = jnp.max(work, axis=0, keepdims=True)
        sel = work == m
        if exact_ties:
            first = jnp.min(jnp.where(sel, key_iota, float(PEER_N_KEYS)), axis=0, keepdims=True)
            sel = key_iota == first
        rank = jnp.where(sel, float(a), rank)
        work = jnp.where(sel, NEG_INF, work)
        vals = jnp.where(val_row == a, m, vals)
    return vals, rank


def _ranked_rows(rank):
    return jnp.sum(jnp.where(rank < NOT_SELECTED, 1.0, 0.0), axis=0, keepdims=True)


def _peer_select(s1, s2, fidx, exact_ties):
    lanes = s1.shape[1]
    key_iota = lax.broadcasted_iota(jnp.int32, (PEER_N_KEYS, lanes), 0).astype(F32)
    v1, rank1 = _top16(s1, key_iota, exact_ties)
    v2, rank2 = _top16(s2, key_iota, exact_ties)

    r16 = lax.broadcasted_iota(jnp.int32, (16, lanes), 0)
    r8 = lax.broadcasted_iota(jnp.int32, (8, lanes), 0)
    v1_lo, v2_lo = v1[0:8, :], v2[0:8, :]
    cand = jnp.concatenate([
        v1 + v2[0:1, :],
        v1_lo + v2[1:2, :],
        jnp.where(r8 < 5, v1_lo + v2[2:3, :], NEG_INF),
        jnp.where(r8 < 4, v1_lo + v2[3:4, :], NEG_INF),
        jnp.where(r16 >= 4, v1[0:1, :] + v2, NEG_INF),
        jnp.where(r8 >= 4, v1[1:2, :] + v2_lo, NEG_INF),
        jnp.where(r8 == 4, v1[2:3, :] + v2_lo, NEG_INF),
    ], axis=0)
    c_top = v1[0:1, :] + v2[0:1, :]
    taken = jnp.zeros(cand.shape, F32)
    z = jnp.zeros((1, lanes), F32)
    for _ in range(PEER_TOPK):
        m = jnp.max(cand, axis=0, keepdims=True)
        sel = cand == m
        if exact_ties:
            first = jnp.min(jnp.where(sel, fidx, 1024.0), axis=0, keepdims=True)
            sel = fidx == first
        taken = jnp.where(sel, 1.0, taken)
        cand = jnp.where(sel, NEG_INF, cand)
        z = z + jnp.exp(m - c_top)

    extra = jnp.where(r8 == 0, jnp.sum(taken[40:56, :], axis=0, keepdims=True),
                      jnp.where(r8 == 1, jnp.sum(taken[56:64, :], axis=0, keepdims=True),
                                jnp.where(r8 == 2, jnp.sum(taken[64:72, :], axis=0, keepdims=True), 0.0)))
    cnt_lo = taken[0:8, :] + taken[16:24, :] + taken[24:32, :] + taken[32:40, :] + extra
    cnt_vec = jnp.concatenate([cnt_lo, taken[8:16, :]], axis=0)
    cnt = jnp.zeros(s1.shape, F32)
    for a in range(PEER_TOPK):
        cnt = jnp.where(rank1 == float(a), cnt_vec[a:a + 1, :], cnt)

    e1n = jnp.exp(s1 - v1[0:1, :]) / z
    e2 = jnp.exp(s2 - v2[0:1, :])
    ranked = jnp.maximum(jnp.maximum(_ranked_rows(rank1), _ranked_rows(rank2)),
                         jnp.sum(taken, axis=0, keepdims=True))
    return rank2, e2, cnt, e1n, ranked


def _peer_query_kernel(x_ref, g_ref, wq_ref, keys_ref, fidx_ref,
                       xn_ref, re2_ref, cnt_ref, e1n_ref, sc_ref):
    tm = x_ref.shape[0]
    xn = _rms_norm(x_ref[...], g_ref[...]).astype(BF16)
    xn_ref[...] = xn
    q = jnp.dot(xn, wq_ref[...], preferred_element_type=F32).astype(BF16)
    for rp in range(2 * PEER_HEADS):
        q_rp = q[:, rp * PEER_D_HALF:(rp + 1) * PEER_D_HALF]
        sc_ref[rp] = lax.dot_general(keys_ref[rp], q_rp, NT_DIMS, preferred_element_type=F32)

    fidx = fidx_ref[...]

    def select_head(r, exact_ties):
        ranked = jnp.zeros((1, V7X_LANES), F32)
        for s in range(tm // V7X_LANES):
            cols = slice(s * V7X_LANES, (s + 1) * V7X_LANES)
            rank2, e2, cnt, e1n, n = _peer_select(sc_ref[2 * r, :, cols], sc_ref[2 * r + 1, :, cols], fidx,
                                                  exact_ties)
            re2_ref[s, r] = rank2 + 0.5 * e2
            cnt_ref[r, :, cols] = cnt
            e1n_ref[r, :, cols] = e1n
            ranked = jnp.maximum(ranked, n)
        return ranked

    def head_body(r, carry):
        ranked = select_head(r, exact_ties=False)

        @pl.when(jnp.max(ranked) > float(PEER_TOPK))
        def _():
            select_head(r, exact_ties=True)

        return carry

    lax.fori_loop(0, PEER_HEADS, head_body, 0)


def peer_query(x, g, w_q, keys, layer, *, tm):
    n, d = x.shape
    dq = w_q.shape[2]
    groups = tm // V7X_LANES
    by_i2 = jax.ShapeDtypeStruct((n // V7X_LANES, PEER_HEADS, PEER_N_KEYS, V7X_LANES), F32)
    by_i2_spec = pl.BlockSpec((groups, PEER_HEADS, PEER_N_KEYS, V7X_LANES), lambda i: (i, 0, 0, 0))
    sel = jax.ShapeDtypeStruct((PEER_HEADS, PEER_N_KEYS, n), F32)
    sel_spec = pl.BlockSpec((PEER_HEADS, PEER_N_KEYS, tm), lambda i: (0, 0, i))
    return pl.pallas_call(
        _peer_query_kernel,
        grid=(n // tm,),
        in_specs=[
            pl.BlockSpec((tm, d), lambda i: (i, 0)),
            pl.BlockSpec((1, d), lambda i: (0, 0)),
            pl.BlockSpec((None, d, dq), lambda i: (layer, 0, 0)),
            pl.BlockSpec((None, 2 * PEER_HEADS, PEER_N_KEYS, PEER_D_HALF), lambda i: (layer, 0, 0, 0)),
            pl.BlockSpec((CAND_ROWS, V7X_LANES), lambda i: (0, 0)),
        ],
        out_specs=[pl.BlockSpec((tm, d), lambda i: (i, 0)), by_i2_spec, sel_spec, sel_spec],
        out_shape=[jax.ShapeDtypeStruct((n, d), BF16), by_i2, sel, sel],
        scratch_shapes=[pltpu.VMEM((2 * PEER_HEADS, PEER_N_KEYS, tm), F32)],
        compiler_params=_params("arbitrary"),
        name="peer_query",
    )(x, g, w_q, keys, _cand_flat_index())


GATE_ROWS = 32


def _gate_chunk(h, a_ref, re2_ref, cnt_ref, e1n_ref):
    te, tm = h.shape
    for k in range(te // PEER_N_KEYS):
        for s in range(tm // V7X_LANES):
            cols = slice(s * V7X_LANES, (s + 1) * V7X_LANES)
            cnt = [cnt_ref[r, k:k + 1, cols] for r in range(PEER_HEADS)]
            e1n = [2.0 * e1n_ref[r, k:k + 1, cols] for r in range(PEER_HEADS)]
            for i2 in range(0, PEER_N_KEYS, GATE_ROWS):
                gate = jnp.zeros((GATE_ROWS, V7X_LANES), F32)
                for r in range(PEER_HEADS):
                    re2 = re2_ref[s, r, i2:i2 + GATE_ROWS, :]
                    live = re2 < cnt[r]
                    gate = gate + jnp.where(live, re2 - jnp.floor(re2), 0.0) * e1n[r]
                row0 = k * PEER_N_KEYS + i2
                hp = h[row0:row0 + GATE_ROWS, cols]
                act = 0.5 * hp * (1.0 + lax.erf(hp * SQRT_HALF))
                a_ref[row0:row0 + GATE_ROWS, cols] = (gate * act).astype(BF16)


def _peer_expert_kernel(xn_ref, x_ref, re2_ref, cnt_ref, e1n_ref, u_ref, v_ref, o_ref, a_ref):
    c = pl.program_id(1)

    @pl.when(c == 0)
    def _():
        o_ref[...] = x_ref[...]

    h = lax.dot_general(u_ref[...], xn_ref[...], NT_DIMS, preferred_element_type=F32)
    _gate_chunk(h, a_ref, re2_ref, cnt_ref, e1n_ref)
    o_ref[...] += lax.dot_general(a_ref[...], v_ref[...], TN_DIMS, preferred_element_type=F32)


def peer_experts(xn, x, re2, cnt, e1n, u, v, layer, *, tm, te):
    n, d = x.shape
    tok = pl.BlockSpec((tm, d), lambda i, c: (i, 0))
    groups = tm // V7X_LANES
    by_i2 = pl.BlockSpec((groups, PEER_HEADS, PEER_N_KEYS, V7X_LANES), lambda i, c: (i, 0, 0, 0))
    by_i1 = pl.BlockSpec((PEER_HEADS, te // PEER_N_KEYS, tm), lambda i, c: (0, c, i))
    wts = pl.BlockSpec((None, te, d), lambda i, c: (layer, c, 0))
    return pl.pallas_call(
        _peer_expert_kernel,
        grid=(n // tm, u.shape[1] // te),
        in_specs=[tok, tok, by_i2, by_i1, by_i1, wts, wts],
        out_specs=tok,
        out_shape=jax.ShapeDtypeStruct((n, d), F32),
        scratch_shapes=[pltpu.VMEM((te, tm), BF16)],
        compiler_params=_params("arbitrary", "arbitrary"),
        name="peer_experts",
    )(xn, x, re2, cnt, e1n, u, v)


def peer_layer(x, g, w_q, keys, u, v, layer, *, tm, te):
    xn, re2, cnt, e1n = peer_query(x, g, w_q, keys, layer, tm=tm)
    return peer_experts(xn, x, re2, cnt, e1n, u, v, layer, tm=tm, te=te)


def _residue_order(tm, dil):
    return np.arange(tm).reshape(tm // dil, dil).T.reshape(tm)


def _qkv_kernel(x_ref, g_ref, w_ref, qg_ref, kg_ref, cos_ref, sin_ref, o_ref, kv_ref, xs_ref, xp_ref, kvf_ref,
                *, dil, tiles, tail_tile):
    tm, d = x_ref.shape
    hd = w_ref.shape[1] // 3
    per_res = tm // dil
    if dil > 1:
        for c in range(d // V7X_LANES):
            xs_ref[c] = x_ref[:, c * V7X_LANES:(c + 1) * V7X_LANES]
        for res in range(dil):
            for c in range(d // V7X_LANES):
                xp_ref[res * per_res:(res + 1) * per_res, c * V7X_LANES:(c + 1) * V7X_LANES] = (
                    xs_ref[c, pl.ds(res, per_res, stride=dil), :])
        x = xp_ref[...]
    else:
        x = x_ref[...]
    xn = _rms_norm(x, g_ref[...]).astype(BF16)
    cos, sin = cos_ref[...], sin_ref[...]

    def store(col0, val):
        for res in range(dil):
            piece = val[res * per_res:(res + 1) * per_res, :]
            o_ref[res, :, col0:col0 + val.shape[1]] = piece.astype(BF16)
            if col0 >= hd:
                kvf_ref[res, :, col0 - hd:col0 - hd + val.shape[1]] = piece

    for kind, gain_ref in enumerate((qg_ref, kg_ref, None)):
        y = jnp.dot(xn, w_ref[:, kind * hd:(kind + 1) * hd], preferred_element_type=F32)
        if gain_ref is None:
            store(kind * hd, y)
            continue
        gain = gain_ref[...]
        for h in range(hd // HEAD_DIM):
            yn = _rms_norm(y[:, h * HEAD_DIM:(h + 1) * HEAD_DIM], gain)
            store(kind * hd + h * HEAD_DIM, yn * cos + pltpu.roll(yn, HEAD_DIM // 2, axis=1) * sin)

    @pl.when(pl.program_id(0) % tiles >= tail_tile)
    def _():
        rows = kv_ref.shape[0]
        if dil == 1:
            kv_ref[...] = kvf_ref[0, tm - rows:, :]
            return
        slabs = xs_ref.shape[0]
        for half in range(2 * hd // (slabs * V7X_LANES)):
            col0 = half * slabs * V7X_LANES
            for res in range(dil):
                for c in range(slabs):
                    xs_ref[c, pl.ds(res, per_res, stride=dil), :] = (
                        kvf_ref[res, :, col0 + c * V7X_LANES:col0 + (c + 1) * V7X_LANES])
            for c in range(slabs):
                kv_ref[:, col0 + c * V7X_LANES:col0 + (c + 1) * V7X_LANES] = xs_ref[c]


def qkv_project(x, g, w_qkv, q_gain, k_gain, pos, group, *, dil, batch, tm, keep):
    n, d = x.shape
    seq = n // batch
    hd3 = w_qkv.shape[1] // N_GROUPS
    tiles = seq // tm
    per_res = tm // dil
    assert per_res % (2 * V7X_SUBLANES) == 0
    kv_rows = min(keep, tm)
    assert keep % kv_rows == 0 and (dil == 1 or kv_rows == tm)
    tail_tile = tiles - keep // kv_rows
    out_specs = [pl.BlockSpec((None, dil, per_res, hd3), lambda i: (i // tiles, 0, i % tiles, 0)),
                 pl.BlockSpec((None, kv_rows, 2 * hd3 // 3),
                              lambda i: (i // tiles, jnp.maximum(i % tiles - tail_tile, 0), 0))]
    out_shape = [jax.ShapeDtypeStruct((batch, dil, seq // dil, hd3), BF16),
                 jax.ShapeDtypeStruct((batch, keep, 2 * hd3 // 3), F32)]
    order = (np.arange(tiles)[:, None] * tm + _residue_order(tm, dil)[None, :]).reshape(seq)
    cos, sin = _rope_tables(jnp.asarray(np.asarray(pos)[order]))
    gain = pl.BlockSpec((None, 1, HEAD_DIM), lambda i: (group, 0, 0))
    rope = pl.BlockSpec((tm, HEAD_DIM), lambda i: (i % tiles, 0))
    return pl.pallas_call(
        functools.partial(_qkv_kernel, dil=dil, tiles=tiles, tail_tile=tail_tile),
        grid=(n // tm,),
        in_specs=[
            pl.BlockSpec((tm, d), lambda i: (i, 0)),
            pl.BlockSpec((1, d), lambda i: (0, 0)),
            pl.BlockSpec((d, hd3), lambda i: (0, group), pipeline_mode=pl.Buffered(1)),
            gain, gain, rope, rope,
        ],
        out_specs=out_specs,
        out_shape=out_shape,
        scratch_shapes=[pltpu.VMEM((d // V7X_LANES, tm, V7X_LANES), F32), pltpu.VMEM((tm, d), F32),
                        pltpu.VMEM((dil, per_res, 2 * hd3 // 3), F32)],
        compiler_params=_params("arbitrary"),
        name=f"qkv_project_g{group}_d{dil}",
    )(x, g, w_qkv, q_gain.reshape(N_GROUPS, 1, HEAD_DIM), k_gain.reshape(N_GROUPS, 1, HEAD_DIM), cos, sin)


def _spread_heads(cols_list, width):
    rows = cols_list[0].shape[0]
    lane = lax.broadcasted_iota(jnp.int32, (rows, width), 1)
    out = jnp.zeros((rows, width), F32)
    for h, col in enumerate(cols_list):
        out = jnp.where(lane == h, col, out)
    return out


def _attn_prompt_kernel(q_ref, kc_ref, kp_ref, vc_ref, vp_ref, o_ref, lse_ref, *, window_blocks):
    i = pl.program_id(2)
    qb = kp_ref.shape[0]
    blocks = q_ref.shape[0] // qb
    qi = lax.broadcasted_iota(jnp.int32, (qb, 2 * qb), 0)
    kj = lax.broadcasted_iota(jnp.int32, (qb, 2 * qb), 1)
    band = (kj <= qi + qb) & (kj >= qi + qb - window_blocks)
    for j in range(blocks):
        rows = slice(j * qb, (j + 1) * qb)
        live = band & ((i > 0) | (kj >= qb)) if j == 0 else band
        lses = []
        for h in range(q_ref.shape[1] // HEAD_DIM):
            cols = slice(h * HEAD_DIM, (h + 1) * HEAD_DIM)
            k_prev = kp_ref[:, cols] if j == 0 else kc_ref[(j - 1) * qb:j * qb, cols]
            v_prev = vp_ref[:, cols] if j == 0 else vc_ref[(j - 1) * qb:j * qb, cols]
            q = q_ref[rows, cols].astype(BF16)
            k = jnp.concatenate([k_prev, kc_ref[rows, cols]], axis=0).astype(BF16)
            v = jnp.concatenate([v_prev, vc_ref[rows, cols]], axis=0).astype(BF16)
            s = lax.dot_general(q, k, NT_DIMS, preferred_element_type=F32) * ATTN_SCALE
            s = jnp.where(live, s, NEG_INF)
            m = jnp.max(s, axis=-1, keepdims=True)
            p = jnp.exp(s - m)
            l = jnp.sum(p, axis=-1, keepdims=True)
            o_ref[rows, cols] = jnp.dot(p.astype(BF16), v, preferred_element_type=F32) / l
            lses.append(m + jnp.log(l))
        lse_ref[rows, :] = _spread_heads(lses, lse_ref.shape[1])


def attn_prompt_group(qkv, group):
    batch, dil, n_strided, hd3 = qkv.shape
    hd = hd3 // 3
    win = WINDOWS[group]
    step_rows = ATTN_STEP_BLOCKS * ATTN_BLOCK
    assert dil == DILATIONS[group] and n_strided % step_rows == 0 and win // dil <= ATTN_BLOCK
    blk = (None, None, step_rows, hd)

    def cur(kind):
        return pl.BlockSpec(blk, lambda b, r, i: (b, r, i, kind))

    def prev(kind):
        return pl.BlockSpec((None, None, ATTN_BLOCK, hd),
                            lambda b, r, i: (b, r, jnp.maximum(i * ATTN_STEP_BLOCKS - 1, 0), kind))

    return pl.pallas_call(
        functools.partial(_attn_prompt_kernel, window_blocks=win // dil),
        grid=(batch, dil, n_strided // step_rows),
        in_specs=[cur(0), cur(1), prev(1), cur(2), prev(2)],
        out_specs=[pl.BlockSpec(blk, lambda b, r, i: (b, r, i, 0)),
                   pl.BlockSpec((None, None, step_rows, V7X_LANES), lambda b, r, i: (b, r, i, 0))],
        out_shape=[jax.ShapeDtypeStruct((batch, dil, n_strided, hd), F32),
                   jax.ShapeDtypeStruct((batch, dil, n_strided, V7X_LANES), F32)],
        compiler_params=_params("arbitrary", "arbitrary", "arbitrary"),
        name=f"attn_prompt_g{group}",
    )(qkv, qkv, qkv, qkv, qkv)


def _head_segments(n_heads, transpose):
    shape = (V7X_LANES, n_heads * HEAD_DIM) if transpose else (n_heads * HEAD_DIM, V7X_LANES)
    c = lax.broadcasted_iota(jnp.int32, shape, 1 if transpose else 0)
    h = lax.broadcasted_iota(jnp.int32, shape, 0 if transpose else 1)
    return jnp.where((c >= h * HEAD_DIM) & (c < (h + 1) * HEAD_DIM), 1.0, 0.0).astype(BF16)


def _attn_sample_kernel(n0_ref, n1_ref, n2_ref, c0_ref, c1_ref, c2_ref, o_ref, lse_ref):
    t = pl.program_id(1)
    for g, (new_ref, cache_ref) in enumerate(zip((n0_ref, n1_ref, n2_ref), (c0_ref, c1_ref, c2_ref))):
        q = new_ref[pl.ds(t, 1), 0]
        k_new, v_new = new_ref[:, 1], new_ref[:, 2]
        k_old, v_old = cache_ref[:, 0], cache_ref[:, 1]
        s_old = jnp.sum(k_old * q, axis=-1, keepdims=True) * ATTN_SCALE
        s_new = jnp.sum(k_new * q, axis=-1, keepdims=True) * ATTN_SCALE
        row_n = lax.broadcasted_iota(jnp.int32, s_new.shape, 0)
        if DILATIONS[g] == 1:
            row_c = lax.broadcasted_iota(jnp.int32, s_old.shape, 0)
            s_old = jnp.where(row_c >= t, s_old, NEG_INF)
            s_new = jnp.where(row_n <= t, s_new, NEG_INF)
        else:
            s_new = jnp.where(row_n == t, s_new, NEG_INF)
        m = jnp.maximum(jnp.max(s_old, axis=0, keepdims=True), jnp.max(s_new, axis=0, keepdims=True))
        p_old, p_new = jnp.exp(s_old - m), jnp.exp(s_new - m)
        l = jnp.sum(p_old, axis=0, keepdims=True) + jnp.sum(p_new, axis=0, keepdims=True)
        o = jnp.sum(p_old * v_old, axis=0, keepdims=True) + jnp.sum(p_new * v_new, axis=0, keepdims=True)
        o_ref[g, pl.ds(t, 1)] = o / l
        lse_ref[g, pl.ds(t, 1)] = m + jnp.log(l)


def attn_sample(qkv, caches, *, batch, t_new):
    n_heads = caches[0].shape[3]
    views, specs = [], []
    for g, cache in enumerate(caches):
        dil, keep = DILATIONS[g], cache.shape[1]
        assert keep == WINDOWS[g] and keep // dil == ATTN_BLOCK and (dil == 1 or t_new <= dil)
        views.append(cache.reshape(batch, keep // dil, dil, 2, n_heads, HEAD_DIM))
        specs.append(pl.BlockSpec((None, ATTN_BLOCK, None, 2, n_heads, HEAD_DIM),
                                  (lambda b, t: (b, 0, 0, 0, 0, 0)) if dil == 1 else
                                  (lambda b, t: (b, 0, t, 0, 0, 0))))
    rows_new = -(-t_new // V7X_SUBLANES) * V7X_SUBLANES
    new_rows = [jnp.pad(a.reshape(batch, t_new, 3, n_heads, HEAD_DIM),
                        ((0, 0), (0, rows_new - t_new), (0, 0), (0, 0), (0, 0))) for a in qkv]
    o, lse = pl.pallas_call(
        _attn_sample_kernel,
        grid=(batch, t_new),
        in_specs=[pl.BlockSpec((None, rows_new, 3, n_heads, HEAD_DIM), lambda b, t: (b, 0, 0, 0, 0))] * N_GROUPS
        + specs,
        out_specs=[pl.BlockSpec((N_GROUPS, None, t_new, n_heads, HEAD_DIM), lambda b, t: (0, b, 0, 0, 0)),
                   pl.BlockSpec((N_GROUPS, None, t_new, n_heads, 1), lambda b, t: (0, b, 0, 0, 0))],
        out_shape=[jax.ShapeDtypeStruct((N_GROUPS, batch, t_new, n_heads, HEAD_DIM), F32),
                   jax.ShapeDtypeStruct((N_GROUPS, batch, t_new, n_heads, 1), F32)],
        compiler_params=_params("arbitrary", "arbitrary"),
        name="attn_sample",
    )(*new_rows, *views)
    lse = jnp.pad(lse.reshape(N_GROUPS, batch * t_new, n_heads), ((0, 0), (0, 0), (0, V7X_LANES - n_heads)))
    return o.reshape(N_GROUPS, batch * t_new, n_heads * HEAD_DIM), lse


def _attn_out_kernel(x_ref, o0_ref, o1_ref, o2_ref, l0_ref, l1_ref, l2_ref, wo_ref, y_ref,
                     og_ref, lg_ref, mix_ref):
    tm = x_ref.shape[0]
    hd = wo_ref.shape[0]
    heads = hd // HEAD_DIM
    for g, (o_ref, l_ref) in enumerate(zip((o0_ref, o1_ref, o2_ref), (l0_ref, l1_ref, l2_ref))):
        dil = o_ref.shape[0]
        for res in range(dil):
            rows = pl.ds(res, tm // dil, stride=dil) if dil > 1 else slice(0, tm)
            lg_ref[g, rows, :] = l_ref[res]
            for h in range(heads):
                og_ref[g, h, rows, :] = o_ref[res, :, h * HEAD_DIM:(h + 1) * HEAD_DIM]
    lse = [lg_ref[g] for g in range(N_GROUPS)]
    m = functools.reduce(jnp.maximum, lse)
    e = [jnp.exp(l - m) for l in lse]
    tot = functools.reduce(jnp.add, e)
    seg_t = _head_segments(heads, transpose=True)
    w = [_split_dot(e[g] / tot, seg_t) for g in range(N_GROUPS)]
    for h in range(heads):
        cols = slice(h * HEAD_DIM, (h + 1) * HEAD_DIM)
        mix = functools.reduce(jnp.add, [w[g][:, cols] * og_ref[g, h] for g in range(N_GROUPS)])
        mix_ref[:, cols] = mix.astype(BF16)
    y_ref[...] = x_ref[...] + jnp.dot(mix_ref[...], wo_ref[...], preferred_element_type=F32)


def attn_out(x, outs, lses, w_o, *, tm):
    n, d = x.shape
    batch = outs[0].shape[0]
    hd = outs[0].shape[3]
    tiles = n // batch // tm

    def by_residue(a):
        dil = a.shape[1]
        return pl.BlockSpec((None, dil, tm // dil, a.shape[3]), lambda i: (i // tiles, 0, i % tiles, 0))

    return pl.pallas_call(
        _attn_out_kernel,
        grid=(n // tm,),
        in_specs=[pl.BlockSpec((tm, d), lambda i: (i, 0))] + [by_residue(a) for a in (*outs, *lses)]
        + [pl.BlockSpec((hd, d), lambda i: (0, 0))],
        out_specs=pl.BlockSpec((tm, d), lambda i: (i, 0)),
        out_shape=jax.ShapeDtypeStruct((n, d), F32),
        scratch_shapes=[pltpu.VMEM((N_GROUPS, hd // HEAD_DIM, tm, HEAD_DIM), F32),
                        pltpu.VMEM((N_GROUPS, tm, V7X_LANES), F32), pltpu.VMEM((tm, hd), BF16)],
        compiler_params=_params("arbitrary"),
        name="attn_out",
    )(x, *outs, *lses, w_o)


def _kv_state_kernel(c_ref, next_ref, n_ref, o_ref):
    i = pl.program_id(1)
    last = pl.num_programs(1) - 1
    rows, t_new = o_ref.shape[0], n_ref.shape[0]
    o_ref[0:rows - t_new] = c_ref[t_new:rows]
    o_ref[rows - t_new:rows] = jnp.where(i == last, n_ref[...], next_ref[0:t_new])


def kv_state_update(cache, new, *, rows):
    batch, keep = cache.shape[:2]
    t_new = new.shape[1]
    tail = cache.shape[2:]
    blocks = keep // rows
    head_rows = V7X_SUBLANES
    assert keep % rows == 0 and rows % head_rows == 0 and t_new <= head_rows < rows and new.shape[2:] == tail
    zeros = (0,) * len(tail)
    heads_per_block = rows // head_rows

    def next_head(b, i):
        return (b, jnp.minimum((i + 1) * heads_per_block, keep // head_rows - 1)) + zeros

    return pl.pallas_call(
        _kv_state_kernel,
        grid=(batch, blocks),
        in_specs=[pl.BlockSpec((None, rows) + tail, lambda b, i: (b, i) + zeros),
                  pl.BlockSpec((None, head_rows) + tail, next_head),
                  pl.BlockSpec((None, t_new) + tail, lambda b, i: (b, 0) + zeros)],
        out_specs=pl.BlockSpec((None, rows) + tail, lambda b, i: (b, i) + zeros),
        out_shape=jax.ShapeDtypeStruct(cache.shape, cache.dtype),
        compiler_params=_params("arbitrary", "arbitrary"),
        name="kv_state_update",
    )(cache, cache, new)


def _rope_tables(pos):
    half = HEAD_DIM // 2
    inv_freq = ROPE_THETA ** (-jnp.arange(half, dtype=F32) / half)
    ang = pos.astype(F32)[:, None] * inv_freq
    cos, sin = jnp.cos(ang), jnp.sin(ang)
    return jnp.concatenate([cos, cos], axis=-1), jnp.concatenate([-sin, sin], axis=-1)


def _pad_rows(a, rows):
    return jnp.pad(a, ((0, rows - a.shape[0]),) + ((0, 0),) * (a.ndim - 1))


PROMPT_TM = 512
ATTN_STEP_BLOCKS = 2
ATTN_OUT_TM = 256
QKV_TM = 256
PEER_TE = 1024
CONV_TN = 1024
SAMPLE_PEER_ROWS = V7X_LANES


def kernel(x_prompt, x_sample, state_conv, cache_kv_w128, cache_kv_w512, cache_kv_w2048,
           norm_mix, norm_ffn, conv_w_in, conv_w, conv_w_out,
           attn_w_qkv, attn_q_norm, attn_k_norm, attn_w_o,
           peer_w_q, peer_sub_keys, peer_u, peer_v):
    batch, seq, d = x_prompt.shape
    dec_batch, dec_seq, _ = x_sample.shape
    n_p, n_s = batch * seq, dec_batch * dec_seq
    caches = (cache_kv_w128, cache_kv_w512, cache_kv_w2048)
    n_heads = caches[0].shape[3]
    hd = n_heads * HEAD_DIM

    w_in, w_out = conv_w_in.astype(BF16), conv_w_out.astype(BF16)
    w_qkv, w_o = attn_w_qkv.astype(BF16), attn_w_o.astype(BF16)
    w_q = peer_w_q.astype(BF16)
    keys = peer_sub_keys.astype(BF16).reshape(peer_sub_keys.shape[0], 2 * PEER_HEADS, PEER_N_KEYS, PEER_D_HALF)
    pu, pv = peer_u.astype(BF16), peer_v.astype(BF16)
    g_mix = norm_mix.reshape(-1, 1, d)
    g_ffn = norm_ffn.reshape(-1, 1, d)

    xp = x_prompt.reshape(n_p, d)
    xs = x_sample.reshape(n_s, d)

    def peer(x, layer, tm):
        return peer_layer(x, g_ffn[layer], w_q, keys, pu, pv, layer, tm=tm, te=PEER_TE)

    def peer_sample(x, layer):
        return peer(_pad_rows(x, SAMPLE_PEER_ROWS), layer, SAMPLE_PEER_ROWS)[:n_s]

    bp, up = conv_in(xp, g_mix[0], w_in, tm=PROMPT_TM, tn=CONV_TN)
    xp = conv_out_prompt(xp, bp, up, jnp.zeros((batch, CONV_WIDTH - 1, d), F32), conv_w, w_out,
                         seq_len=seq, tm=PROMPT_TM)
    conv_state_prompt = up.reshape(batch, seq, d)[:, seq - (CONV_WIDTH - 1):]

    bs, us = conv_in(xs, g_mix[0], w_in, tm=n_s, tn=CONV_TN)
    u_ext = jnp.concatenate([state_conv, us.reshape(dec_batch, dec_seq, d)], axis=1)
    shifted = [u_ext[:, k:k + dec_seq].reshape(n_s, d) for k in range(CONV_WIDTH)]
    xs = conv_out_sample(xs, bs, shifted[0], shifted[1], shifted[2], conv_w, w_out)
    conv_state_sample = u_ext[:, dec_seq:]

    xp = peer(xp, 0, PROMPT_TM)
    xs = peer_sample(xs, 0)

    qkv_p, kv_prompt = zip(*[
        qkv_project(xp, g_mix[1], w_qkv, attn_q_norm, attn_k_norm, np.arange(seq), g,
                    dil=DILATIONS[g], batch=batch, tm=QKV_TM, keep=min(WINDOWS[g], seq)) for g in range(N_GROUPS)])
    outs, lses = zip(*[attn_prompt_group(qkv_p[g], g) for g in range(N_GROUPS)])
    xp = attn_out(xp, outs, lses, w_o, tm=ATTN_OUT_TM)

    pos_s = np.tile(PAST_LEN + np.arange(dec_seq), dec_batch)
    q_s, kv_s = zip(*[qkv_project(xs, g_mix[1], w_qkv, attn_q_norm, attn_k_norm, pos_s, g,
                                  dil=1, batch=1, tm=n_s, keep=n_s) for g in range(N_GROUPS)])
    kv_s = [kv.reshape(n_s, 2 * hd) for kv in kv_s]
    qkv_s = [jnp.concatenate([q.reshape(n_s, 3 * hd)[:, :hd].astype(F32), kv], axis=1) for q, kv in zip(q_s, kv_s)]
    o_s, lse_s = attn_sample(qkv_s, caches, batch=dec_batch, t_new=dec_seq)
    xs = attn_out(xs, [o_s[g].reshape(1, 1, n_s, hd) for g in range(N_GROUPS)],
                  [lse_s[g].reshape(1, 1, n_s, V7X_LANES) for g in range(N_GROUPS)], w_o, tm=n_s)

    xp = peer(xp, 1, PROMPT_TM)
    xs = peer_sample(xs, 1)

    kv_prompt = [kv.reshape(batch, kv.shape[1], 2, n_heads, HEAD_DIM) for kv in kv_prompt]
    kv_sample = [kv_state_update(cache, kv.reshape(dec_batch, dec_seq, 2, n_heads, HEAD_DIM), rows=ATTN_BLOCK)
                 for cache, kv in zip(caches, kv_s)]

    return (xp.reshape(batch, seq, d), xs.reshape(dec_batch, dec_seq, d),
            conv_state_prompt, conv_state_sample, *kv_prompt, *kv_sample)
```
